```python
import jax, jax.numpy as jnp
from jax import lax
import numpy as np

D_MODEL = 1024
BATCH = 16
SEQ = 256
DEPTH = 2
DEC_BATCH = 8
DEC_SEQ = 4096
PAST_LEN = 256

GRID_W = 64
HEAD_DIM = 64
N_HEADS_A = D_MODEL // 128
W_A = N_HEADS_A * HEAD_DIM
WIN_R = 8
WIN_C = 16
QC = 16
KBW = QC + WIN_C
CTX_QBLK = 128
N_GROUPS_B = 4
W_B = D_MODEL // 2
C_B = W_B // N_GROUPS_B
CHUNK = 128
D_RNN = 3 * D_MODEL // 4
RNN_BLOCKS = 8
RNN_BW = D_RNN // RNN_BLOCKS
CONV_W = 4
CONV_LEFT = 2
RG_C = 8.0
D_POOL = D_MODEL // 4
POOL_WINDOWS = (2, 4, 8, 16)
POOL_C = D_POOL // len(POOL_WINDOWS)
D_FF = ((8 * D_MODEL // 3 + 127) // 128) * 128
N_EXPERTS = 8
TOP_K = 2
N_EVEN = (DEPTH + 1) // 2
N_ODD = DEPTH // 2
D_IN_EVEN = 3 * W_A + 2 * W_B
D_IN_ODD = 2 * D_RNN + D_POOL
EPS = 1e-6
NEG = -1e30

kernel_name = 'hybrid_na_sgu_rglru_pool_moe_step'


def rmsnorm(x, g):
    xf = x.astype(jnp.float32)
    y = xf * lax.rsqrt(jnp.mean(xf * xf, axis=-1, keepdims=True) + EPS)
    return (y * g.astype(jnp.float32)).astype(x.dtype)


def modulation(cond, w_ada, b_ada):
    m = jax.nn.silu(cond) @ w_ada + b_ada
    return m.reshape(cond.shape[0], 6, D_MODEL)


def pre(x, g, shift, scale):
    return rmsnorm(x, g) * (1 + scale[:, None]) + shift[:, None]


def post(x, y, g, gate):
    return x + gate[:, None] * rmsnorm(y, g)


def ctx_attention(q, k, v):
    B, N, H, hd = q.shape
    nb = N // CTX_QBLK
    scale = hd ** -0.5
    qb = q.reshape(B, nb, CTX_QBLK, H, hd).transpose(1, 0, 2, 3, 4)

    def one_block(qi):
        s = jnp.einsum('bqhd,bkhd->bhqk', qi, k).astype(jnp.float32) * scale
        p = jax.nn.softmax(s, axis=-1).astype(v.dtype)
        return jnp.einsum('bhqk,bkhd->bqhd', p, v)

    o = lax.map(one_block, qb)
    return o.transpose(1, 0, 2, 3, 4).reshape(B, N, H * hd)


def na_latent(q, k, v, k_ctx, v_ctx, rpb):
    B, N, H, hd = q.shape
    rows = N // GRID_W
    kr = min(WIN_R, rows)
    nqb = GRID_W // QC
    scale = hd ** -0.5
    r = jnp.arange(rows)
    rs = jnp.clip(r - kr // 2, 0, rows - kr)
    key_rows = rs[:, None] + jnp.arange(kr)[None, :]
    qb = jnp.arange(nqb)
    kcs = jnp.clip(qb * QC - WIN_C // 2, 0, GRID_W - KBW)
    key_cols = kcs[:, None] + jnp.arange(KBW)[None, :]
    qcol = qb[:, None] * QC + jnp.arange(QC)[None, :]
    cs = jnp.clip(qcol - WIN_C // 2, 0, GRID_W - WIN_C)
    kc = key_cols[:, None, :]
    col_valid = (kc >= cs[..., None]) & (kc < cs[..., None] + WIN_C)
    dr = key_rows - r[:, None] + (WIN_R - 1)
    dc = jnp.clip(kc - qcol[..., None] + (WIN_C - 1), 0, 2 * WIN_C - 2)
    bias = rpb[:, dr[:, None, None, :, None], dc[None, :, :, None, :]].astype(jnp.float32)
    bias = bias.transpose(1, 2, 0, 3, 4, 5)
    kl = k.reshape(B, rows, GRID_W, H, hd)
    vl = v.reshape(B, rows, GRID_W, H, hd)
    ri = key_rows[:, None, :, None]
    ci = key_cols[None, :, None, :]
    kg = kl[:, ri, ci]
    vg = vl[:, ri, ci]
    qg = q.reshape(B, rows, nqb, QC, H, hd)
    s_loc = jnp.einsum('brgqhd,brgkwhd->brghqkw', qg, kg).astype(jnp.float32) * scale + bias[None]
    s_loc = jnp.where(col_valid[None, None, :, None, :, None, :], s_loc, NEG)
    s_ctx = jnp.einsum('brgqhd,bhcd->brghqc', qg, k_ctx).astype(jnp.float32) * scale
    n_loc = kr * KBW
    s = jnp.concatenate([s_loc.reshape(B, rows, nqb, H, QC, n_loc), s_ctx], axis=-1)
    p = jax.nn.softmax(s, axis=-1).astype(v.dtype)
    p_loc = p[..., :n_loc].reshape(B, rows, nqb, H, QC, kr, KBW)
    p_ctx = p[..., n_loc:]
    o = (jnp.einsum('brghqkw,brgkwhd->brgqhd', p_loc, vg)
         + jnp.einsum('brghqc,bhcd->brgqhd', p_ctx, v_ctx))
    return o.reshape(B, N, H * hd)


def spatial_gating(u, g, w_s, b_s):
    B, N, _ = u.shape
    u = jax.nn.gelu(u)
    gf = jax.nn.gelu(g).astype(jnp.float32).reshape(B, N // CHUNK, CHUNK, N_GROUPS_B, C_B)
    mu = jnp.mean(gf, axis=-1, keepdims=True)
    var = jnp.mean(jnp.square(gf - mu), axis=-1, keepdims=True)
    gn = ((gf - mu) * lax.rsqrt(var + EPS)).astype(u.dtype)
    s = jnp.einsum('gpq,bnqgc->bnpgc', w_s, gn) + b_s.T[None, None, :, :, None]
    return u * s.reshape(B, N, W_B)


def even_split(h, w_in):
    B, N, _ = h.shape
    z = h @ w_in
    q, k, v, u, g = jnp.split(z, [W_A, 2 * W_A, 3 * W_A, 3 * W_A + W_B], axis=-1)
    shp = (B, N, N_HEADS_A, HEAD_DIM)
    return q.reshape(shp), k.reshape(shp), v.reshape(shp), u, g


def even_mixer_ctx(h, w_in, w_out, w_s, b_s):
    q, k, v, u, g = even_split(h, w_in)
    o = jnp.concatenate([ctx_attention(q, k, v), spatial_gating(u, g, w_s, b_s)], axis=-1) @ w_out
    return o, k.transpose(0, 2, 1, 3), v.transpose(0, 2, 1, 3)


def even_mixer_lat(h, k_ctx, v_ctx, w_in, w_out, rpb, w_s, b_s):
    q, k, v, u, g = even_split(h, w_in)
    o_a = na_latent(q, k, v, k_ctx, v_ctx, rpb)
    return jnp.concatenate([o_a, spatial_gating(u, g, w_s, b_s)], axis=-1) @ w_out


def dwconv_centred(x, w, b):
    N = x.shape[1]
    xp = jnp.pad(x, ((0, 0), (CONV_LEFT, CONV_W - 1 - CONV_LEFT), (0, 0)))
    y = b
    for j in range(CONV_W):
        y = y + w[j] * xp[:, j:j + N]
    return y


def block_diag(x, w, b):
    B, N, _ = x.shape
    y = jnp.einsum('bnki,kij->bnkj', x.reshape(B, N, RNN_BLOCKS, RNN_BW), w)
    return y.reshape(B, N, D_RNN) + b


def rglru_scan(x, h0, wa, ba, wx, bx, lam, reverse):
    xf = x.astype(jnp.float32)
    rg = jax.nn.sigmoid(block_diag(x, wa, ba).astype(jnp.float32))
    ig = jax.nn.sigmoid(block_diag(x, wx, bx).astype(jnp.float32))
    log_a = -RG_C * rg * jax.nn.softplus(-lam.astype(jnp.float32))
    a = jnp.exp(log_a)
    b = jnp.sqrt(-jnp.expm1(2.0 * log_a)) * (ig * xf)

    def combine(e1, e2):
        return (e1[0] * e2[0], e2[0] * e1[1] + e2[1])

    A, Bc = lax.associative_scan(combine, (a, b), axis=1, reverse=reverse)
    return A * h0[:, None, :] + Bc


def multiscale_pool(x, pool_w, pool_scale):
    B, N, _ = x.shape
    xf = x.astype(jnp.float32)
    csum = jnp.concatenate([jnp.zeros((B, 1, D_POOL), jnp.float32), jnp.cumsum(xf, axis=1)], axis=1)
    t = jnp.arange(N)
    outs = []
    for gi, w in enumerate(POOL_WINDOWS):
        sl = slice(gi * POOL_C, (gi + 1) * POOL_C)
        lo = jnp.clip(t - w // 2, 0, N)
        hi = jnp.clip(t + w - w // 2, 0, N)
        cnt = (hi - lo).astype(jnp.float32)[None, :, None]
        mean = (csum[:, hi, sl] - csum[:, lo, sl]) / cnt
        outs.append(jnp.einsum('bnc,cd->bnd', (mean - xf[..., sl]).astype(x.dtype), pool_w[gi]))
    return jnp.concatenate(outs, axis=-1) * pool_scale


def odd_mixer(h, h0, w_in, w_out, conv_w, conv_b, wa, ba, wx, bx, lam, pool_w, pool_scale):
    z = h @ w_in
    gate, xr, xq = jnp.split(z, [D_RNN, 2 * D_RNN], axis=-1)
    xr = dwconv_centred(xr, conv_w, conv_b)
    h_f = rglru_scan(xr, h0[:, 0], wa[0], ba[0], wx[0], bx[0], lam[0], False)
    h_b = rglru_scan(xr, h0[:, 1], wa[1], ba[1], wx[1], bx[1], lam[1], True)
    y_rec = (h_f + h_b).astype(h.dtype) * jax.nn.gelu(gate)
    y_pool = multiscale_pool(xq, pool_w, pool_scale)
    out = jnp.concatenate([y_rec, y_pool], axis=-1) @ w_out
    return out, h_f, h_b


def swiglu(x, wg, wu, wd):
    return (jax.nn.silu(x @ wg) * (x @ wu)) @ wd


def moe(x, router_w, wg, wu, wd):
    B, N, D = x.shape
    t = x.reshape(B * N, D)
    logits = (t @ router_w).astype(jnp.float32)
    topv, topi = lax.top_k(logits, TOP_K)
    w = jax.nn.softmax(topv, axis=-1)
    gates = jnp.sum(jax.nn.one_hot(topi, N_EXPERTS, dtype=jnp.float32) * w[..., None], axis=1)
    out = jnp.zeros_like(t)
    for e in range(N_EXPERTS):
        out = out + gates[:, e:e + 1].astype(x.dtype) * swiglu(t, wg[e], wu[e], wd[e])
    return out.reshape(B, N, D)


def setup_inputs(seed: int = 0) -> dict:
    key = jax.random.key(seed)
    ks = jax.random.split(key, 36)
    f32 = jnp.float32

    def nrm(k, shape, scale=1.0):
        return jax.random.normal(k, shape, f32) * scale

    lam_u = jax.random.uniform(ks[35], (N_ODD, 2, D_RNN), f32, 0.9, 0.999)
    return {
        'x_prompt': nrm(ks[0], (BATCH, SEQ, D_MODEL)),
        'x_sample': nrm(ks[1], (DEC_BATCH, DEC_SEQ, D_MODEL)),
        'cache_k': nrm(ks[2], (DEC_BATCH, N_EVEN, N_HEADS_A, PAST_LEN, HEAD_DIM)),
        'cache_v': nrm(ks[3], (DEC_BATCH, N_EVEN, N_HEADS_A, PAST_LEN, HEAD_DIM)),
        'state_h': nrm(ks[4], (DEC_BATCH, N_ODD, 2, D_RNN), 0.5),
        'c': nrm(ks[5], (DEC_BATCH, D_MODEL)),
        'c_ctx': nrm(ks[6], (D_MODEL,)),
        'w_ada': nrm(ks[7], (DEPTH, D_MODEL, 6 * D_MODEL), 0.5 * D_MODEL ** -0.5),
        'b_ada': nrm(ks[8], (DEPTH, 6 * D_MODEL), 0.02),
        'norm_g': 1.0 + nrm(ks[9], (DEPTH, 4, D_MODEL), 0.02),
        'w_in_even': nrm(ks[10], (N_EVEN, D_MODEL, D_IN_EVEN), D_MODEL ** -0.5),
        'w_out_even': nrm(ks[11], (N_EVEN, W_A + W_B, D_MODEL), (W_A + W_B) ** -0.5),
        'na_rpb': nrm(ks[12], (N_EVEN, N_HEADS_A, 2 * WIN_R - 1, 2 * WIN_C - 1), 0.1),
        'sgu_w': nrm(ks[13], (N_EVEN, N_GROUPS_B, CHUNK, CHUNK), CHUNK ** -0.5),
        'sgu_b': 1.0 + nrm(ks[14], (N_EVEN, N_GROUPS_B, CHUNK), 0.02),
        'w_in_odd': nrm(ks[15], (N_ODD, D_MODEL, D_IN_ODD), D_MODEL ** -0.5),
        'w_out_odd': nrm(ks[16], (N_ODD, D_RNN + D_POOL, D_MODEL), (D_RNN + D_POOL) ** -0.5),
        'conv_w': nrm(ks[17], (N_ODD, CONV_W, D_RNN), 0.5),
        'conv_b': nrm(ks[18], (N_ODD, D_RNN), 0.02),
        'rg_wa': nrm(ks[19], (N_ODD, 2, RNN_BLOCKS, RNN_BW, RNN_BW), RNN_BW ** -0.5),
        'rg_ba': nrm(ks[20], (N_ODD, 2, D_RNN), 0.02),
        'rg_wx': nrm(ks[21], (N_ODD, 2, RNN_BLOCKS, RNN_BW, RNN_BW), RNN_BW ** -0.5),
        'rg_bx': nrm(ks[22], (N_ODD, 2, D_RNN), 0.02),
        'rg_lam': jnp.log(lam_u) - jnp.log1p(-lam_u),
        'pool_w': nrm(ks[23], (N_ODD, len(POOL_WINDOWS), POOL_C, POOL_C), POOL_C ** -0.5),
        'pool_scale': 1.0 + nrm(ks[24], (N_ODD, D_POOL), 0.1),
        'ffn_wg': nrm(ks[25], (N_EVEN, D_MODEL, D_FF), D_MODEL ** -0.5),
        'ffn_wu': nrm(ks[26], (N_EVEN, D_MODEL, D_FF), D_MODEL ** -0.5),
        'ffn_wd': nrm(ks[27], (N_EVEN, D_FF, D_MODEL), D_FF ** -0.5),
        'router_w': nrm(ks[28], (N_ODD, D_MODEL, N_EXPERTS), D_MODEL ** -0.5),
        'moe_wg': nrm(ks[29], (N_ODD, N_EXPERTS, D_MODEL, D_FF), D_MODEL ** -0.5),
        'moe_wu': nrm(ks[30], (N_ODD, N_EXPERTS, D_MODEL, D_FF), D_MODEL ** -0.5),
        'moe_wd': nrm(ks[31], (N_ODD, N_EXPERTS, D_FF, D_MODEL), D_FF ** -0.5),
    }


def reference(x_prompt, x_sample, cache_k, cache_v, state_h, c, c_ctx, w_ada, b_ada, norm_g,
              w_in_even, w_out_even, na_rpb, sgu_w, sgu_b, w_in_odd, w_out_odd, conv_w, conv_b,
              rg_wa, rg_ba, rg_wx, rg_bx, rg_lam, pool_w, pool_scale, ffn_wg, ffn_wu, ffn_wd,
              router_w, moe_wg, moe_wu, moe_wd):
    xp = x_prompt
    xs = x_sample
    ks_list, vs_list, hs_list = [], [], []
    for i in range(DEPTH):
        j = i // 2
        mp = modulation(c_ctx[None, :], w_ada[i], b_ada[i])
        ms = modulation(c, w_ada[i], b_ada[i])
        g = norm_g[i]
        hp = pre(xp, g[0], mp[:, 0], mp[:, 1])
        hs = pre(xs, g[0], ms[:, 0], ms[:, 1])
        if i % 2 == 0:
            yp, kp, vp = even_mixer_ctx(hp, w_in_even[j], w_out_even[j], sgu_w[j], sgu_b[j])
            ys = even_mixer_lat(hs, cache_k[:, j], cache_v[:, j], w_in_even[j], w_out_even[j],
                                na_rpb[j], sgu_w[j], sgu_b[j])
            ks_list.append(kp)
            vs_list.append(vp)
        else:
            h0p = jnp.zeros((xp.shape[0], 2, D_RNN), jnp.float32)
            yp, hfp, hbp = odd_mixer(hp, h0p, w_in_odd[j], w_out_odd[j], conv_w[j], conv_b[j],
                                     rg_wa[j], rg_ba[j], rg_wx[j], rg_bx[j], rg_lam[j],
                                     pool_w[j], pool_scale[j])
            ys = odd_mixer(hs, state_h[:, j].astype(jnp.float32), w_in_odd[j], w_out_odd[j],
                           conv_w[j], conv_b[j], rg_wa[j], rg_ba[j], rg_wx[j], rg_bx[j], rg_lam[j],
                           pool_w[j], pool_scale[j])[0]
            hs_list.append(jnp.stack([hfp[:, -1], hbp[:, 0]], axis=1))
        xp = post(xp, yp, g[1], mp[:, 2])
        xs = post(xs, ys, g[1], ms[:, 2])
        hp = pre(xp, g[2], mp[:, 3], mp[:, 4])
        hs = pre(xs, g[2], ms[:, 3], ms[:, 4])
        if i % 2 == 0:
            yp = swiglu(hp, ffn_wg[j], ffn_wu[j], ffn_wd[j])
            ys = swiglu(hs, ffn_wg[j], ffn_wu[j], ffn_wd[j])
        else:
            yp = moe(hp, router_w[j], moe_wg[j], moe_wu[j], moe_wd[j])
            ys = moe(hs, router_w[j], moe_wg[j], moe_wu[j], moe_wd[j])
        xp = post(xp, yp, g[3], mp[:, 5])
        xs = post(xs, ys, g[3], ms[:, 5])
    new_k = jnp.stack(ks_list, axis=1)
    new_v = jnp.stack(vs_list, axis=1)
    new_h = jnp.stack(hs_list, axis=1)
    return (xp, xs, new_k, new_v, new_h)
```

```python
import functools

import jax
import jax.numpy as jnp
from jax import lax
from jax.experimental import pallas as pl
from jax.experimental.pallas import tpu as pltpu

F32 = jnp.float32
BF16 = jnp.bfloat16

D_MODEL = 1024
GRID_W = 64
HEAD_DIM = 64
N_HEADS = 8
W_A = N_HEADS * HEAD_DIM
WIN_R = 8
WIN_C = 16
N_GROUPS_B = 4
W_B = 512
CHUNK = 128
D_RNN = 768
RNN_BLOCKS = 8
RNN_BW = D_RNN // RNN_BLOCKS
RNN_HALF = D_RNN // 2
CONV_W = 4
CONV_LEFT = 2
RG_C = 8.0
D_POOL = 256
POOL_WINDOWS = (2, 4, 8, 16)
POOL_C = D_POOL // len(POOL_WINDOWS)
D_FF = 2816
N_EXPERTS = 8
EPS = 1e-6
NEG = -1e30

LANES = 128
SUBLANES = 8
VMEM_LIMIT = 56 * 1024 * 1024
Q_ROWS = 4
KEY_ROWS = Q_ROWS + WIN_R - 1
HALO = 8
FF_CHUNKS = 2
FF_CHUNK = D_FF // FF_CHUNKS


def _cparams(sem):
    return pltpu.CompilerParams(dimension_semantics=sem, vmem_limit_bytes=VMEM_LIMIT)


def _const_spec(shape):
    nd = len(shape)
    return pl.BlockSpec(shape, lambda *_: (0,) * nd, pipeline_mode=pl.Buffered(1))


def _dot(a, b):
    return jnp.dot(a, b, preferred_element_type=F32)


def _dot_nt(a, b):
    return lax.dot_general(a, b, (((1,), (1,)), ((), ())), preferred_element_type=F32)


def _rms(x, g):
    ms = jnp.mean(x * x, axis=-1, keepdims=True)
    return x * lax.rsqrt(ms + EPS) * g


def _pre(x, g, shift, scale):
    return _rms(x, g) * (1.0 + scale) + shift


def _gelu(x):
    return jax.nn.gelu(x, approximate=True)


def _sigmoid(x):
    return 1.0 / (1.0 + jnp.exp(-x))


def _silu(x):
    return x * _sigmoid(x)


def _mod_kernel(c_ref, w_ref, b_ref, o_ref):
    s = _silu(c_ref[...])
    o_ref[0] = jnp.dot(s, w_ref[0], preferred_element_type=F32,
                       precision=lax.Precision.HIGHEST) + b_ref[0]


def _modulation(cond, w_ada, b_ada):
    depth, d, n = w_ada.shape
    rows = cond.shape[0]
    bn = 768
    return pl.pallas_call(
        _mod_kernel,
        grid=(depth, n // bn),
        in_specs=[
            pl.BlockSpec((rows, d), lambda i, j: (0, 0)),
            pl.BlockSpec((1, d, bn), lambda i, j: (i, 0, j)),
            pl.BlockSpec((1, 1, bn), lambda i, j: (i, 0, j)),
        ],
        out_specs=pl.BlockSpec((1, rows, bn), lambda i, j: (i, 0, j)),
        out_shape=jax.ShapeDtypeStruct((depth, rows, n), F32),
        compiler_params=_cparams(("arbitrary", "arbitrary")),
        name="adaln_modulation",
    )(cond, w_ada, b_ada.reshape(depth, 1, n))


def _l0_in_kernel(x_ref, mod_ref, g_ref, w_ref, ws_ref, bs_ref, *out_refs, tm, emit_f32):
    q_ref, k_ref, v_ref, sg_ref = out_refs[:4]
    x = x_ref[...]
    h = _pre(x, g_ref[0:1, :], mod_ref[0, 0:1, :], mod_ref[0, 1:2, :]).astype(BF16)
    z = _dot(h, w_ref[...])
    q_ref[...] = z[:, 0:W_A].astype(BF16)
    k_ref[...] = z[:, W_A:2 * W_A].astype(BF16)
    v_ref[...] = z[:, 2 * W_A:3 * W_A].astype(BF16)
    if emit_f32:
        out_refs[4][...] = z[:, W_A:2 * W_A]
        out_refs[5][...] = z[:, 2 * W_A:3 * W_A]
    u = _gelu(z[:, 3 * W_A:3 * W_A + W_B])
    gf = _gelu(z[:, 3 * W_A + W_B:])
    n_chunks = tm // CHUNK
    for gi in range(N_GROUPS_B):
        gg = gf[:, gi * LANES:(gi + 1) * LANES]
        mu = jnp.mean(gg, axis=-1, keepdims=True)
        dd = gg - mu
        var = jnp.mean(dd * dd, axis=-1, keepdims=True)
        gn = (dd * lax.rsqrt(var + EPS)).astype(BF16)
        rhs = jnp.concatenate([gn[c * CHUNK:(c + 1) * CHUNK, :] for c in range(n_chunks)], axis=1)
        s = _dot(ws_ref[gi], rhs) + bs_ref[:, gi:gi + 1]
        for c in range(n_chunks):
            uu = u[c * CHUNK:(c + 1) * CHUNK, gi * LANES:(gi + 1) * LANES]
            sg_ref[c * CHUNK:(c + 1) * CHUNK, gi * LANES:(gi + 1) * LANES] = (
                uu * s[:, c * LANES:(c + 1) * LANES]).astype(BF16)


def _l0_in(x, mod, g, w_in, w_s, b_s_t, *, tm, rows_per_batch, emit_f32):
    rows = x.shape[0]
    tpb = rows_per_batch // tm
    nb = mod.shape[0]
    d_in = w_in.shape[1]
    row_spec = lambda w: pl.BlockSpec((tm, w), lambda i: (i, 0))
    out_shape = [jax.ShapeDtypeStruct((rows, W_A), BF16)] * 3 + [jax.ShapeDtypeStruct((rows, W_B), BF16)]
    out_specs = [row_spec(W_A)] * 3 + [row_spec(W_B)]
    if emit_f32:
        out_shape += [jax.ShapeDtypeStruct((rows, W_A), F32)] * 2
        out_specs += [row_spec(W_A)] * 2
    return pl.pallas_call(
        functools.partial(_l0_in_kernel, tm=tm, emit_f32=emit_f32),
        grid=(rows // tm,),
        in_specs=[
            row_spec(D_MODEL),
            pl.BlockSpec((1, 6, D_MODEL), lambda i: ((i // tpb) % nb, 0, 0)),
            _const_spec((4, D_MODEL)),
            _const_spec((D_MODEL, d_in)),
            _const_spec((N_GROUPS_B, CHUNK, CHUNK)),
            _const_spec((CHUNK, N_GROUPS_B)),
        ],
        out_specs=out_specs,
        out_shape=out_shape,
        compiler_params=_cparams(("arbitrary",)),
        name="l0_in_proj_sgu",
    )(x, mod, g, w_in, w_s, b_s_t)


def _attn_kernel(*refs, n_local, has_ctx, rows):
    if has_ctx:
        q_ref, k_ref, v_ref, kc_ref, vc_ref, bias_ref, o_ref = refs
        r0 = pl.program_id(2) * Q_ROWS
        ks = jnp.clip(r0 - WIN_R // 2, 0, rows - KEY_ROWS)
        start = pl.multiple_of(ks * GRID_W, GRID_W)
        kl = k_ref[pl.ds(start, n_local), :]
        vl = v_ref[pl.ds(start, n_local), :]
    else:
        q_ref, k_ref, v_ref, o_ref = refs
        kl = k_ref[...]
        vl = v_ref[...]
    q = q_ref[...] * jnp.asarray(HEAD_DIM ** -0.5, BF16)
    first = lax.broadcasted_iota(jnp.int32, (1, LANES), 1) < HEAD_DIM
    outs = []
    for half in range(2):
        qh = jnp.where(first if half == 0 else jnp.logical_not(first), q, jnp.zeros_like(q))
        s = _dot_nt(qh, kl)
        if has_ctx:
            s = s + bias_ref[0, half]
            sc = _dot_nt(qh, kc_ref[...])
            m = jnp.maximum(jnp.max(s, axis=-1, keepdims=True), jnp.max(sc, axis=-1, keepdims=True))
        else:
            m = jnp.max(s, axis=-1, keepdims=True)
        p = jnp.exp(s - m)
        l = jnp.sum(p, axis=-1, keepdims=True)
        acc = _dot(p.astype(BF16), vl)
        if has_ctx:
            pc = jnp.exp(sc - m)
            l = l + jnp.sum(pc, axis=-1, keepdims=True)
            acc = acc + _dot(pc.astype(BF16), vc_ref[...])
        outs.append(acc / l)
    o_ref[...] = jnp.where(first, outs[0], outs[1]).astype(BF16)


def _attn_ctx(q, k, v, *, n_batch, n_seq):
    rows = q.shape[0]
    spec = pl.BlockSpec((n_seq, LANES), lambda b, hp: (b, hp))
    return pl.pallas_call(
        functools.partial(_attn_kernel, n_local=n_seq, has_ctx=False, rows=0),
        grid=(n_batch, W_A // LANES),
        in_specs=[spec, spec, spec],
        out_specs=spec,
        out_shape=jax.ShapeDtypeStruct((rows, W_A), BF16),
        compiler_params=_cparams(("arbitrary", "arbitrary")),
        name="attn_context",
    )(q, k, v)


def _attn_latent(q, k, v, kc, vc, bias, *, n_batch, n_seq, n_ctx):
    rows = n_seq // GRID_W
    n_rg = rows // Q_ROWS
    tq = Q_ROWS * GRID_W
    n_local = KEY_ROWS * GRID_W
    img_spec = pl.BlockSpec((n_seq, LANES), lambda b, hp, rg: (b, hp))
    ctx_spec = pl.BlockSpec((n_ctx, LANES), lambda b, hp, rg: (b, hp))
    q_spec = pl.BlockSpec((tq, LANES), lambda b, hp, rg: (b * n_rg + rg, hp))

    def bias_map(b, hp, rg):
        cfg = jnp.where(rg == 0, 0, jnp.where(rg == n_rg - 1, 2, 1))
        return (cfg, hp, 0, 0)

    return pl.pallas_call(
        functools.partial(_attn_kernel, n_local=n_local, has_ctx=True, rows=rows),
        grid=(n_batch, W_A // LANES, n_rg),
        in_specs=[q_spec, img_spec, img_spec, ctx_spec, ctx_spec,
                  pl.BlockSpec((1, 2, tq, n_local), bias_map)],
        out_specs=q_spec,
        out_shape=jax.ShapeDtypeStruct((n_batch * n_seq, W_A), BF16),
        compiler_params=_cparams(("arbitrary", "arbitrary", "arbitrary")),
        name="attn_latent",
    )(q, k, v, kc, vc, bias)


def _latent_bias_table(rpb, rows):
    tables = []
    for r0 in (0, 2 * Q_ROWS, rows - Q_ROWS):
        ks = min(max(r0 - WIN_R // 2, 0), rows - KEY_ROWS)
        qrow = r0 + jnp.arange(Q_ROWS)[:, None, None, None]
        qcol = jnp.arange(GRID_W)[None, :, None, None]
        krow = ks + jnp.arange(KEY_ROWS)[None, None, :, None]
        kcol = jnp.arange(GRID_W)[None, None, None, :]
        rs = jnp.clip(qrow - WIN_R // 2, 0, rows - WIN_R)
        cs = jnp.clip(qcol - WIN_C // 2, 0, GRID_W - WIN_C)
        valid = (krow >= rs) & (krow < rs + WIN_R) & (kcol >= cs) & (kcol < cs + WIN_C)
        dr = jnp.clip(krow - qrow + (WIN_R - 1), 0, 2 * WIN_R - 2)
        dc = jnp.clip(kcol - qcol + (WIN_C - 1), 0, 2 * WIN_C - 2)
        dr, dc, valid = jnp.broadcast_arrays(dr, dc, valid)
        b = jnp.where(valid[None], rpb[:, dr, dc].astype(F32), NEG)
        tables.append(b.reshape(rpb.shape[0], Q_ROWS * GRID_W, KEY_ROWS * GRID_W))
    return jnp.stack(tables)


def _out_kernel(*refs, n_in, router):
    in_refs = refs[:n_in]
    x_ref, mod_ref, g_ref, w_ref = refs[n_in:n_in + 4]
    rest = refs[n_in + 4:]
    if router:
        rw_ref, x1_ref, h2_ref, gates_ref = rest
    else:
        x1_ref, h2_ref = rest
    y = None
    off = 0
    for r in in_refs:
        w = r.shape[1]
        part = _dot(r[...], w_ref[off:off + w, :])
        y = part if y is None else y + part
        off += w
    x1 = x_ref[...] + mod_ref[0, 2:3, :] * _rms(y, g_ref[1:2, :])
    x1_ref[...] = x1
    h2 = _pre(x1, g_ref[2:3, :], mod_ref[0, 3:4, :], mod_ref[0, 4:5, :])
    h2_ref[...] = h2.astype(BF16)
    if router:
        logits = jnp.dot(h2, rw_ref[...], preferred_element_type=F32,
                         precision=lax.Precision.HIGHEST)
        lane = lax.broadcasted_iota(jnp.int32, logits.shape, 1).astype(F32)
        logits = jnp.where(lane < N_EXPERTS, logits, -jnp.inf)
        m1 = jnp.max(logits, axis=-1, keepdims=True)
        i1 = jnp.min(jnp.where(logits == m1, lane, float(LANES)), axis=-1, keepdims=True)
        rest_l = jnp.where(lane == i1, -jnp.inf, logits)
        m2 = jnp.max(rest_l, axis=-1, keepdims=True)
        i2 = jnp.min(jnp.where(rest_l == m2, lane, float(LANES)), axis=-1, keepdims=True)
        e2 = jnp.exp(m2 - m1)
        w1 = 1.0 / (1.0 + e2)
        w2 = e2 / (1.0 + e2)
        gates_ref[...] = jnp.where(lane == i1, w1, 0.0) + jnp.where(lane == i2, w2, 0.0)


def _out_proj(ins, x, mod, g, w_out, router_w, *, tm, rows_per_batch):
    rows = x.shape[0]
    tpb = rows_per_batch // tm
    nb = mod.shape[0]
    router = router_w is not None
    row_spec = lambda w: pl.BlockSpec((tm, w), lambda i: (i, 0))
    in_specs = [row_spec(a.shape[1]) for a in ins] + [
        row_spec(D_MODEL),
        pl.BlockSpec((1, 6, D_MODEL), lambda i: ((i // tpb) % nb, 0, 0)),
        _const_spec((4, D_MODEL)),
        _const_spec((D_MODEL, D_MODEL)),
    ]
    args = list(ins) + [x, mod, g, w_out]
    out_shape = [jax.ShapeDtypeStruct((rows, D_MODEL), F32), jax.ShapeDtypeStruct((rows, D_MODEL), BF16)]
    out_specs = [row_spec(D_MODEL), row_spec(D_MODEL)]
    if router:
        in_specs.append(_const_spec((D_MODEL, LANES)))
        args.append(router_w)
        out_shape.append(jax.ShapeDtypeStruct((rows, LANES), F32))
        out_specs.append(row_spec(LANES))
    return pl.pallas_call(
        functools.partial(_out_kernel, n_in=len(ins), router=router),
        grid=(rows // tm,),
        in_specs=in_specs,
        out_specs=out_specs,
        out_shape=out_shape,
        compiler_params=_cparams(("arbitrary",)),
        name="out_proj_norms_router" if router else "out_proj_norms",
    )(*args)


def _ffn_kernel(*refs, gated, n_e):
    if gated:
        h_ref, x1_ref, mod_ref, g_ref, gates_ref, wg_ref, wu_ref, wd_ref, o_ref, acc_ref = refs
    else:
        h_ref, x1_ref, mod_ref, g_ref, wg_ref, wu_ref, wd_ref, o_ref, acc_ref = refs
    e = pl.program_id(1)
    fc = pl.program_id(2)

    @pl.when((e == 0) & (fc == 0))
    def _():
        acc_ref[...] = jnp.zeros_like(acc_ref)

    h = h_ref[...]
    t = _silu(_dot(h, wg_ref[0])) * _dot(h, wu_ref[0])
    y = _dot(t.astype(BF16), wd_ref[0])
    if gated:
        gates = gates_ref[...]
        lane = lax.broadcasted_iota(jnp.int32, gates.shape, 1)
        y = y * jnp.sum(jnp.where(lane == e, gates, 0.0), axis=-1, keepdims=True)
    acc_ref[...] += y

    @pl.when((e == n_e - 1) & (fc == FF_CHUNKS - 1))
    def _():
        o_ref[...] = x1_ref[...] + mod_ref[0, 5:6, :] * _rms(acc_ref[...], g_ref[3:4, :])


def _ffn(h2, x1, mod, g, gates, wg, wu, wd, *, tm, rows_per_batch):
    rows = h2.shape[0]
    tpb = rows_per_batch // tm
    nb = mod.shape[0]
    n_e = wg.shape[0]
    gated = gates is not None
    row_spec = lambda w: pl.BlockSpec((tm, w), lambda i, e, f: (i, 0))
    in_specs = [row_spec(D_MODEL), row_spec(D_MODEL),
                pl.BlockSpec((1, 6, D_MODEL), lambda i, e, f: ((i // tpb) % nb, 0, 0)),
                _const_spec((4, D_MODEL))]
    args = [h2, x1, mod, g]
    if gated:
        in_specs.append(row_spec(LANES))
        args.append(gates)
    in_specs += [
        pl.BlockSpec((1, D_MODEL, FF_CHUNK), lambda i, e, f: (e, 0, f)),
        pl.BlockSpec((1, D_MODEL, FF_CHUNK), lambda i, e, f: (e, 0, f)),
        pl.BlockSpec((1, FF_CHUNK, D_MODEL), lambda i, e, f: (e, f, 0)),
    ]
    args += [wg, wu, wd]
    return pl.pallas_call(
        functools.partial(_ffn_kernel, gated=gated, n_e=n_e),
        grid=(rows // tm, n_e, FF_CHUNKS),
        in_specs=in_specs,
        out_specs=row_spec(D_MODEL),
        out_shape=jax.ShapeDtypeStruct((rows, D_MODEL), F32),
        scratch_shapes=[pltpu.VMEM((tm, D_MODEL), F32)],
        compiler_params=_cparams(("arbitrary", "arbitrary", "arbitrary")),
        name="moe_swiglu_post" if gated else "swiglu_post",
    )(*args)


def _l1_in_kernel(x_ref, mod_ref, g_ref, w_ref, gate_ref, xr_ref, xq_ref):
    h = _pre(x_ref[...], g_ref[0:1, :], mod_ref[0, 0:1, :], mod_ref[0, 1:2, :]).astype(BF16)
    z = _dot(h, w_ref[...])
    gate_ref[...] = z[:, 0:D_RNN].astype(BF16)
    xr_ref[...] = z[:, D_RNN:2 * D_RNN]
    xq_ref[...] = z[:, 2 * D_RNN:]


def _l1_in(x, mod, g, w_in, *, tm, rows_per_batch):
    rows = x.shape[0]
    tpb = rows_per_batch // tm
    nb = mod.shape[0]
    row_spec = lambda w: pl.BlockSpec((tm, w), lambda i: (i, 0))
    return pl.pallas_call(
        _l1_in_kernel,
        grid=(rows // tm,),
        in_specs=[row_spec(D_MODEL),
                  pl.BlockSpec((1, 6, D_MODEL), lambda i: ((i // tpb) % nb, 0, 0)),
                  _const_spec((4, D_MODEL)),
                  _const_spec((D_MODEL, w_in.shape[1]))],
        out_specs=[row_spec(D_RNN), row_spec(D_RNN), row_spec(D_POOL)],
        out_shape=[jax.ShapeDtypeStruct((rows, D_RNN), BF16),
                   jax.ShapeDtypeStruct((rows, D_RNN), F32),
                   jax.ShapeDtypeStruct((rows, D_POOL), F32)],
        compiler_params=_cparams(("arbitrary",)),
        name="l1_in_proj",
    )(x, mod, g, w_in)


def _with_halo(x_ref, prev_ref, next_ref, t_idx, n_tiles):
    x = x_ref[...]
    prev = jnp.where(t_idx > 0, prev_ref[:, 0], 0.0)
    nxt = jnp.where(t_idx < n_tiles - 1, next_ref[:, 0], 0.0)
    return jnp.concatenate([prev, x, nxt], axis=1)


def _scan_kernel(*refs, fwd, tt, n_tiles, n_seq):
    if fwd:
        (xr_ref, xrp_ref, xrn_ref, xq_ref, xqp_ref, xqn_ref, cw_ref, cb_ref, wgt_ref, ba_ref, bx_ref,
         lam_ref, h0_ref, pw_ref, ps_ref, hf_ref, yp_ref, hl_ref, a_scr, b_scr, h_scr, carry) = refs
    else:
        (xr_ref, xrp_ref, xrn_ref, cw_ref, cb_ref, wgt_ref, ba_ref, bx_ref, lam_ref, h0_ref,
         hf_ref, gate_ref, yr_ref, hl_ref, a_scr, b_scr, h_scr, carry) = refs
    i = pl.program_id(1)
    t_idx = i if fwd else n_tiles - 1 - i
    nb = SUBLANES

    ext = _with_halo(xr_ref, xrp_ref, xrn_ref, t_idx, n_tiles)
    xc = cb_ref[...][None]
    for j in range(CONV_W):
        lo = HALO + j - CONV_LEFT
        xc = xc + cw_ref[j:j + 1, :][None] * ext[:, lo:lo + tt, :]
    xc = xc.reshape(nb * tt, D_RNN)
    xb = xc.astype(BF16)
    r0 = _dot(xb[:, :RNN_HALF], wgt_ref[0])
    r1 = _dot(xb[:, RNN_HALF:], wgt_ref[1])
    ra = jnp.concatenate([r0[:, :RNN_HALF], r1[:, :RNN_HALF]], axis=1) + ba_ref[...]
    ri = jnp.concatenate([r0[:, RNN_HALF:], r1[:, RNN_HALF:]], axis=1) + bx_ref[...]
    nl = -lam_ref[...]
    softplus = jnp.maximum(nl, 0.0) + jnp.log1p(jnp.exp(-jnp.abs(nl)))
    log_a = -RG_C * _sigmoid(ra) * softplus
    a = jnp.exp(log_a)
    th = jnp.tanh(log_a)
    b = jnp.sqrt(-2.0 * th / (1.0 - th)) * (_sigmoid(ri) * xc)
    n_lc = D_RNN // LANES
    for bi in range(nb):
        for lc in range(n_lc):
            a_scr[lc, pl.ds(bi, tt, stride=nb), :] = a[bi * tt:(bi + 1) * tt, lc * LANES:(lc + 1) * LANES]
            b_scr[lc, pl.ds(bi, tt, stride=nb), :] = b[bi * tt:(bi + 1) * tt, lc * LANES:(lc + 1) * LANES]

    @pl.when(i == 0)
    def _():
        for lc in range(n_lc):
            carry[lc] = h0_ref[:, lc * LANES:(lc + 1) * LANES]

    def step(s, h):
        t = s if fwd else tt - 1 - s
        row = pl.multiple_of(t * nb, nb)
        h = a_scr[:, pl.ds(row, nb), :] * h + b_scr[:, pl.ds(row, nb), :]
        h_scr[:, pl.ds(row, nb), :] = h
        return h

    h_last = lax.fori_loop(0, tt, step, carry[...], unroll=8)
    carry[...] = h_last
    for lc in range(n_lc):
        hl_ref[:, lc * LANES:(lc + 1) * LANES] = h_last[lc]

    def unscan(bi):
        return jnp.concatenate([h_scr[lc, pl.ds(bi, tt, stride=nb), :] for lc in range(n_lc)], axis=1)

    if fwd:
        for bi in range(nb):
            hf_ref[bi] = unscan(bi)
        e = _with_halo(xq_ref, xqp_ref, xqn_ref, t_idx, n_tiles)
        length = tt + 2 * HALO
        a2 = e[:, 0:length - 1] + e[:, 1:length]
        a4 = a2[:, 0:length - 3] + a2[:, 2:length - 1]
        a8 = a4[:, 0:length - 7] + a4[:, 4:length - 3]
        a16 = a8[:, 0:length - 15] + a8[:, 8:length - 7]
        lane = lax.broadcasted_iota(jnp.int32, (1, 1, D_POOL), 2)
        wsum = jnp.where(lane < POOL_C, a2[:, HALO - 1:HALO - 1 + tt],
                         jnp.where(lane < 2 * POOL_C, a4[:, HALO - 2:HALO - 2 + tt],
                                   jnp.where(lane < 3 * POOL_C, a8[:, HALO - 4:HALO - 4 + tt],
                                             a16[:, 0:tt])))
        half = jnp.where(lane < POOL_C, POOL_WINDOWS[0] // 2,
                         jnp.where(lane < 2 * POOL_C, POOL_WINDOWS[1] // 2,
                                   jnp.where(lane < 3 * POOL_C, POOL_WINDOWS[2] // 2,
                                             POOL_WINDOWS[3] // 2)))
        tg = t_idx * tt + lax.broadcasted_iota(jnp.int32, (1, tt, D_POOL), 1)
        cnt = (jnp.minimum(tg + half, n_seq) - jnp.maximum(tg - half, 0)).astype(F32)
        dlt = (wsum / cnt - xq_ref[...]).astype(BF16).reshape(nb * tt, D_POOL)
        yp = _dot(dlt, pw_ref[...]) * ps_ref[...]
        yp_ref[...] = yp.reshape(nb, tt, D_POOL).astype(BF16)
    else:
        for bi in range(nb):
            hb = unscan(bi)
            yr_ref[bi] = ((hf_ref[bi] + hb) * _gelu(gate_ref[bi].astype(F32))).astype(BF16)


def _scan(fwd, xr, xq, conv_w, conv_b, w_gates, ba, bx, lam, h0, pool_w, pool_scale, h_f, gate, *, tt):
    n_batch, n_seq, _ = xr.shape
    n_bg = n_batch // SUBLANES
    n_tiles = n_seq // tt
    hb = tt // HALO
    n_hb = n_seq // HALO

    def t_of(i):
        return i if fwd else n_tiles - 1 - i

    def tile(c):
        return pl.BlockSpec((SUBLANES, tt, c), lambda b, i: (b, t_of(i), 0))

    def prev(c):
        return pl.BlockSpec((SUBLANES, 1, HALO, c),
                            lambda b, i: (b, jnp.maximum(t_of(i) * hb - 1, 0), 0, 0))

    def nxt(c):
        return pl.BlockSpec((SUBLANES, 1, HALO, c),
                            lambda b, i: (b, jnp.minimum((t_of(i) + 1) * hb, n_hb - 1), 0, 0))

    state = pl.BlockSpec((SUBLANES, D_RNN), lambda b, i: (b, 0))
    xr4 = xr.reshape(n_batch, n_hb, HALO, D_RNN)
    common = [_const_spec((CONV_W, D_RNN)), _const_spec((1, D_RNN)),
              _const_spec((2, RNN_HALF, D_RNN)), _const_spec((1, D_RNN)), _const_spec((1, D_RNN)),
              _const_spec((1, D_RNN)), state]
    common_args = [conv_w, conv_b, w_gates, ba, bx, lam, h0]
    n_lc = D_RNN // LANES
    scratch = ([pltpu.VMEM((n_lc, SUBLANES * tt, LANES), F32)] * 3
               + [pltpu.VMEM((n_lc, SUBLANES, LANES), F32)])
    if fwd:
        xq4 = xq.reshape(n_batch, n_hb, HALO, D_POOL)
        in_specs = ([tile(D_RNN), prev(D_RNN), nxt(D_RNN), tile(D_POOL), prev(D_POOL), nxt(D_POOL)]
                    + common + [_const_spec((D_POOL, D_POOL)), _const_spec((1, D_POOL))])
        args = [xr, xr4, xr4, xq, xq4, xq4] + common_args + [pool_w, pool_scale]
        out_specs = [tile(D_RNN), tile(D_POOL), state]
        out_shape = [jax.ShapeDtypeStruct((n_batch, n_seq, D_RNN), F32),
                     jax.ShapeDtypeStruct((n_batch, n_seq, D_POOL), BF16),
                     jax.ShapeDtypeStruct((n_batch, D_RNN), F32)]
    else:
        in_specs = [tile(D_RNN), prev(D_RNN), nxt(D_RNN)] + common + [tile(D_RNN), tile(D_RNN)]
        args = [xr, xr4, xr4] + common_args + [h_f, gate]
        out_specs = [tile(D_RNN), state]
        out_shape = [jax.ShapeDtypeStruct((n_batch, n_seq, D_RNN), BF16),
                     jax.ShapeDtypeStruct((n_batch, D_RNN), F32)]
    return pl.pallas_call(
        functools.partial(_scan_kernel, fwd=fwd, tt=tt, n_tiles=n_tiles, n_seq=n_seq),
        grid=(n_bg, n_tiles),
        in_specs=in_specs,
        out_specs=out_specs,
        out_shape=out_shape,
        scratch_shapes=scratch,
        compiler_params=_cparams(("arbitrary", "arbitrary")),
        name="rglru_forward_pool" if fwd else "rglru_backward_combine",
    )(*args)


def _gate_weights(wa, wx):
    per_half = RNN_BLOCKS // 2

    def half_dense(w, k):
        blocks = [w[k * per_half + b] for b in range(per_half)]
        rows = []
        for bi, blk in enumerate(blocks):
            row = [blk if bj == bi else jnp.zeros_like(blk) for bj in range(per_half)]
            rows.append(jnp.concatenate(row, axis=1))
        return jnp.concatenate(rows, axis=0)

    return jnp.stack([jnp.concatenate([half_dense(wa, k), half_dense(wx, k)], axis=1)
                      for k in range(2)]).astype(BF16)


def _pool_weights(pool_w):
    n = pool_w.shape[0]
    rows = []
    for i in range(n):
        rows.append(jnp.concatenate([pool_w[i] if j == i else jnp.zeros_like(pool_w[i])
                                     for j in range(n)], axis=1))
    return jnp.concatenate(rows, axis=0).astype(BF16)


def kernel(x_prompt, x_sample, cache_k, cache_v, state_h, c, c_ctx, w_ada, b_ada, norm_g, w_in_even,
           w_out_even, na_rpb, sgu_w, sgu_b, w_in_odd, w_out_odd, conv_w, conv_b, rg_wa, rg_ba, rg_wx,
           rg_bx, rg_lam, pool_w, pool_scale, ffn_wg, ffn_wu, ffn_wd, router_w, moe_wg, moe_wu, moe_wd):
    bp, n_p, _ = x_prompt.shape
    bs, n_s, _ = x_sample.shape
    n_ctx = cache_k.shape[3]

    cond = jnp.concatenate([c_ctx[None, :], c, jnp.zeros((2 * SUBLANES - 1 - bs, D_MODEL), F32)], axis=0)
    mods = _modulation(cond, w_ada, b_ada).reshape(w_ada.shape[0], cond.shape[0], 6, D_MODEL)

    streams = [
        dict(x=x_prompt.reshape(bp * n_p, D_MODEL), nb=bp, n=n_p, tm=256, latent=False),
        dict(x=x_sample.reshape(bs * n_s, D_MODEL), nb=bs, n=n_s, tm=512, latent=True),
    ]
    new_k = new_v = new_h = None

    g = norm_g[0]
    w_in = w_in_even[0].astype(BF16)
    w_out = w_out_even[0].astype(BF16)
    w_s = sgu_w[0].astype(BF16)
    b_s_t = sgu_b[0].T
    wg, wu, wd = (w[0:1].astype(BF16) for w in (ffn_wg, ffn_wu, ffn_wd))
    bias = _latent_bias_table(na_rpb[0], n_s // GRID_W)
    kc = cache_k[:, 0].transpose(0, 2, 1, 3).reshape(bs * n_ctx, W_A).astype(BF16)
    vc = cache_v[:, 0].transpose(0, 2, 1, 3).reshape(bs * n_ctx, W_A).astype(BF16)
    for st in streams:
        mod = mods[0, 1:1 + bs] if st["latent"] else mods[0, 0:1]
        outs = _l0_in(st["x"], mod, g, w_in, w_s, b_s_t, tm=st["tm"], rows_per_batch=st["n"],
                      emit_f32=not st["latent"])
        q, k, v, sg = outs[:4]
        if st["latent"]:
            oa = _attn_latent(q, k, v, kc, vc, bias, n_batch=st["nb"], n_seq=st["n"], n_ctx=n_ctx)
        else:
            oa = _attn_ctx(q, k, v, n_batch=st["nb"], n_seq=st["n"])
            heads = lambda a: a.reshape(bp, n_p, N_HEADS, HEAD_DIM).transpose(0, 2, 1, 3)[:, None]
            new_k, new_v = heads(outs[4]), heads(outs[5])
        x1, h2 = _out_proj([oa, sg], st["x"], mod, g, w_out, None, tm=st["tm"], rows_per_batch=st["n"])
        st["x"] = _ffn(h2, x1, mod, g, None, wg, wu, wd, tm=st["tm"], rows_per_batch=st["n"])

    g = norm_g[1]
    w_in = w_in_odd[0].astype(BF16)
    w_out = w_out_odd[0].astype(BF16)
    w_gates = [_gate_weights(rg_wa[0, d], rg_wx[0, d]) for d in range(2)]
    pw = _pool_weights(pool_w[0])
    ps = pool_scale[0][None, :]
    rw = jnp.pad(router_w[0], ((0, 0), (0, LANES - N_EXPERTS)))
    wg, wu, wd = (w[0].astype(BF16) for w in (moe_wg, moe_wu, moe_wd))
    for st in streams:
        mod = mods[1, 1:1 + bs] if st["latent"] else mods[1, 0:1]
        nb, n = st["nb"], st["n"]
        gate, xr, xq = _l1_in(st["x"], mod, g, w_in, tm=st["tm"], rows_per_batch=n)
        gate, xr, xq = (a.reshape(nb, n, a.shape[1]) for a in (gate, xr, xq))
        h0 = state_h[:, 0].astype(F32) if st["latent"] else jnp.zeros((nb, 2, D_RNN), F32)
        scan_args = lambda d: (conv_w[0], conv_b[0][None, :], w_gates[d], rg_ba[0, d][None, :],
                               rg_bx[0, d][None, :], rg_lam[0, d][None, :], h0[:, d])
        h_f, y_pool, h_f_last = _scan(True, xr, xq, *scan_args(0), pw, ps, None, None, tt=128)
        y_rec, h_b_last = _scan(False, xr, None, *scan_args(1), None, None, h_f, gate, tt=128)
        if not st["latent"]:
            new_h = jnp.stack([h_f_last, h_b_last], axis=1)[:, None]
        x1, h2, gates = _out_proj([y_rec.reshape(nb * n, D_RNN), y_pool.reshape(nb * n, D_POOL)],
                                  st["x"], mod, g, w_out, rw, tm=st["tm"], rows_per_batch=n)
        st["x"] = _ffn(h2, x1, mod, g, gates, wg, wu, wd, tm=st["tm"], rows_per_batch=n)

    y_prompt = streams[0]["x"].reshape(bp, n_p, D_MODEL)
    y_sample = streams[1]["x"].reshape(bs, n_s, D_MODEL)
    return (y_prompt, y_sample, new_k, new_v, new_h)
```

```python
import functools

import jax
import jax.numpy as jnp
import numpy as np
from jax import lax
from jax.experimental import pallas as pl
from jax.experimental.pallas import tpu as pltpu

F32 = jnp.float32
BF16 = jnp.bfloat16

D_MODEL = 1024
GRID_W = 64
HEAD_DIM = 64
N_HEADS = 8
W_A = N_HEADS * HEAD_DIM
WIN_R = 8
WIN_C = 16
N_GROUPS_B = 4
W_B = 512
CHUNK = 128
D_RNN = 768
RNN_BLOCKS = 8
RNN_BW = D_RNN // RNN_BLOCKS
RNN_HALF = D_RNN // 2
CONV_W = 4
CONV_LEFT = 2
RG_C = 8.0
D_POOL = 256
POOL_WINDOWS = (2, 4, 8, 16)
POOL_C = D_POOL // len(POOL_WINDOWS)
D_FF = 2816
N_EXPERTS = 8
EPS = 1e-6
NEG = -1e30

LANES = 128
SUBLANES = 8
VMEM_LIMIT = 56 * 1024 * 1024
Q_ROWS = 4
KEY_ROWS = Q_ROWS + WIN_R - 1
HALO = 8
FF_CHUNKS = 2
FF_CHUNK = D_FF // FF_CHUNKS


def _cparams(sem):
    return pltpu.CompilerParams(dimension_semantics=sem, vmem_limit_bytes=VMEM_LIMIT)


def _const_spec(shape):
    nd = len(shape)
    return pl.BlockSpec(shape, lambda *_: (0,) * nd, pipeline_mode=pl.Buffered(1))


def _dot(a, b):
    return jnp.dot(a, b, preferred_element_type=F32)


def _dot_nt(a, b):
    return lax.dot_general(a, b, (((1,), (1,)), ((), ())), preferred_element_type=F32)


def _rms(x, g):
    ms = jnp.mean(x * x, axis=-1, keepdims=True)
    return x * lax.rsqrt(ms + EPS) * g


def _pre(x, g, shift, scale):
    return _rms(x, g) * (1.0 + scale) + shift


def _gelu(x):
    return jax.nn.gelu(x, approximate=True)


def _sigmoid(x):
    return 1.0 / (1.0 + jnp.exp(-x))


def _silu(x):
    return x * _sigmoid(x)


def _mod_kernel(c_ref, w_ref, b_ref, o_ref):
    s = _silu(c_ref[...])
    o_ref[0] = jnp.dot(s, w_ref[0], preferred_element_type=F32,
                       precision=lax.Precision.HIGHEST) + b_ref[0]


def _modulation(cond, w_ada, b_ada):
    depth, d, n = w_ada.shape
    rows = cond.shape[0]
    bn = 768
    return pl.pallas_call(
        _mod_kernel,
        grid=(depth, n // bn),
        in_specs=[
            pl.BlockSpec((rows, d), lambda i, j: (0, 0)),
            pl.BlockSpec((1, d, bn), lambda i, j: (i, 0, j)),
            pl.BlockSpec((1, 1, bn), lambda i, j: (i, 0, j)),
        ],
        out_specs=pl.BlockSpec((1, rows, bn), lambda i, j: (i, 0, j)),
        out_shape=jax.ShapeDtypeStruct((depth, rows, n), F32),
        compiler_params=_cparams(("arbitrary", "arbitrary")),
        name="adaln_modulation",
    )(cond, w_ada, b_ada.reshape(depth, 1, n))


def _l0_in_kernel(x_ref, mod_ref, g_ref, w_ref, ws_ref, bs_ref, *out_refs, tm, emit_f32):
    q_ref, k_ref, v_ref, sg_ref = out_refs[:4]
    x = x_ref[...]
    h = _pre(x, g_ref[0:1, :], mod_ref[0, 0:1, :], mod_ref[0, 1:2, :]).astype(BF16)
    z = _dot(h, w_ref[...])
    q_ref[...] = z[:, 0:W_A].astype(BF16)
    k_ref[...] = z[:, W_A:2 * W_A].astype(BF16)
    v_ref[...] = z[:, 2 * W_A:3 * W_A].astype(BF16)
    if emit_f32:
        out_refs[4][...] = z[:, W_A:2 * W_A]
        out_refs[5][...] = z[:, 2 * W_A:3 * W_A]
    u = _gelu(z[:, 3 * W_A:3 * W_A + W_B])
    gf = _gelu(z[:, 3 * W_A + W_B:])
    n_chunks = tm // CHUNK
    for gi in range(N_GROUPS_B):
        gg = gf[:, gi * LANES:(gi + 1) * LANES]
        mu = jnp.mean(gg, axis=-1, keepdims=True)
        dd = gg - mu
        var = jnp.mean(dd * dd, axis=-1, keepdims=True)
        gn = (dd * lax.rsqrt(var + EPS)).astype(BF16)
        rhs = jnp.concatenate([gn[c * CHUNK:(c + 1) * CHUNK, :] for c in range(n_chunks)], axis=1)
        s = _dot(ws_ref[gi], rhs) + bs_ref[:, gi:gi + 1]
        for c in range(n_chunks):
            uu = u[c * CHUNK:(c + 1) * CHUNK, gi * LANES:(gi + 1) * LANES]
            sg_ref[c * CHUNK:(c + 1) * CHUNK, gi * LANES:(gi + 1) * LANES] = (
                uu * s[:, c * LANES:(c + 1) * LANES]).astype(BF16)


def _l0_in(x, mod, g, w_in, w_s, b_s_t, *, tm, rows_per_batch, emit_f32):
    rows = x.shape[0]
    tpb = rows_per_batch // tm
    nb = mod.shape[0]
    d_in = w_in.shape[1]
    row_spec = lambda w: pl.BlockSpec((tm, w), lambda i: (i, 0))
    out_shape = [jax.ShapeDtypeStruct((rows, W_A), BF16)] * 3 + [jax.ShapeDtypeStruct((rows, W_B), BF16)]
    out_specs = [row_spec(W_A)] * 3 + [row_spec(W_B)]
    if emit_f32:
        out_shape += [jax.ShapeDtypeStruct((rows, W_A), F32)] * 2
        out_specs += [row_spec(W_A)] * 2
    return pl.pallas_call(
        functools.partial(_l0_in_kernel, tm=tm, emit_f32=emit_f32),
        grid=(rows // tm,),
        in_specs=[
            row_spec(D_MODEL),
            pl.BlockSpec((1, 6, D_MODEL), lambda i: ((i // tpb) % nb, 0, 0)),
            _const_spec((4, D_MODEL)),
            _const_spec((D_MODEL, d_in)),
            _const_spec((N_GROUPS_B, CHUNK, CHUNK)),
            _const_spec((CHUNK, N_GROUPS_B)),
        ],
        out_specs=out_specs,
        out_shape=out_shape,
        compiler_params=_cparams(("arbitrary",)),
        name="l0_in_proj_sgu",
    )(x, mod, g, w_in, w_s, b_s_t)


def _attn_kernel(*refs, n_local, has_ctx, rows):
    if has_ctx:
        q_ref, k_ref, v_ref, kc_ref, vc_ref, bias_ref, o_ref = refs
        r0 = pl.program_id(2) * Q_ROWS
        ks = jnp.clip(r0 - WIN_R // 2, 0, rows - KEY_ROWS)
        start = pl.multiple_of(ks * GRID_W, GRID_W)
        kl = k_ref[pl.ds(start, n_local), :]
        vl = v_ref[pl.ds(start, n_local), :]
    else:
        q_ref, k_ref, v_ref, o_ref = refs
        kl = k_ref[...]
        vl = v_ref[...]
    q = q_ref[...] * jnp.asarray(HEAD_DIM ** -0.5, BF16)
    first = lax.broadcasted_iota(jnp.int32, (1, LANES), 1) < HEAD_DIM
    outs = []
    for half in range(2):
        qh = jnp.where(first if half == 0 else jnp.logical_not(first), q, jnp.zeros_like(q))
        s = _dot_nt(qh, kl)
        if has_ctx:
            s = s + bias_ref[0, half]
            sc = _dot_nt(qh, kc_ref[...])
            m = jnp.maximum(jnp.max(s, axis=-1, keepdims=True), jnp.max(sc, axis=-1, keepdims=True))
        else:
            m = jnp.max(s, axis=-1, keepdims=True)
        p = jnp.exp(s - m)
        l = jnp.sum(p, axis=-1, keepdims=True)
        acc = _dot(p.astype(BF16), vl)
        if has_ctx:
            pc = jnp.exp(sc - m)
            l = l + jnp.sum(pc, axis=-1, keepdims=True)
            acc = acc + _dot(pc.astype(BF16), vc_ref[...])
        outs.append(acc / l)
    o_ref[...] = jnp.where(first, outs[0], outs[1]).astype(BF16)


def _attn_ctx(q, k, v, *, n_batch, n_seq):
    rows = q.shape[0]
    spec = pl.BlockSpec((n_seq, LANES), lambda b, hp: (b, hp))
    return pl.pallas_call(
        functools.partial(_attn_kernel, n_local=n_seq, has_ctx=False, rows=0),
        grid=(n_batch, W_A // LANES),
        in_specs=[spec, spec, spec],
        out_specs=spec,
        out_shape=jax.ShapeDtypeStruct((rows, W_A), BF16),
        compiler_params=_cparams(("arbitrary", "arbitrary")),
        name="attn_context",
    )(q, k, v)


def _attn_latent(q, k, v, kc, vc, bias, *, n_batch, n_seq, n_ctx):
    rows = n_seq // GRID_W
    n_rg = rows // Q_ROWS
    tq = Q_ROWS * GRID_W
    n_local = KEY_ROWS * GRID_W
    img_spec = pl.BlockSpec((n_seq, LANES), lambda b, hp, rg: (b, hp))
    ctx_spec = pl.BlockSpec((n_ctx, LANES), lambda b, hp, rg: (b, hp))
    q_spec = pl.BlockSpec((tq, LANES), lambda b, hp, rg: (b * n_rg + rg, hp))

    def bias_map(b, hp, rg):
        cfg = jnp.where(rg == 0, 0, jnp.where(rg == n_rg - 1, 2, 1))
        return (cfg, hp, 0, 0)

    return pl.pallas_call(
        functools.partial(_attn_kernel, n_local=n_local, has_ctx=True, rows=rows),
        grid=(n_batch, W_A // LANES, n_rg),
        in_specs=[q_spec, img_spec, img_spec, ctx_spec, ctx_spec,
                  pl.BlockSpec((1, 2, tq, n_local), bias_map)],
        out_specs=q_spec,
        out_shape=jax.ShapeDtypeStruct((n_batch * n_seq, W_A), BF16),
        compiler_params=_cparams(("arbitrary", "arbitrary", "arbitrary")),
        name="attn_latent",
    )(q, k, v, kc, vc, bias)


def _latent_bias_table(rpb, rows):
    n_heads = rpb.shape[0]
    qcol = np.arange(GRID_W)[:, None]
    kcol = np.arange(GRID_W)[None, :]
    cs = np.clip(qcol - WIN_C // 2, 0, GRID_W - WIN_C)
    col_valid = (kcol >= cs) & (kcol < cs + WIN_C)
    dc = np.clip(kcol - qcol + (WIN_C - 1), 0, 2 * WIN_C - 2)
    pick = (dc[:, :, None] == np.arange(2 * WIN_C - 1)).astype(np.float32)
    col_tab = jnp.einsum("hrc,qkc->hrqk", rpb.astype(F32), pick, precision=lax.Precision.HIGHEST)
    col_tab = jnp.where(col_valid[None, None], col_tab, NEG)
    tables = []
    for r0 in (0, 2 * Q_ROWS, rows - Q_ROWS):
        ks = min(max(r0 - WIN_R // 2, 0), rows - KEY_ROWS)
        qrow = r0 + np.arange(Q_ROWS)[:, None]
        krow = ks + np.arange(KEY_ROWS)[None, :]
        rs = np.clip(qrow - WIN_R // 2, 0, rows - WIN_R)
        row_valid = (krow >= rs) & (krow < rs + WIN_R)
        dr = np.clip(krow - qrow + (WIN_R - 1), 0, 2 * WIN_R - 2)
        blocks = [jnp.stack([col_tab[:, dr[i, k]] if row_valid[i, k]
                             else jnp.full((n_heads, GRID_W, GRID_W), NEG, F32)
                             for k in range(KEY_ROWS)], axis=2)
                  for i in range(Q_ROWS)]
        b = jnp.stack(blocks, axis=1)
        tables.append(b.reshape(n_heads, Q_ROWS * GRID_W, KEY_ROWS * GRID_W))
    return jnp.stack(tables)


def _out_kernel(*refs, n_in, router):
    in_refs = refs[:n_in]
    x_ref, mod_ref, g_ref, w_ref = refs[n_in:n_in + 4]
    rest = refs[n_in + 4:]
    if router:
        rw_ref, x1_ref, h2_ref, gates_ref = rest
    else:
        x1_ref, h2_ref = rest
    y = None
    off = 0
    for r in in_refs:
        w = r.shape[1]
        part = _dot(r[...], w_ref[off:off + w, :])
        y = part if y is None else y + part
        off += w
    x1 = x_ref[...] + mod_ref[0, 2:3, :] * _rms(y, g_ref[1:2, :])
    x1_ref[...] = x1
    h2 = _pre(x1, g_ref[2:3, :], mod_ref[0, 3:4, :], mod_ref[0, 4:5, :])
    h2_ref[...] = h2.astype(BF16)
    if router:
        logits = jnp.dot(h2, rw_ref[...], preferred_element_type=F32,
                         precision=lax.Precision.HIGHEST)
        lane = lax.broadcasted_iota(jnp.int32, logits.shape, 1).astype(F32)
        logits = jnp.where(lane < N_EXPERTS, logits, -jnp.inf)
        m1 = jnp.max(logits, axis=-1, keepdims=True)
        i1 = jnp.min(jnp.where(logits == m1, lane, float(LANES)), axis=-1, keepdims=True)
        rest_l = jnp.where(lane == i1, -jnp.inf, logits)
        m2 = jnp.max(rest_l, axis=-1, keepdims=True)
        i2 = jnp.min(jnp.where(rest_l == m2, lane, float(LANES)), axis=-1, keepdims=True)
        e2 = jnp.exp(m2 - m1)
        w1 = 1.0 / (1.0 + e2)
        w2 = e2 / (1.0 + e2)
        gates_ref[...] = jnp.where(lane == i1, w1, 0.0) + jnp.where(lane == i2, w2, 0.0)


def _out_proj(ins, x, mod, g, w_out, router_w, *, tm, rows_per_batch):
    rows = x.shape[0]
    tpb = rows_per_batch // tm
    nb = mod.shape[0]
    router = router_w is not None
    row_spec = lambda w: pl.BlockSpec((tm, w), lambda i: (i, 0))
    in_specs = [row_spec(a.shape[1]) for a in ins] + [
        row_spec(D_MODEL),
        pl.BlockSpec((1, 6, D_MODEL), lambda i: ((i // tpb) % nb, 0, 0)),
        _const_spec((4, D_MODEL)),
        _const_spec((D_MODEL, D_MODEL)),
    ]
    args = list(ins) + [x, mod, g, w_out]
    out_shape = [jax.ShapeDtypeStruct((rows, D_MODEL), F32), jax.ShapeDtypeStruct((rows, D_MODEL), BF16)]
    out_specs = [row_spec(D_MODEL), row_spec(D_MODEL)]
    if router:
        in_specs.append(_const_spec((D_MODEL, LANES)))
        args.append(router_w)
        out_shape.append(jax.ShapeDtypeStruct((rows, LANES), F32))
        out_specs.append(row_spec(LANES))
    return pl.pallas_call(
        functools.partial(_out_kernel, n_in=len(ins), router=router),
        grid=(rows // tm,),
        in_specs=in_specs,
        out_specs=out_specs,
        out_shape=out_shape,
        compiler_params=_cparams(("arbitrary",)),
        name="out_proj_norms_router" if router else "out_proj_norms",
    )(*args)


def _ffn_kernel(*refs, gated, n_e):
    if gated:
        h_ref, x1_ref, mod_ref, g_ref, gates_ref, wg_ref, wu_ref, wd_ref, o_ref, acc_ref = refs
    else:
        h_ref, x1_ref, mod_ref, g_ref, wg_ref, wu_ref, wd_ref, o_ref, acc_ref = refs
    e = pl.program_id(1)
    fc = pl.program_id(2)

    @pl.when((e == 0) & (fc == 0))
    def _():
        acc_ref[...] = jnp.zeros_like(acc_ref)

    h = h_ref[...]
    t = _silu(_dot(h, wg_ref[0])) * _dot(h, wu_ref[0])
    y = _dot(t.astype(BF16), wd_ref[0])
    if gated:
        gates = gates_ref[...]
        lane = lax.broadcasted_iota(jnp.int32, gates.shape, 1)
        y = y * jnp.sum(jnp.where(lane == e, gates, 0.0), axis=-1, keepdims=True)
    acc_ref[...] += y

    @pl.when((e == n_e - 1) & (fc == FF_CHUNKS - 1))
    def _():
        o_ref[...] = x1_ref[...] + mod_ref[0, 5:6, :] * _rms(acc_ref[...], g_ref[3:4, :])


def _ffn(h2, x1, mod, g, gates, wg, wu, wd, *, tm, rows_per_batch):
    rows = h2.shape[0]
    tpb = rows_per_batch // tm
    nb = mod.shape[0]
    n_e = wg.shape[0]
    gated = gates is not None
    row_spec = lambda w: pl.BlockSpec((tm, w), lambda i, e, f: (i, 0))
    in_specs = [row_spec(D_MODEL), row_spec(D_MODEL),
                pl.BlockSpec((1, 6, D_MODEL), lambda i, e, f: ((i // tpb) % nb, 0, 0)),
                _const_spec((4, D_MODEL))]
    args = [h2, x1, mod, g]
    if gated:
        in_specs.append(row_spec(LANES))
        args.append(gates)
    in_specs += [
        pl.BlockSpec((1, D_MODEL, FF_CHUNK), lambda i, e, f: (e, 0, f)),
        pl.BlockSpec((1, D_MODEL, FF_CHUNK), lambda i, e, f: (e, 0, f)),
        pl.BlockSpec((1, FF_CHUNK, D_MODEL), lambda i, e, f: (e, f, 0)),
    ]
    args += [wg, wu, wd]
    return pl.pallas_call(
        functools.partial(_ffn_kernel, gated=gated, n_e=n_e),
        grid=(rows // tm, n_e, FF_CHUNKS),
        in_specs=in_specs,
        out_specs=row_spec(D_MODEL),
        out_shape=jax.ShapeDtypeStruct((rows, D_MODEL), F32),
        scratch_shapes=[pltpu.VMEM((tm, D_MODEL), F32)],
        compiler_params=_cparams(("arbitrary", "arbitrary", "arbitrary")),
        name="moe_swiglu_post" if gated else "swiglu_post",
    )(*args)


def _l1_in_kernel(x_ref, mod_ref, g_ref, w_ref, gate_ref, xr_ref, xq_ref):
    h = _pre(x_ref[...], g_ref[0:1, :], mod_ref[0, 0:1, :], mod_ref[0, 1:2, :]).astype(BF16)
    z = _dot(h, w_ref[...])
    gate_ref[...] = z[:, 0:D_RNN].astype(BF16)
    xr_ref[...] = z[:, D_RNN:2 * D_RNN]
    xq_ref[...] = z[:, 2 * D_RNN:]


def _l1_in(x, mod, g, w_in, *, tm, rows_per_batch):
    rows = x.shape[0]
    tpb = rows_per_batch // tm
    nb = mod.shape[0]
    row_spec = lambda w: pl.BlockSpec((tm, w), lambda i: (i, 0))
    return pl.pallas_call(
        _l1_in_kernel,
        grid=(rows // tm,),
        in_specs=[row_spec(D_MODEL),
                  pl.BlockSpec((1, 6, D_MODEL), lambda i: ((i // tpb) % nb, 0, 0)),
                  _const_spec((4, D_MODEL)),
                  _const_spec((D_MODEL, w_in.shape[1]))],
        out_specs=[row_spec(D_RNN), row_spec(D_RNN), row_spec(D_POOL)],
        out_shape=[jax.ShapeDtypeStruct((rows, D_RNN), BF16),
                   jax.ShapeDtypeStruct((rows, D_RNN), F32),
                   jax.ShapeDtypeStruct((rows, D_POOL), F32)],
        compiler_params=_cparams(("arbitrary",)),
        name="l1_in_proj",
    )(x, mod, g, w_in)


def _with_halo(x_ref, prev_ref, next_ref, t_idx, n_tiles):
    x = x_ref[...]
    prev = jnp.where(t_idx > 0, prev_ref[:, 0], 0.0)
    nxt = jnp.where(t_idx < n_tiles - 1, next_ref[:, 0], 0.0)
    return jnp.concatenate([prev, x, nxt], axis=1)


def _scan_kernel(*refs, fwd, tt, n_tiles, n_seq):
    if fwd:
        (xr_ref, xrp_ref, xrn_ref, xq_ref, xqp_ref, xqn_ref, cw_ref, cb_ref, wgt_ref, ba_ref, bx_ref,
         lam_ref, h0_ref, pw_ref, ps_ref, hf_ref, yp_ref, hl_ref, a_scr, b_scr, h_scr, carry) = refs
    else:
        (xr_ref, xrp_ref, xrn_ref, cw_ref, cb_ref, wgt_ref, ba_ref, bx_ref, lam_ref, h0_ref,
         hf_ref, gate_ref, yr_ref, hl_ref, a_scr, b_scr, h_scr, carry) = refs
    i = pl.program_id(1)
    t_idx = i if fwd else n_tiles - 1 - i
    nb = SUBLANES

    ext = _with_halo(xr_ref, xrp_ref, xrn_ref, t_idx, n_tiles)
    xc = cb_ref[...][None]
    for j in range(CONV_W):
        lo = HALO + j - CONV_LEFT
        xc = xc + cw_ref[j:j + 1, :][None] * ext[:, lo:lo + tt, :]
    xc = xc.reshape(nb * tt, D_RNN)
    xb = xc.astype(BF16)
    r0 = _dot(xb[:, :RNN_HALF], wgt_ref[0])
    r1 = _dot(xb[:, RNN_HALF:], wgt_ref[1])
    ra = jnp.concatenate([r0[:, :RNN_HALF], r1[:, :RNN_HALF]], axis=1) + ba_ref[...]
    ri = jnp.concatenate([r0[:, RNN_HALF:], r1[:, RNN_HALF:]], axis=1) + bx_ref[...]
    nl = -lam_ref[...]
    softplus = jnp.maximum(nl, 0.0) + jnp.log1p(jnp.exp(-jnp.abs(nl)))
    log_a = -RG_C * _sigmoid(ra) * softplus
    a = jnp.exp(log_a)
    th = jnp.tanh(log_a)
    b = jnp.sqrt(-2.0 * th / (1.0 - th)) * (_sigmoid(ri) * xc)
    n_lc = D_RNN // LANES
    for bi in range(nb):
        for lc in range(n_lc):
            a_scr[lc, pl.ds(bi, tt, stride=nb), :] = a[bi * tt:(bi + 1) * tt, lc * LANES:(lc + 1) * LANES]
            b_scr[lc, pl.ds(bi, tt, stride=nb), :] = b[bi * tt:(bi + 1) * tt, lc * LANES:(lc + 1) * LANES]

    @pl.when(i == 0)
    def _():
        for lc in range(n_lc):
            carry[lc] = h0_ref[:, lc * LANES:(lc + 1) * LANES]

    def step(s, h):
        t = s if fwd else tt - 1 - s
        row = pl.multiple_of(t * nb, nb)
        h = a_scr[:, pl.ds(row, nb), :] * h + b_scr[:, pl.ds(row, nb), :]
        h_scr[:, pl.ds(row, nb), :] = h
        return h

    h_last = lax.fori_loop(0, tt, step, carry[...], unroll=8)
    carry[...] = h_last
    for lc in range(n_lc):
        hl_ref[:, lc * LANES:(lc + 1) * LANES] = h_last[lc]

    def unscan(bi):
        return jnp.concatenate([h_scr[lc, pl.ds(bi, tt, stride=nb), :] for lc in range(n_lc)], axis=1)

    if fwd:
        for bi in range(nb):
            hf_ref[bi] = unscan(bi)
        e = _with_halo(xq_ref, xqp_ref, xqn_ref, t_idx, n_tiles)
        length = tt + 2 * HALO
        a2 = e[:, 0:length - 1] + e[:, 1:length]
        a4 = a2[:, 0:length - 3] + a2[:, 2:length - 1]
        a8 = a4[:, 0:length - 7] + a4[:, 4:length - 3]
        a16 = a8[:, 0:length - 15] + a8[:, 8:length - 7]
        lane = lax.broadcasted_iota(jnp.int32, (1, 1, D_POOL), 2)
        wsum = jnp.where(lane < POOL_C, a2[:, HALO - 1:HALO - 1 + tt],
                         jnp.where(lane < 2 * POOL_C, a4[:, HALO - 2:HALO - 2 + tt],
                                   jnp.where(lane < 3 * POOL_C, a8[:, HALO - 4:HALO - 4 + tt],
                                             a16[:, 0:tt])))
        half = jnp.where(lane < POOL_C, POOL_WINDOWS[0] // 2,
                         jnp.where(lane < 2 * POOL_C, POOL_WINDOWS[1] // 2,
                                   jnp.where(lane < 3 * POOL_C, POOL_WINDOWS[2] // 2,
                                             POOL_WINDOWS[3] // 2)))
        tg = t_idx * tt + lax.broadcasted_iota(jnp.int32, (1, tt, D_POOL), 1)
        cnt = (jnp.minimum(tg + half, n_seq) - jnp.maximum(tg - half, 0)).astype(F32)
        dlt = (wsum / cnt - xq_ref[...]).astype(BF16).reshape(nb * tt, D_POOL)
        yp = _dot(dlt, pw_ref[...]) * ps_ref[...]
        yp_ref[...] = yp.reshape(nb, tt, D_POOL).astype(BF16)
    else:
        for bi in range(nb):
            hb = unscan(bi)
            yr_ref[bi] = ((hf_ref[bi] + hb) * _gelu(gate_ref[bi].astype(F32))).astype(BF16)


def _scan(fwd, xr, xq, conv_w, conv_b, w_gates, ba, bx, lam, h0, pool_w, pool_scale, h_f, gate, *, tt):
    n_batch, n_seq, _ = xr.shape
    n_bg = n_batch // SUBLANES
    n_tiles = n_seq // tt
    hb = tt // HALO
    n_hb = n_seq // HALO

    def t_of(i):
        return i if fwd else n_tiles - 1 - i

    def tile(c):
        return pl.BlockSpec((SUBLANES, tt, c), lambda b, i: (b, t_of(i), 0))

    def prev(c):
        return pl.BlockSpec((SUBLANES, 1, HALO, c),
                            lambda b, i: (b, jnp.maximum(t_of(i) * hb - 1, 0), 0, 0))

    def nxt(c):
        return pl.BlockSpec((SUBLANES, 1, HALO, c),
                            lambda b, i: (b, jnp.minimum((t_of(i) + 1) * hb, n_hb - 1), 0, 0))

    state = pl.BlockSpec((SUBLANES, D_RNN), lambda b, i: (b, 0))
    xr4 = xr.reshape(n_batch, n_hb, HALO, D_RNN)
    common = [_const_spec((CONV_W, D_RNN)), _const_spec((1, D_RNN)),
              _const_spec((2, RNN_HALF, D_RNN)), _const_spec((1, D_RNN)), _const_spec((1, D_RNN)),
              _const_spec((1, D_RNN)), state]
    common_args = [conv_w, conv_b, w_gates, ba, bx, lam, h0]
    n_lc = D_RNN // LANES
    scratch = ([pltpu.VMEM((n_lc, SUBLANES * tt, LANES), F32)] * 3
               + [pltpu.VMEM((n_lc, SUBLANES, LANES), F32)])
    if fwd:
        xq4 = xq.reshape(n_batch, n_hb, HALO, D_POOL)
        in_specs = ([tile(D_RNN), prev(D_RNN), nxt(D_RNN), tile(D_POOL), prev(D_POOL), nxt(D_POOL)]
                    + common + [_const_spec((D_POOL, D_POOL)), _const_spec((1, D_POOL))])
        args = [xr, xr4, xr4, xq, xq4, xq4] + common_args + [pool_w, pool_scale]
        out_specs = [tile(D_RNN), tile(D_POOL), state]
        out_shape = [jax.ShapeDtypeStruct((n_batch, n_seq, D_RNN), F32),
                     jax.ShapeDtypeStruct((n_batch, n_seq, D_POOL), BF16),
                     jax.ShapeDtypeStruct((n_batch, D_RNN), F32)]
    else:
        in_specs = [tile(D_RNN), prev(D_RNN), nxt(D_RNN)] + common + [tile(D_RNN), tile(D_RNN)]
        args = [xr, xr4, xr4] + common_args + [h_f, gate]
        out_specs = [tile(D_RNN), state]
        out_shape = [jax.ShapeDtypeStruct((n_batch, n_seq, D_RNN), BF16),
                     jax.ShapeDtypeStruct((n_batch, D_RNN), F32)]
    return pl.pallas_call(
        functools.partial(_scan_kernel, fwd=fwd, tt=tt, n_tiles=n_tiles, n_seq=n_seq),
        grid=(n_bg, n_tiles),
        in_specs=in_specs,
        out_specs=out_specs,
        out_shape=out_shape,
        scratch_shapes=scratch,
        compiler_params=_cparams(("arbitrary", "arbitrary")),
        name="rglru_forward_pool" if fwd else "rglru_backward_combine",
    )(*args)


def _gate_weights(wa, wx):
    per_half = RNN_BLOCKS // 2

    def half_dense(w, k):
        blocks = [w[k * per_half + b] for b in range(per_half)]
        rows = []
        for bi, blk in enumerate(blocks):
            row = [blk if bj == bi else jnp.zeros_like(blk) for bj in range(per_half)]
            rows.append(jnp.concatenate(row, axis=1))
        return jnp.concatenate(rows, axis=0)

    return jnp.stack([jnp.concatenate([half_dense(wa, k), half_dense(wx, k)], axis=1)
                      for k in range(2)]).astype(BF16)


def _pool_weights(pool_w):
    n = pool_w.shape[0]
    rows = []
    for i in range(n):
        rows.append(jnp.concatenate([pool_w[i] if j == i else jnp.zeros_like(pool_w[i])
                                     for j in range(n)], axis=1))
    return jnp.concatenate(rows, axis=0).astype(BF16)


def kernel(x_prompt, x_sample, cache_k, cache_v, state_h, c, c_ctx, w_ada, b_ada, norm_g, w_in_even,
           w_out_even, na_rpb, sgu_w, sgu_b, w_in_odd, w_out_odd, conv_w, conv_b, rg_wa, rg_ba, rg_wx,
           rg_bx, rg_lam, pool_w, pool_scale, ffn_wg, ffn_wu, ffn_wd, router_w, moe_wg, moe_wu, moe_wd):
    bp, n_p, _ = x_prompt.shape
    bs, n_s, _ = x_sample.shape
    n_ctx = cache_k.shape[3]

    cond = jnp.concatenate([c_ctx[None, :], c, jnp.zeros((2 * SUBLANES - 1 - bs, D_MODEL), F32)], axis=0)
    mods = _modulation(cond, w_ada, b_ada).reshape(w_ada.shape[0], cond.shape[0], 6, D_MODEL)

    streams = [
        dict(x=x_prompt.reshape(bp * n_p, D_MODEL), nb=bp, n=n_p, tm=256, latent=False),
        dict(x=x_sample.reshape(bs * n_s, D_MODEL), nb=bs, n=n_s, tm=512, latent=True),
    ]
    new_k = new_v = new_h = None

    g = norm_g[0]
    w_in = w_in_even[0].astype(BF16)
    w_out = w_out_even[0].astype(BF16)
    w_s = sgu_w[0].astype(BF16)
    b_s_t = sgu_b[0].T
    wg, wu, wd = (w[0:1].astype(BF16) for w in (ffn_wg, ffn_wu, ffn_wd))
    bias = _latent_bias_table(na_rpb[0], n_s // GRID_W)
    kc = cache_k[:, 0].transpose(0, 2, 1, 3).reshape(bs * n_ctx, W_A).astype(BF16)
    vc = cache_v[:, 0].transpose(0, 2, 1, 3).reshape(bs * n_ctx, W_A).astype(BF16)
    for st in streams:
        mod = mods[0, 1:1 + bs] if st["latent"] else mods[0, 0:1]
        outs = _l0_in(st["x"], mod, g, w_in, w_s, b_s_t, tm=st["tm"], rows_per_batch=st["n"],
                      emit_f32=not st["latent"])
        q, k, v, sg = outs[:4]
        if st["latent"]:
            oa = _attn_latent(q, k, v, kc, vc, bias, n_batch=st["nb"], n_seq=st["n"], n_ctx=n_ctx)
        else:
            oa = _attn_ctx(q, k, v, n_batch=st["nb"], n_seq=st["n"])
            heads = lambda a: a.reshape(bp, n_p, N_HEADS, HEAD_DIM).transpose(0, 2, 1, 3)[:, None]
            new_k, new_v = heads(outs[4]), heads(outs[5])
        x1, h2 = _out_proj([oa, sg], st["x"], mod, g, w_out, None, tm=st["tm"], rows_per_batch=st["n"])
        st["x"] = _ffn(h2, x1, mod, g, None, wg, wu, wd, tm=st["tm"], rows_per_batch=st["n"])

    g = norm_g[1]
    w_in = w_in_odd[0].astype(BF16)
    w_out = w_out_odd[0].astype(BF16)
    w_gates = [_gate_weights(rg_wa[0, d], rg_wx[0, d]) for d in range(2)]
    pw = _pool_weights(pool_w[0])
    ps = pool_scale[0][None, :]
    rw = jnp.pad(router_w[0], ((0, 0), (0, LANES - N_EXPERTS)))
    wg, wu, wd = (w[0].astype(BF16) for w in (moe_wg, moe_wu, moe_wd))
    for st in streams:
        mod = mods[1, 1:1 + bs] if st["latent"] else mods[1, 0:1]
        nb, n = st["nb"], st["n"]
        gate, xr, xq = _l1_in(st["x"], mod, g, w_in, tm=st["tm"], rows_per_batch=n)
        gate, xr, xq = (a.reshape(nb, n, a.shape[1]) for a in (gate, xr, xq))
        h0 = state_h[:, 0].astype(F32) if st["latent"] else jnp.zeros((nb, 2, D_RNN), F32)
        scan_args = lambda d: (conv_w[0], conv_b[0][None, :], w_gates[d], rg_ba[0, d][None, :],
                               rg_bx[0, d][None, :], rg_lam[0, d][None, :], h0[:, d])
        h_f, y_pool, h_f_last = _scan(True, xr, xq, *scan_args(0), pw, ps, None, None, tt=128)
        y_rec, h_b_last = _scan(False, xr, None, *scan_args(1), None, None, h_f, gate, tt=128)
        if not st["latent"]:
            new_h = jnp.stack([h_f_last, h_b_last], axis=1)[:, None]
        x1, h2, gates = _out_proj([y_rec.reshape(nb * n, D_RNN), y_pool.reshape(nb * n, D_POOL)],
                                  st["x"], mod, g, w_out, rw, tm=st["tm"], rows_per_batch=n)
        st["x"] = _ffn(h2, x1, mod, g, gates, wg, wu, wd, tm=st["tm"], rows_per_batch=n)

    y_prompt = streams[0]["x"].reshape(bp, n_p, D_MODEL)
    y_sample = streams[1]["x"].reshape(bs, n_s, D_MODEL)
    return (y_prompt, y_sample, new_k, new_v, new_h)
```

```python
import functools

import jax
import jax.numpy as jnp
import numpy as np
from jax import lax
from jax.experimental import pallas as pl
from jax.experimental.pallas import tpu as pltpu

F32 = jnp.float32
BF16 = jnp.bfloat16

D_MODEL = 1024
GRID_W = 64
HEAD_DIM = 64
N_HEADS = 8
W_A = N_HEADS * HEAD_DIM
WIN_R = 8
WIN_C = 16
N_GROUPS_B = 4
W_B = 512
CHUNK = 128
D_RNN = 768
RNN_BLOCKS = 8
RNN_BW = D_RNN // RNN_BLOCKS
RNN_HALF = D_RNN // 2
CONV_W = 4
CONV_LEFT = 2
RG_C = 8.0
D_POOL = 256
POOL_WINDOWS = (2, 4, 8, 16)
POOL_C = D_POOL // len(POOL_WINDOWS)
D_FF = 2816
N_EXPERTS = 8
EPS = 1e-6
NEG = -1e30

LANES = 128
SUBLANES = 8
VMEM_LIMIT = 56 * 1024 * 1024
Q_ROWS = 4
KEY_ROWS = Q_ROWS + WIN_R - 1
HALO = 8
FF_CHUNKS = 2
FF_CHUNK = D_FF // FF_CHUNKS
MOE_SRC = 256
MOE_TILE = 256
MOE_CTILE = 128
MOE_CMAX = (MOE_SRC - 1 + MOE_CTILE - 1) // MOE_CTILE + 1


def _cparams(sem):
    return pltpu.CompilerParams(dimension_semantics=sem, vmem_limit_bytes=VMEM_LIMIT)


def _const_spec(shape):
    nd = len(shape)
    return pl.BlockSpec(shape, lambda *_: (0,) * nd, pipeline_mode=pl.Buffered(1))


def _dot(a, b):
    return jnp.dot(a, b, preferred_element_type=F32)


def _dot_nt(a, b):
    return lax.dot_general(a, b, (((1,), (1,)), ((), ())), preferred_element_type=F32)


def _rms(x, g):
    ms = jnp.mean(x * x, axis=-1, keepdims=True)
    return x * lax.rsqrt(ms + EPS) * g


def _pre(x, g, shift, scale):
    return _rms(x, g) * (1.0 + scale) + shift


def _gelu(x):
    return jax.nn.gelu(x, approximate=True)


def _sigmoid(x):
    return 1.0 / (1.0 + jnp.exp(-x))


def _silu(x):
    return x * _sigmoid(x)


def _mod_kernel(c_ref, w_ref, b_ref, o_ref):
    s = _silu(c_ref[...])
    o_ref[0] = jnp.dot(s, w_ref[0], preferred_element_type=F32,
                       precision=lax.Precision.HIGHEST) + b_ref[0]


def _modulation(cond, w_ada, b_ada):
    depth, d, n = w_ada.shape
    rows = cond.shape[0]
    bn = 768
    return pl.pallas_call(
        _mod_kernel,
        grid=(depth, n // bn),
        in_specs=[
            pl.BlockSpec((rows, d), lambda i, j: (0, 0)),
            pl.BlockSpec((1, d, bn), lambda i, j: (i, 0, j)),
            pl.BlockSpec((1, 1, bn), lambda i, j: (i, 0, j)),
        ],
        out_specs=pl.BlockSpec((1, rows, bn), lambda i, j: (i, 0, j)),
        out_shape=jax.ShapeDtypeStruct((depth, rows, n), F32),
        compiler_params=_cparams(("arbitrary", "arbitrary")),
        name="adaln_modulation",
    )(cond, w_ada, b_ada.reshape(depth, 1, n))


def _l0_in_kernel(x_ref, mod_ref, g_ref, w_ref, ws_ref, bs_ref, *out_refs, tm, emit_f32):
    q_ref, k_ref, v_ref, sg_ref = out_refs[:4]
    x = x_ref[...]
    h = _pre(x, g_ref[0:1, :], mod_ref[0, 0:1, :], mod_ref[0, 1:2, :]).astype(BF16)
    z = _dot(h, w_ref[...])
    q_ref[...] = z[:, 0:W_A].astype(BF16)
    k_ref[...] = z[:, W_A:2 * W_A].astype(BF16)
    v_ref[...] = z[:, 2 * W_A:3 * W_A].astype(BF16)
    if emit_f32:
        out_refs[4][...] = z[:, W_A:2 * W_A]
        out_refs[5][...] = z[:, 2 * W_A:3 * W_A]
    u = _gelu(z[:, 3 * W_A:3 * W_A + W_B])
    gf = _gelu(z[:, 3 * W_A + W_B:])
    n_chunks = tm // CHUNK
    for gi in range(N_GROUPS_B):
        gg = gf[:, gi * LANES:(gi + 1) * LANES]
        mu = jnp.mean(gg, axis=-1, keepdims=True)
        dd = gg - mu
        var = jnp.mean(dd * dd, axis=-1, keepdims=True)
        gn = (dd * lax.rsqrt(var + EPS)).astype(BF16)
        rhs = jnp.concatenate([gn[c * CHUNK:(c + 1) * CHUNK, :] for c in range(n_chunks)], axis=1)
        s = _dot(ws_ref[gi], rhs) + bs_ref[:, gi:gi + 1]
        for c in range(n_chunks):
            uu = u[c * CHUNK:(c + 1) * CHUNK, gi * LANES:(gi + 1) * LANES]
            sg_ref[c * CHUNK:(c + 1) * CHUNK, gi * LANES:(gi + 1) * LANES] = (
                uu * s[:, c * LANES:(c + 1) * LANES]).astype(BF16)


def _l0_in(x, mod, g, w_in, w_s, b_s_t, *, tm, rows_per_batch, emit_f32):
    rows = x.shape[0]
    tpb = rows_per_batch // tm
    nb = mod.shape[0]
    d_in = w_in.shape[1]
    row_spec = lambda w: pl.BlockSpec((tm, w), lambda i: (i, 0))
    out_shape = [jax.ShapeDtypeStruct((rows, W_A), BF16)] * 3 + [jax.ShapeDtypeStruct((rows, W_B), BF16)]
    out_specs = [row_spec(W_A)] * 3 + [row_spec(W_B)]
    if emit_f32:
        out_shape += [jax.ShapeDtypeStruct((rows, W_A), F32)] * 2
        out_specs += [row_spec(W_A)] * 2
    return pl.pallas_call(
        functools.partial(_l0_in_kernel, tm=tm, emit_f32=emit_f32),
        grid=(rows // tm,),
        in_specs=[
            row_spec(D_MODEL),
            pl.BlockSpec((1, 6, D_MODEL), lambda i: ((i // tpb) % nb, 0, 0)),
            _const_spec((4, D_MODEL)),
            _const_spec((D_MODEL, d_in)),
            _const_spec((N_GROUPS_B, CHUNK, CHUNK)),
            _const_spec((CHUNK, N_GROUPS_B)),
        ],
        out_specs=out_specs,
        out_shape=out_shape,
        compiler_params=_cparams(("arbitrary",)),
        name="l0_in_proj_sgu",
    )(x, mod, g, w_in, w_s, b_s_t)


def _attn_kernel(*refs, n_local, has_ctx, rows):
    if has_ctx:
        q_ref, k_ref, v_ref, kc_ref, vc_ref, bias_ref, o_ref = refs
        r0 = pl.program_id(2) * Q_ROWS
        ks = jnp.clip(r0 - WIN_R // 2, 0, rows - KEY_ROWS)
        start = pl.multiple_of(ks * GRID_W, GRID_W)
        kl = k_ref[pl.ds(start, n_local), :]
        vl = v_ref[pl.ds(start, n_local), :]
    else:
        q_ref, k_ref, v_ref, o_ref = refs
        kl = k_ref[...]
        vl = v_ref[...]
    q = q_ref[...] * jnp.asarray(HEAD_DIM ** -0.5, BF16)
    first = lax.broadcasted_iota(jnp.int32, (1, LANES), 1) < HEAD_DIM
    outs = []
    for half in range(2):
        qh = jnp.where(first if half == 0 else jnp.logical_not(first), q, jnp.zeros_like(q))
        s = _dot_nt(qh, kl)
        if has_ctx:
            s = s + bias_ref[0, half]
            sc = _dot_nt(qh, kc_ref[...])
            m = jnp.maximum(jnp.max(s, axis=-1, keepdims=True), jnp.max(sc, axis=-1, keepdims=True))
        else:
            m = jnp.max(s, axis=-1, keepdims=True)
        p = jnp.exp(s - m)
        l = jnp.sum(p, axis=-1, keepdims=True)
        acc = _dot(p.astype(BF16), vl)
        if has_ctx:
            pc = jnp.exp(sc - m)
            l = l + jnp.sum(pc, axis=-1, keepdims=True)
            acc = acc + _dot(pc.astype(BF16), vc_ref[...])
        outs.append(acc / l)
    o_ref[...] = jnp.where(first, outs[0], outs[1]).astype(BF16)


def _attn_ctx(q, k, v, *, n_batch, n_seq):
    rows = q.shape[0]
    spec = pl.BlockSpec((n_seq, LANES), lambda b, hp: (b, hp))
    return pl.pallas_call(
        functools.partial(_attn_kernel, n_local=n_seq, has_ctx=False, rows=0),
        grid=(n_batch, W_A // LANES),
        in_specs=[spec, spec, spec],
        out_specs=spec,
        out_shape=jax.ShapeDtypeStruct((rows, W_A), BF16),
        compiler_params=_cparams(("arbitrary", "arbitrary")),
        name="attn_context",
    )(q, k, v)


def _attn_latent(q, k, v, kc, vc, bias, *, n_batch, n_seq, n_ctx):
    rows = n_seq // GRID_W
    n_rg = rows // Q_ROWS
    tq = Q_ROWS * GRID_W
    n_local = KEY_ROWS * GRID_W
    img_spec = pl.BlockSpec((n_seq, LANES), lambda b, hp, rg: (b, hp))
    ctx_spec = pl.BlockSpec((n_ctx, LANES), lambda b, hp, rg: (b, hp))
    q_spec = pl.BlockSpec((tq, LANES), lambda b, hp, rg: (b * n_rg + rg, hp))

    def bias_map(b, hp, rg):
        cfg = jnp.where(rg == 0, 0, jnp.where(rg == n_rg - 1, 2, 1))
        return (cfg, hp, 0, 0)

    return pl.pallas_call(
        functools.partial(_attn_kernel, n_local=n_local, has_ctx=True, rows=rows),
        grid=(n_batch, W_A // LANES, n_rg),
        in_specs=[q_spec, img_spec, img_spec, ctx_spec, ctx_spec,
                  pl.BlockSpec((1, 2, tq, n_local), bias_map)],
        out_specs=q_spec,
        out_shape=jax.ShapeDtypeStruct((n_batch * n_seq, W_A), BF16),
        compiler_params=_cparams(("arbitrary", "arbitrary", "arbitrary")),
        name="attn_latent",
    )(q, k, v, kc, vc, bias)


def _latent_bias_table(rpb, rows):
    n_heads = rpb.shape[0]
    qcol = np.arange(GRID_W)[:, None]
    kcol = np.arange(GRID_W)[None, :]
    cs = np.clip(qcol - WIN_C // 2, 0, GRID_W - WIN_C)
    col_valid = (kcol >= cs) & (kcol < cs + WIN_C)
    dc = np.clip(kcol - qcol + (WIN_C - 1), 0, 2 * WIN_C - 2)
    pick = (dc[:, :, None] == np.arange(2 * WIN_C - 1)).astype(np.float32)
    col_tab = jnp.einsum("hrc,qkc->hrqk", rpb.astype(F32), pick, precision=lax.Precision.HIGHEST)
    col_tab = jnp.where(col_valid[None, None], col_tab, NEG)
    tables = []
    for r0 in (0, 2 * Q_ROWS, rows - Q_ROWS):
        ks = min(max(r0 - WIN_R // 2, 0), rows - KEY_ROWS)
        qrow = r0 + np.arange(Q_ROWS)[:, None]
        krow = ks + np.arange(KEY_ROWS)[None, :]
        rs = np.clip(qrow - WIN_R // 2, 0, rows - WIN_R)
        row_valid = (krow >= rs) & (krow < rs + WIN_R)
        dr = np.clip(krow - qrow + (WIN_R - 1), 0, 2 * WIN_R - 2)
        blocks = [jnp.stack([col_tab[:, dr[i, k]] if row_valid[i, k]
                             else jnp.full((n_heads, GRID_W, GRID_W), NEG, F32)
                             for k in range(KEY_ROWS)], axis=2)
                  for i in range(Q_ROWS)]
        b = jnp.stack(blocks, axis=1)
        tables.append(b.reshape(n_heads, Q_ROWS * GRID_W, KEY_ROWS * GRID_W))
    return jnp.stack(tables)


def _out_kernel(*refs, n_in, router):
    in_refs = refs[:n_in]
    x_ref, mod_ref, g_ref, w_ref = refs[n_in:n_in + 4]
    rest = refs[n_in + 4:]
    if router:
        rw_ref, x1_ref, h2_ref, gates_ref, rankc_ref, rankr_ref = rest
    else:
        x1_ref, h2_ref = rest
    y = None
    off = 0
    for r in in_refs:
        w = r.shape[1]
        part = _dot(r[...], w_ref[off:off + w, :])
        y = part if y is None else y + part
        off += w
    x1 = x_ref[...] + mod_ref[0, 2:3, :] * _rms(y, g_ref[1:2, :])
    x1_ref[...] = x1
    h2 = _pre(x1, g_ref[2:3, :], mod_ref[0, 3:4, :], mod_ref[0, 4:5, :])
    h2_ref[...] = h2.astype(BF16)
    if router:
        h_hi = h2.astype(BF16)
        h_lo = (h2 - h_hi.astype(F32)).astype(BF16)
        rw = rw_ref[...]
        r_hi = rw.astype(BF16)
        r_lo = (rw - r_hi.astype(F32)).astype(BF16)
        logits = _dot_nt(r_hi, h_hi) + (_dot_nt(r_hi, h_lo) + _dot_nt(r_lo, h_hi))
        tm = logits.shape[1]
        row = lax.broadcasted_iota(jnp.int32, logits.shape, 0).astype(F32)
        logits = jnp.where(row < N_EXPERTS, logits, -jnp.inf)
        m1 = jnp.max(logits, axis=0, keepdims=True)
        i1 = jnp.min(jnp.where(logits == m1, row, float(LANES)), axis=0, keepdims=True)
        rest_l = jnp.where(row == i1, -jnp.inf, logits)
        m2 = jnp.max(rest_l, axis=0, keepdims=True)
        i2 = jnp.min(jnp.where(rest_l == m2, row, float(LANES)), axis=0, keepdims=True)
        e2 = jnp.exp(m2 - m1)
        w1 = 1.0 / (1.0 + e2)
        w2 = e2 / (1.0 + e2)
        sel1 = row == i1
        sel2 = row == i2
        member = jnp.where(sel1, 1.0, 0.0) + jnp.where(sel2, 1.0, 0.0)
        gates = jnp.where(sel1, w1, 0.0) + jnp.where(sel2, w2, 0.0)
        before = (lax.broadcasted_iota(jnp.int32, (tm, tm), 0)
                  < lax.broadcasted_iota(jnp.int32, (tm, tm), 1))
        rank = _dot(member.astype(BF16), jnp.where(before, 1.0, 0.0).astype(BF16))
        rank = jnp.where(member > 0.0, rank, -1.0)
        gates_ref[...] = gates.T
        rankc_ref[...] = rank.T.astype(jnp.int32)
        rankr_ref[...] = rank[0:N_EXPERTS, :].astype(jnp.int32)


def _out_proj(ins, x, mod, g, w_out, router_w, *, tm, rows_per_batch):
    rows = x.shape[0]
    tpb = rows_per_batch // tm
    nb = mod.shape[0]
    router = router_w is not None
    row_spec = lambda w: pl.BlockSpec((tm, w), lambda i: (i, 0))
    in_specs = [row_spec(a.shape[1]) for a in ins] + [
        row_spec(D_MODEL),
        pl.BlockSpec((1, 6, D_MODEL), lambda i: ((i // tpb) % nb, 0, 0)),
        _const_spec((4, D_MODEL)),
        _const_spec((D_MODEL, D_MODEL)),
    ]
    args = list(ins) + [x, mod, g, w_out]
    out_shape = [jax.ShapeDtypeStruct((rows, D_MODEL), F32), jax.ShapeDtypeStruct((rows, D_MODEL), BF16)]
    out_specs = [row_spec(D_MODEL), row_spec(D_MODEL)]
    if router:
        in_specs.append(_const_spec((LANES, D_MODEL)))
        args.append(router_w)
        out_shape += [jax.ShapeDtypeStruct((rows, LANES), F32),
                      jax.ShapeDtypeStruct((rows, LANES), jnp.int32),
                      jax.ShapeDtypeStruct((N_EXPERTS, rows), jnp.int32)]
        out_specs += [row_spec(LANES), row_spec(LANES),
                      pl.BlockSpec((N_EXPERTS, tm), lambda i: (0, i))]
    return pl.pallas_call(
        functools.partial(_out_kernel, n_in=len(ins), router=router),
        grid=(rows // tm,),
        in_specs=in_specs,
        out_specs=out_specs,
        out_shape=out_shape,
        compiler_params=_cparams(("arbitrary",)),
        name="out_proj_norms_router" if router else "out_proj_norms",
    )(*args)


def _ffn_kernel(*refs, gated, n_e):
    if gated:
        h_ref, x1_ref, mod_ref, g_ref, gates_ref, wg_ref, wu_ref, wd_ref, o_ref, acc_ref = refs
    else:
        h_ref, x1_ref, mod_ref, g_ref, wg_ref, wu_ref, wd_ref, o_ref, acc_ref = refs
    e = pl.program_id(1)
    fc = pl.program_id(2)

    @pl.when((e == 0) & (fc == 0))
    def _():
        acc_ref[...] = jnp.zeros_like(acc_ref)

    h = h_ref[...]
    t = _silu(_dot(h, wg_ref[0])) * _dot(h, wu_ref[0])
    y = _dot(t.astype(BF16), wd_ref[0])
    if gated:
        gates = gates_ref[...]
        lane = lax.broadcasted_iota(jnp.int32, gates.shape, 1)
        y = y * jnp.sum(jnp.where(lane == e, gates, 0.0), axis=-1, keepdims=True)
    acc_ref[...] += y

    @pl.when((e == n_e - 1) & (fc == FF_CHUNKS - 1))
    def _():
        o_ref[...] = x1_ref[...] + mod_ref[0, 5:6, :] * _rms(acc_ref[...], g_ref[3:4, :])


def _ffn(h2, x1, mod, g, gates, wg, wu, wd, *, tm, rows_per_batch):
    rows = h2.shape[0]
    tpb = rows_per_batch // tm
    nb = mod.shape[0]
    n_e = wg.shape[0]
    gated = gates is not None
    row_spec = lambda w: pl.BlockSpec((tm, w), lambda i, e, f: (i, 0))
    in_specs = [row_spec(D_MODEL), row_spec(D_MODEL),
                pl.BlockSpec((1, 6, D_MODEL), lambda i, e, f: ((i // tpb) % nb, 0, 0)),
                _const_spec((4, D_MODEL))]
    args = [h2, x1, mod, g]
    if gated:
        in_specs.append(row_spec(LANES))
        args.append(gates)
    in_specs += [
        pl.BlockSpec((1, D_MODEL, FF_CHUNK), lambda i, e, f: (e, 0, f)),
        pl.BlockSpec((1, D_MODEL, FF_CHUNK), lambda i, e, f: (e, 0, f)),
        pl.BlockSpec((1, FF_CHUNK, D_MODEL), lambda i, e, f: (e, f, 0)),
    ]
    args += [wg, wu, wd]
    return pl.pallas_call(
        functools.partial(_ffn_kernel, gated=gated, n_e=n_e),
        grid=(rows // tm, n_e, FF_CHUNKS),
        in_specs=in_specs,
        out_specs=row_spec(D_MODEL),
        out_shape=jax.ShapeDtypeStruct((rows, D_MODEL), F32),
        scratch_shapes=[pltpu.VMEM((tm, D_MODEL), F32)],
        compiler_params=_cparams(("arbitrary", "arbitrary", "arbitrary")),
        name="moe_swiglu_post" if gated else "swiglu_post",
    )(*args)


def _moe_plan(rank_row, rank_col):
    n_tok = rank_row.shape[1]
    n_src = n_tok // MOE_SRC
    n_tiles = 2 * n_tok // MOE_TILE + N_EXPERTS
    member = rank_row >= 0
    cnt_blk = member.reshape(N_EXPERTS, n_src, MOE_SRC).sum(-1).astype(jnp.int32)
    cum = jnp.concatenate([jnp.zeros((N_EXPERTS, 1), jnp.int32), jnp.cumsum(cnt_blk, axis=1)], axis=1)
    cnt = cum[:, -1]
    tiles_e = (cnt + MOE_TILE - 1) // MOE_TILE
    tile_end = jnp.cumsum(tiles_e)
    tile_base = tile_end - tiles_e
    slot0 = tile_base[:, None] * MOE_TILE + cum[:, :-1]
    pos_row = jnp.where(member, jnp.repeat(slot0, MOE_SRC, axis=1) + rank_row, -1)
    slot0_col = jnp.pad(slot0.T, ((0, 0), (0, LANES - N_EXPERTS)))
    pos_col = jnp.where(rank_col >= 0, jnp.repeat(slot0_col, MOE_SRC, axis=0) + rank_col, -1)
    d = jnp.arange(n_tiles, dtype=jnp.int32)
    tile_e = jnp.minimum(jnp.sum(d[:, None] >= tile_end[None, :], axis=1), N_EXPERTS - 1).astype(jnp.int32)
    valid = d < tile_end[-1]
    lo_slot = (d - tile_base[tile_e]) * MOE_TILE
    hi_slot = jnp.minimum(lo_slot + MOE_TILE, cnt[tile_e])
    cum_d = cum[tile_e]
    tile_lo = jnp.sum(cum_d[:, 1:] <= lo_slot[:, None], axis=1).astype(jnp.int32)
    tile_hi = jnp.sum(cum_d[:, :-1] < hi_slot[:, None], axis=1).astype(jnp.int32) - 1
    tile_n = jnp.where(valid, tile_hi - tile_lo + 1, 0).astype(jnp.int32)
    tile_lo = jnp.where(valid, tile_lo, 0)
    first = slot0.T
    c_t0 = first // MOE_CTILE
    c_nt = jnp.where(cnt_blk.T > 0, (first + cnt_blk.T - 1) // MOE_CTILE - c_t0 + 1, 0)
    return (pos_row, pos_col, tile_e, tile_lo, tile_n,
            c_t0.reshape(-1).astype(jnp.int32), c_nt.reshape(-1).astype(jnp.int32), n_tiles)


def _moe_ffn_kernel(te_ref, tl_ref, tn_ref, pos_ref, h_hbm, wg_ref, wu_ref, wd_ref, o_ref,
                    hbuf, sem, acc_ref, *, n_tiles):
    d = pl.program_id(0)
    e = te_ref[d]
    lo = tl_ref[d]
    n = tn_ref[d]

    def copy(blk, slot):
        start = pl.multiple_of(blk * MOE_SRC, MOE_SRC)
        return pltpu.make_async_copy(h_hbm.at[pl.ds(start, MOE_SRC), :], hbuf.at[slot], sem.at[slot])

    @pl.when((d == 0) & (n > 0))
    def _():
        copy(lo, 0).start()

    acc_ref[...] = jnp.zeros_like(acc_ref)
    slot_ids = d * MOE_TILE + lax.broadcasted_iota(jnp.int32, (MOE_TILE, 1), 0)

    def gather(j, carry):
        slot = j % 2
        copy(lo + j, slot).wait()

        @pl.when(j + 1 < n)
        def _():
            copy(lo + j + 1, 1 - slot).start()

        start = pl.multiple_of((lo + j) * MOE_SRC, MOE_SRC)
        pos = pos_ref[pl.ds(e, 1), pl.ds(start, MOE_SRC)]
        pick = jnp.where(pos == slot_ids, 1.0, 0.0).astype(BF16)
        acc_ref[...] += _dot(pick, hbuf[slot])
        return carry

    lax.fori_loop(0, n, gather, 0)

    @pl.when(d + 1 < n_tiles)
    def _():
        @pl.when(tn_ref[d + 1] > 0)
        def _():
            copy(tl_ref[d + 1], 0).start()

    @pl.when(n > 0)
    def _():
        x = acc_ref[...].astype(BF16)
        y = None
        for fc in range(FF_CHUNKS):
            cols = slice(fc * FF_CHUNK, (fc + 1) * FF_CHUNK)
            t = _silu(_dot(x, wg_ref[0, :, cols])) * _dot(x, wu_ref[0, :, cols])
            part = _dot(t.astype(BF16), wd_ref[0, cols, :])
            y = part if y is None else y + part
        o_ref[...] = y.astype(BF16)

    @pl.when(n == 0)
    def _():
        o_ref[...] = jnp.zeros_like(o_ref)


def _moe_ffn(h2, pos_row, tile_e, tile_lo, tile_n, wg, wu, wd, *, n_tiles):
    n_tok = h2.shape[0]
    w_spec = lambda shape: pl.BlockSpec((1,) + shape, lambda d, te, tl, tn: (te[d], 0, 0))
    grid_spec = pltpu.PrefetchScalarGridSpec(
        num_scalar_prefetch=3,
        grid=(n_tiles,),
        in_specs=[
            pl.BlockSpec((N_EXPERTS, n_tok), lambda d, te, tl, tn: (0, 0), pipeline_mode=pl.Buffered(1)),
            pl.BlockSpec(memory_space=pl.ANY),
            w_spec((D_MODEL, D_FF)), w_spec((D_MODEL, D_FF)), w_spec((D_FF, D_MODEL)),
        ],
        out_specs=pl.BlockSpec((MOE_TILE, D_MODEL), lambda d, te, tl, tn: (d, 0)),
        scratch_shapes=[pltpu.VMEM((2, MOE_SRC, D_MODEL), BF16),
                        pltpu.SemaphoreType.DMA((2,)),
                        pltpu.VMEM((MOE_TILE, D_MODEL), F32)],
    )
    return pl.pallas_call(
        functools.partial(_moe_ffn_kernel, n_tiles=n_tiles),
        grid_spec=grid_spec,
        out_shape=jax.ShapeDtypeStruct((n_tiles * MOE_TILE, D_MODEL), BF16),
        compiler_params=_cparams(("arbitrary",)),
        name="moe_dispatch_swiglu",
    )(tile_e, tile_lo, tile_n, pos_row, h2, wg, wu, wd)


def _moe_combine_kernel(t0_ref, nt_ref, pos_ref, gates_ref, ys_hbm, x1_ref, mod_ref, g_ref, o_ref,
                        ybuf, sem, acc_ref):
    s = pl.program_id(0)

    def copy(e, k):
        start = pl.multiple_of((t0_ref[s * N_EXPERTS + e] + k) * MOE_CTILE, MOE_CTILE)
        idx = e * MOE_CMAX + k
        return pltpu.make_async_copy(ys_hbm.at[pl.ds(start, MOE_CTILE), :], ybuf.at[idx], sem.at[idx])

    for e in range(N_EXPERTS):
        for k in range(MOE_CMAX):
            @pl.when(nt_ref[s * N_EXPERTS + e] > k)
            def _(e=e, k=k):
                copy(e, k).start()

    acc_ref[...] = jnp.zeros_like(acc_ref)
    lane = lax.broadcasted_iota(jnp.int32, (1, MOE_CTILE), 1)
    for e in range(N_EXPERTS):
        for k in range(MOE_CMAX):
            @pl.when(nt_ref[s * N_EXPERTS + e] > k)
            def _(e=e, k=k):
                copy(e, k).wait()
                base = (t0_ref[s * N_EXPERTS + e] + k) * MOE_CTILE
                pick = jnp.where(pos_ref[:, e:e + 1] == base + lane, 1.0, 0.0).astype(BF16)
                acc_ref[...] += gates_ref[:, e:e + 1] * _dot(pick, ybuf[e * MOE_CMAX + k])

    o_ref[...] = x1_ref[...] + mod_ref[0, 5:6, :] * _rms(acc_ref[...], g_ref[3:4, :])


def _moe_combine(ys, pos_col, gates, c_t0, c_nt, x1, mod, g, *, rows_per_batch):
    n_tok = x1.shape[0]
    tpb = rows_per_batch // MOE_SRC
    nb = mod.shape[0]
    row_spec = lambda w: pl.BlockSpec((MOE_SRC, w), lambda s, t0, nt: (s, 0))
    grid_spec = pltpu.PrefetchScalarGridSpec(
        num_scalar_prefetch=2,
        grid=(n_tok // MOE_SRC,),
        in_specs=[
            row_spec(LANES), row_spec(LANES),
            pl.BlockSpec(memory_space=pl.ANY),
            row_spec(D_MODEL),
            pl.BlockSpec((1, 6, D_MODEL), lambda s, t0, nt: ((s // tpb) % nb, 0, 0)),
            pl.BlockSpec((4, D_MODEL), lambda s, t0, nt: (0, 0), pipeline_mode=pl.Buffered(1)),
        ],
        out_specs=row_spec(D_MODEL),
        scratch_shapes=[pltpu.VMEM((N_EXPERTS * MOE_CMAX, MOE_CTILE, D_MODEL), BF16),
                        pltpu.SemaphoreType.DMA((N_EXPERTS * MOE_CMAX,)),
                        pltpu.VMEM((MOE_SRC, D_MODEL), F32)],
    )
    return pl.pallas_call(
        _moe_combine_kernel,
        grid_spec=grid_spec,
        out_shape=jax.ShapeDtypeStruct((n_tok, D_MODEL), F32),
        compiler_params=_cparams(("arbitrary",)),
        name="moe_combine_post",
    )(c_t0, c_nt, pos_col, gates, ys, x1, mod, g)


def _l1_in_kernel(x_ref, mod_ref, g_ref, w_ref, gate_ref, xr_ref, xq_ref):
    h = _pre(x_ref[...], g_ref[0:1, :], mod_ref[0, 0:1, :], mod_ref[0, 1:2, :]).astype(BF16)
    z = _dot(h, w_ref[...])
    gate_ref[...] = z[:, 0:D_RNN].astype(BF16)
    xr_ref[...] = z[:, D_RNN:2 * D_RNN]
    xq_ref[...] = z[:, 2 * D_RNN:]


def _l1_in(x, mod, g, w_in, *, tm, rows_per_batch):
    rows = x.shape[0]
    tpb = rows_per_batch // tm
    nb = mod.shape[0]
    row_spec = lambda w: pl.BlockSpec((tm, w), lambda i: (i, 0))
    return pl.pallas_call(
        _l1_in_kernel,
        grid=(rows // tm,),
        in_specs=[row_spec(D_MODEL),
                  pl.BlockSpec((1, 6, D_MODEL), lambda i: ((i // tpb) % nb, 0, 0)),
                  _const_spec((4, D_MODEL)),
                  _const_spec((D_MODEL, w_in.shape[1]))],
        out_specs=[row_spec(D_RNN), row_spec(D_RNN), row_spec(D_POOL)],
        out_shape=[jax.ShapeDtypeStruct((rows, D_RNN), BF16),
                   jax.ShapeDtypeStruct((rows, D_RNN), F32),
                   jax.ShapeDtypeStruct((rows, D_POOL), F32)],
        compiler_params=_cparams(("arbitrary",)),
        name="l1_in_proj",
    )(x, mod, g, w_in)


def _with_halo(x_ref, prev_ref, next_ref, t_idx, n_tiles):
    x = x_ref[...]
    prev = jnp.where(t_idx > 0, prev_ref[:, 0], 0.0)
    nxt = jnp.where(t_idx < n_tiles - 1, next_ref[:, 0], 0.0)
    return jnp.concatenate([prev, x, nxt], axis=1)


def _scan_kernel(*refs, fwd, tt, n_tiles, n_seq):
    if fwd:
        (xr_ref, xrp_ref, xrn_ref, xq_ref, xqp_ref, xqn_ref, cw_ref, cb_ref, wgt_ref, ba_ref, bx_ref,
         lam_ref, h0_ref, pw_ref, ps_ref, hf_ref, yp_ref, hl_ref, a_scr, b_scr, h_scr, carry) = refs
    else:
        (xr_ref, xrp_ref, xrn_ref, cw_ref, cb_ref, wgt_ref, ba_ref, bx_ref, lam_ref, h0_ref,
         hf_ref, gate_ref, yr_ref, hl_ref, a_scr, b_scr, h_scr, carry) = refs
    i = pl.program_id(1)
    t_idx = i if fwd else n_tiles - 1 - i
    nb = SUBLANES

    ext = _with_halo(xr_ref, xrp_ref, xrn_ref, t_idx, n_tiles)
    xc = cb_ref[...][None]
    for j in range(CONV_W):
        lo = HALO + j - CONV_LEFT
        xc = xc + cw_ref[j:j + 1, :][None] * ext[:, lo:lo + tt, :]
    xc = xc.reshape(nb * tt, D_RNN)
    xb = xc.astype(BF16)
    r0 = _dot(xb[:, :RNN_HALF], wgt_ref[0])
    r1 = _dot(xb[:, RNN_HALF:], wgt_ref[1])
    ra = jnp.concatenate([r0[:, :RNN_HALF], r1[:, :RNN_HALF]], axis=1) + ba_ref[...]
    ri = jnp.concatenate([r0[:, RNN_HALF:], r1[:, RNN_HALF:]], axis=1) + bx_ref[...]
    nl = -lam_ref[...]
    softplus = jnp.maximum(nl, 0.0) + jnp.log1p(jnp.exp(-jnp.abs(nl)))
    log_a = -RG_C * _sigmoid(ra) * softplus
    a = jnp.exp(log_a)
    th = jnp.tanh(log_a)
    b = jnp.sqrt(-2.0 * th / (1.0 - th)) * (_sigmoid(ri) * xc)
    n_lc = D_RNN // LANES
    for bi in range(nb):
        for lc in range(n_lc):
            a_scr[lc, pl.ds(bi, tt, stride=nb), :] = a[bi * tt:(bi + 1) * tt, lc * LANES:(lc + 1) * LANES]
            b_scr[lc, pl.ds(bi, tt, stride=nb), :] = b[bi * tt:(bi + 1) * tt, lc * LANES:(lc + 1) * LANES]

    @pl.when(i == 0)
    def _():
        for lc in range(n_lc):
            carry[lc] = h0_ref[:, lc * LANES:(lc + 1) * LANES]

    def step(s, h):
        t = s if fwd else tt - 1 - s
        row = pl.multiple_of(t * nb, nb)
        h = a_scr[:, pl.ds(row, nb), :] * h + b_scr[:, pl.ds(row, nb), :]
        h_scr[:, pl.ds(row, nb), :] = h
        return h

    h_last = lax.fori_loop(0, tt, step, carry[...], unroll=8)
    carry[...] = h_last
    for lc in range(n_lc):
        hl_ref[:, lc * LANES:(lc + 1) * LANES] = h_last[lc]

    def unscan(bi):
        return jnp.concatenate([h_scr[lc, pl.ds(bi, tt, stride=nb), :] for lc in range(n_lc)], axis=1)

    if fwd:
        for bi in range(nb):
            hf_ref[bi] = unscan(bi)
        e = _with_halo(xq_ref, xqp_ref, xqn_ref, t_idx, n_tiles)
        length = tt + 2 * HALO
        a2 = e[:, 0:length - 1] + e[:, 1:length]
        a4 = a2[:, 0:length - 3] + a2[:, 2:length - 1]
        a8 = a4[:, 0:length - 7] + a4[:, 4:length - 3]
        a16 = a8[:, 0:length - 15] + a8[:, 8:length - 7]
        lane = lax.broadcasted_iota(jnp.int32, (1, 1, D_POOL), 2)
        wsum = jnp.where(lane < POOL_C, a2[:, HALO - 1:HALO - 1 + tt],
                         jnp.where(lane < 2 * POOL_C, a4[:, HALO - 2:HALO - 2 + tt],
                                   jnp.where(lane < 3 * POOL_C, a8[:, HALO - 4:HALO - 4 + tt],
                                             a16[:, 0:tt])))
        half = jnp.where(lane < POOL_C, POOL_WINDOWS[0] // 2,
                         jnp.where(lane < 2 * POOL_C, POOL_WINDOWS[1] // 2,
                                   jnp.where(lane < 3 * POOL_C, POOL_WINDOWS[2] // 2,
                                             POOL_WINDOWS[3] // 2)))
        tg = t_idx * tt + lax.broadcasted_iota(jnp.int32, (1, tt, D_POOL), 1)
        cnt = (jnp.minimum(tg + half, n_seq) - jnp.maximum(tg - half, 0)).astype(F32)
        dlt = (wsum / cnt - xq_ref[...]).astype(BF16).reshape(nb * tt, D_POOL)
        yp = _dot(dlt, pw_ref[...]) * ps_ref[...]
        yp_ref[...] = yp.reshape(nb, tt, D_POOL).astype(BF16)
    else:
        for bi in range(nb):
            hb = unscan(bi)
            yr_ref[bi] = ((hf_ref[bi] + hb) * _gelu(gate_ref[bi].astype(F32))).astype(BF16)


def _scan(fwd, xr, xq, conv_w, conv_b, w_gates, ba, bx, lam, h0, pool_w, pool_scale, h_f, gate, *, tt):
    n_batch, n_seq, _ = xr.shape
    n_bg = n_batch // SUBLANES
    n_tiles = n_seq // tt
    hb = tt // HALO
    n_hb = n_seq // HALO

    def t_of(i):
        return i if fwd else n_tiles - 1 - i

    def tile(c):
        return pl.BlockSpec((SUBLANES, tt, c), lambda b, i: (b, t_of(i), 0))

    def prev(c):
        return pl.BlockSpec((SUBLANES, 1, HALO, c),
                            lambda b, i: (b, jnp.maximum(t_of(i) * hb - 1, 0), 0, 0))

    def nxt(c):
        return pl.BlockSpec((SUBLANES, 1, HALO, c),
                            lambda b, i: (b, jnp.minimum((t_of(i) + 1) * hb, n_hb - 1), 0, 0))

    state = pl.BlockSpec((SUBLANES, D_RNN), lambda b, i: (b, 0))
    xr4 = xr.reshape(n_batch, n_hb, HALO, D_RNN)
    common = [_const_spec((CONV_W, D_RNN)), _const_spec((1, D_RNN)),
              _const_spec((2, RNN_HALF, D_RNN)), _const_spec((1, D_RNN)), _const_spec((1, D_RNN)),
              _const_spec((1, D_RNN)), state]
    common_args = [conv_w, conv_b, w_gates, ba, bx, lam, h0]
    n_lc = D_RNN // LANES
    scratch = ([pltpu.VMEM((n_lc, SUBLANES * tt, LANES), F32)] * 3
               + [pltpu.VMEM((n_lc, SUBLANES, LANES), F32)])
    if fwd:
        xq4 = xq.reshape(n_batch, n_hb, HALO, D_POOL)
        in_specs = ([tile(D_RNN), prev(D_RNN), nxt(D_RNN), tile(D_POOL), prev(D_POOL), nxt(D_POOL)]
                    + common + [_const_spec((D_POOL, D_POOL)), _const_spec((1, D_POOL))])
        args = [xr, xr4, xr4, xq, xq4, xq4] + common_args + [pool_w, pool_scale]
        out_specs = [tile(D_RNN), tile(D_POOL), state]
        out_shape = [jax.ShapeDtypeStruct((n_batch, n_seq, D_RNN), F32),
                     jax.ShapeDtypeStruct((n_batch, n_seq, D_POOL), BF16),
                     jax.ShapeDtypeStruct((n_batch, D_RNN), F32)]
    else:
        in_specs = [tile(D_RNN), prev(D_RNN), nxt(D_RNN)] + common + [tile(D_RNN), tile(D_RNN)]
        args = [xr, xr4, xr4] + common_args + [h_f, gate]
        out_specs = [tile(D_RNN), state]
        out_shape = [jax.ShapeDtypeStruct((n_batch, n_seq, D_RNN), BF16),
                     jax.ShapeDtypeStruct((n_batch, D_RNN), F32)]
    return pl.pallas_call(
        functools.partial(_scan_kernel, fwd=fwd, tt=tt, n_tiles=n_tiles, n_seq=n_seq),
        grid=(n_bg, n_tiles),
        in_specs=in_specs,
        out_specs=out_specs,
        out_shape=out_shape,
        scratch_shapes=scratch,
        compiler_params=_cparams(("arbitrary", "arbitrary")),
        name="rglru_forward_pool" if fwd else "rglru_backward_combine",
    )(*args)


def _gate_weights(wa, wx):
    per_half = RNN_BLOCKS // 2

    def half_dense(w, k):
        blocks = [w[k * per_half + b] for b in range(per_half)]
        rows = []
        for bi, blk in enumerate(blocks):
            row = [blk if bj == bi else jnp.zeros_like(blk) for bj in range(per_half)]
            rows.append(jnp.concatenate(row, axis=1))
        return jnp.concatenate(rows, axis=0)

    return jnp.stack([jnp.concatenate([half_dense(wa, k), half_dense(wx, k)], axis=1)
                      for k in range(2)]).astype(BF16)


def _pool_weights(pool_w):
    n = pool_w.shape[0]
    rows = []
    for i in range(n):
        rows.append(jnp.concatenate([pool_w[i] if j == i else jnp.zeros_like(pool_w[i])
                                     for j in range(n)], axis=1))
    return jnp.concatenate(rows, axis=0).astype(BF16)


def kernel(x_prompt, x_sample, cache_k, cache_v, state_h, c, c_ctx, w_ada, b_ada, norm_g, w_in_even,
           w_out_even, na_rpb, sgu_w, sgu_b, w_in_odd, w_out_odd, conv_w, conv_b, rg_wa, rg_ba, rg_wx,
           rg_bx, rg_lam, pool_w, pool_scale, ffn_wg, ffn_wu, ffn_wd, router_w, moe_wg, moe_wu, moe_wd):
    bp, n_p, _ = x_prompt.shape
    bs, n_s, _ = x_sample.shape
    n_ctx = cache_k.shape[3]

    cond = jnp.concatenate([c_ctx[None, :], c, jnp.zeros((2 * SUBLANES - 1 - bs, D_MODEL), F32)], axis=0)
    mods = _modulation(cond, w_ada, b_ada).reshape(w_ada.shape[0], cond.shape[0], 6, D_MODEL)

    streams = [
        dict(x=x_prompt.reshape(bp * n_p, D_MODEL), nb=bp, n=n_p, tm=256, latent=False),
        dict(x=x_sample.reshape(bs * n_s, D_MODEL), nb=bs, n=n_s, tm=512, latent=True),
    ]
    new_k = new_v = new_h = None

    g = norm_g[0]
    w_in = w_in_even[0].astype(BF16)
    w_out = w_out_even[0].astype(BF16)
    w_s = sgu_w[0].astype(BF16)
    b_s_t = sgu_b[0].T
    wg, wu, wd = (w[0:1].astype(BF16) for w in (ffn_wg, ffn_wu, ffn_wd))
    bias = _latent_bias_table(na_rpb[0], n_s // GRID_W)
    kc = cache_k[:, 0].transpose(0, 2, 1, 3).reshape(bs * n_ctx, W_A).astype(BF16)
    vc = cache_v[:, 0].transpose(0, 2, 1, 3).reshape(bs * n_ctx, W_A).astype(BF16)
    for st in streams:
        mod = mods[0, 1:1 + bs] if st["latent"] else mods[0, 0:1]
        outs = _l0_in(st["x"], mod, g, w_in, w_s, b_s_t, tm=st["tm"], rows_per_batch=st["n"],
                      emit_f32=not st["latent"])
        q, k, v, sg = outs[:4]
        if st["latent"]:
            oa = _attn_latent(q, k, v, kc, vc, bias, n_batch=st["nb"], n_seq=st["n"], n_ctx=n_ctx)
        else:
            oa = _attn_ctx(q, k, v, n_batch=st["nb"], n_seq=st["n"])
            heads = lambda a: a.reshape(bp, n_p, N_HEADS, HEAD_DIM).transpose(0, 2, 1, 3)[:, None]
            new_k, new_v = heads(outs[4]), heads(outs[5])
        x1, h2 = _out_proj([oa, sg], st["x"], mod, g, w_out, None, tm=st["tm"], rows_per_batch=st["n"])
        st["x"] = _ffn(h2, x1, mod, g, None, wg, wu, wd, tm=st["tm"], rows_per_batch=st["n"])

    g = norm_g[1]
    w_in = w_in_odd[0].astype(BF16)
    w_out = w_out_odd[0].astype(BF16)
    w_gates = [_gate_weights(rg_wa[0, d], rg_wx[0, d]) for d in range(2)]
    pw = _pool_weights(pool_w[0])
    ps = pool_scale[0][None, :]
    rw = jnp.pad(router_w[0].T, ((0, LANES - N_EXPERTS), (0, 0)))
    wg, wu, wd = (w[0].astype(BF16) for w in (moe_wg, moe_wu, moe_wd))
    for st in streams:
        mod = mods[1, 1:1 + bs] if st["latent"] else mods[1, 0:1]
        nb, n = st["nb"], st["n"]
        gate, xr, xq = _l1_in(st["x"], mod, g, w_in, tm=st["tm"], rows_per_batch=n)
        gate, xr, xq = (a.reshape(nb, n, a.shape[1]) for a in (gate, xr, xq))
        h0 = state_h[:, 0].astype(F32) if st["latent"] else jnp.zeros((nb, 2, D_RNN), F32)
        scan_args = lambda d: (conv_w[0], conv_b[0][None, :], w_gates[d], rg_ba[0, d][None, :],
                               rg_bx[0, d][None, :], rg_lam[0, d][None, :], h0[:, d])
        h_f, y_pool, h_f_last = _scan(True, xr, xq, *scan_args(0), pw, ps, None, None, tt=128)
        y_rec, h_b_last = _scan(False, xr, None, *scan_args(1), None, None, h_f, gate, tt=128)
        if not st["latent"]:
            new_h = jnp.stack([h_f_last, h_b_last], axis=1)[:, None]
        x1, h2, gates, rank_col, rank_row = _out_proj(
            [y_rec.reshape(nb * n, D_RNN), y_pool.reshape(nb * n, D_POOL)],
            st["x"], mod, g, w_out, rw, tm=MOE_SRC, rows_per_batch=n)
        pos_row, pos_col, tile_e, tile_lo, tile_n, c_t0, c_nt, n_tiles = _moe_plan(rank_row, rank_col)
        ys = _moe_ffn(h2, pos_row, tile_e, tile_lo, tile_n, wg, wu, wd, n_tiles=n_tiles)
        st["x"] = _moe_combine(ys, pos_col, gates, c_t0, c_nt, x1, mod, g, rows_per_batch=n)

    y_prompt = streams[0]["x"].reshape(bp, n_p, D_MODEL)
    y_sample = streams[1]["x"].reshape(bs, n_s, D_MODEL)
    return (y_prompt, y_sample, new_k, new_v, new_h)
```

```python
import functools

import jax
import jax.numpy as jnp
import numpy as np
from jax import lax
from jax.experimental import pallas as pl
from jax.experimental.pallas import tpu as pltpu

F32 = jnp.float32
BF16 = jnp.bfloat16

D_MODEL = 1024
GRID_W = 64
HEAD_DIM = 64
N_HEADS = 8
W_A = N_HEADS * HEAD_DIM
WIN_R = 8
WIN_C = 16
N_GROUPS_B = 4
W_B = 512
CHUNK = 128
D_RNN = 768
RNN_BLOCKS = 8
RNN_BW = D_RNN // RNN_BLOCKS
RNN_HALF = D_RNN // 2
CONV_W = 4
CONV_LEFT = 2
RG_C = 8.0
D_POOL = 256
POOL_WINDOWS = (2, 4, 8, 16)
POOL_C = D_POOL // len(POOL_WINDOWS)
D_FF = 2816
N_EXPERTS = 8
EPS = 1e-6
NEG = -1e30

LANES = 128
SUBLANES = 8
VMEM_LIMIT = 56 * 1024 * 1024
Q_ROWS = 4
KEY_ROWS = Q_ROWS + WIN_R - 1
HALO = 8
FF_CHUNKS = 2
FF_CHUNK = D_FF // FF_CHUNKS
MOE_SRC = 256
MOE_TILE = 256
MOE_RING = 8
MOE_CTILE = 128
MOE_CMAX = (MOE_SRC - 1 + MOE_CTILE - 1) // MOE_CTILE + 1


def _cparams(sem):
    return pltpu.CompilerParams(dimension_semantics=sem, vmem_limit_bytes=VMEM_LIMIT)


def _const_spec(shape):
    nd = len(shape)
    return pl.BlockSpec(shape, lambda *_: (0,) * nd, pipeline_mode=pl.Buffered(1))


def _dot(a, b):
    return jnp.dot(a, b, preferred_element_type=F32)


def _dot_nt(a, b):
    return lax.dot_general(a, b, (((1,), (1,)), ((), ())), preferred_element_type=F32)


def _rms(x, g):
    ms = jnp.mean(x * x, axis=-1, keepdims=True)
    return x * lax.rsqrt(ms + EPS) * g


def _pre(x, g, shift, scale):
    return _rms(x, g) * (1.0 + scale) + shift


def _gelu(x):
    return jax.nn.gelu(x, approximate=True)


def _sigmoid(x):
    return 1.0 / (1.0 + jnp.exp(-x))


def _silu(x):
    return x * _sigmoid(x)


def _mod_kernel(c_ref, w_ref, b_ref, o_ref):
    s = _silu(c_ref[...])
    o_ref[0] = jnp.dot(s, w_ref[0], preferred_element_type=F32,
                       precision=lax.Precision.HIGHEST) + b_ref[0]


def _modulation(cond, w_ada, b_ada):
    depth, d, n = w_ada.shape
    rows = cond.shape[0]
    bn = 768
    return pl.pallas_call(
        _mod_kernel,
        grid=(depth, n // bn),
        in_specs=[
            pl.BlockSpec((rows, d), lambda i, j: (0, 0)),
            pl.BlockSpec((1, d, bn), lambda i, j: (i, 0, j)),
            pl.BlockSpec((1, 1, bn), lambda i, j: (i, 0, j)),
        ],
        out_specs=pl.BlockSpec((1, rows, bn), lambda i, j: (i, 0, j)),
        out_shape=jax.ShapeDtypeStruct((depth, rows, n), F32),
        compiler_params=_cparams(("arbitrary", "arbitrary")),
        name="adaln_modulation",
    )(cond, w_ada, b_ada.reshape(depth, 1, n))


def _l0_in_kernel(x_ref, mod_ref, g_ref, w_ref, ws_ref, bs_ref, *out_refs, tm, emit_f32):
    q_ref, k_ref, v_ref, sg_ref = out_refs[:4]
    x = x_ref[...]
    h = _pre(x, g_ref[0:1, :], mod_ref[0, 0:1, :], mod_ref[0, 1:2, :]).astype(BF16)
    z = _dot(h, w_ref[...])
    q_ref[...] = z[:, 0:W_A].astype(BF16)
    k_ref[...] = z[:, W_A:2 * W_A].astype(BF16)
    v_ref[...] = z[:, 2 * W_A:3 * W_A].astype(BF16)
    if emit_f32:
        out_refs[4][...] = z[:, W_A:2 * W_A]
        out_refs[5][...] = z[:, 2 * W_A:3 * W_A]
    u = _gelu(z[:, 3 * W_A:3 * W_A + W_B])
    gf = _gelu(z[:, 3 * W_A + W_B:])
    n_chunks = tm // CHUNK
    for gi in range(N_GROUPS_B):
        gg = gf[:, gi * LANES:(gi + 1) * LANES]
        mu = jnp.mean(gg, axis=-1, keepdims=True)
        dd = gg - mu
        var = jnp.mean(dd * dd, axis=-1, keepdims=True)
        gn = (dd * lax.rsqrt(var + EPS)).astype(BF16)
        rhs = jnp.concatenate([gn[c * CHUNK:(c + 1) * CHUNK, :] for c in range(n_chunks)], axis=1)
        s = _dot(ws_ref[gi], rhs) + bs_ref[:, gi:gi + 1]
        for c in range(n_chunks):
            uu = u[c * CHUNK:(c + 1) * CHUNK, gi * LANES:(gi + 1) * LANES]
            sg_ref[c * CHUNK:(c + 1) * CHUNK, gi * LANES:(gi + 1) * LANES] = (
                uu * s[:, c * LANES:(c + 1) * LANES]).astype(BF16)


def _l0_in(x, mod, g, w_in, w_s, b_s_t, *, tm, rows_per_batch, emit_f32):
    rows = x.shape[0]
    tpb = rows_per_batch // tm
    nb = mod.shape[0]
    d_in = w_in.shape[1]
    row_spec = lambda w: pl.BlockSpec((tm, w), lambda i: (i, 0))
    out_shape = [jax.ShapeDtypeStruct((rows, W_A), BF16)] * 3 + [jax.ShapeDtypeStruct((rows, W_B), BF16)]
    out_specs = [row_spec(W_A)] * 3 + [row_spec(W_B)]
    if emit_f32:
        out_shape += [jax.ShapeDtypeStruct((rows, W_A), F32)] * 2
        out_specs += [row_spec(W_A)] * 2
    return pl.pallas_call(
        functools.partial(_l0_in_kernel, tm=tm, emit_f32=emit_f32),
        grid=(rows // tm,),
        in_specs=[
            row_spec(D_MODEL),
            pl.BlockSpec((1, 6, D_MODEL), lambda i: ((i // tpb) % nb, 0, 0)),
            _const_spec((4, D_MODEL)),
            _const_spec((D_MODEL, d_in)),
            _const_spec((N_GROUPS_B, CHUNK, CHUNK)),
            _const_spec((CHUNK, N_GROUPS_B)),
        ],
        out_specs=out_specs,
        out_shape=out_shape,
        compiler_params=_cparams(("arbitrary",)),
        name="l0_in_proj_sgu",
    )(x, mod, g, w_in, w_s, b_s_t)


def _attn_kernel(*refs, n_local, has_ctx, rows):
    if has_ctx:
        q_ref, k_ref, v_ref, kc_ref, vc_ref, bias_ref, o_ref = refs
        r0 = pl.program_id(2) * Q_ROWS
        ks = jnp.clip(r0 - WIN_R // 2, 0, rows - KEY_ROWS)
        start = pl.multiple_of(ks * GRID_W, GRID_W)
        kl = k_ref[pl.ds(start, n_local), :]
        vl = v_ref[pl.ds(start, n_local), :]
    else:
        q_ref, k_ref, v_ref, o_ref = refs
        kl = k_ref[...]
        vl = v_ref[...]
    q = q_ref[...] * jnp.asarray(HEAD_DIM ** -0.5, BF16)
    first = lax.broadcasted_iota(jnp.int32, (1, LANES), 1) < HEAD_DIM
    outs = []
    for half in range(2):
        qh = jnp.where(first if half == 0 else jnp.logical_not(first), q, jnp.zeros_like(q))
        s = _dot_nt(qh, kl)
        if has_ctx:
            s = s + bias_ref[0, half]
            sc = _dot_nt(qh, kc_ref[...])
            m = jnp.maximum(jnp.max(s, axis=-1, keepdims=True), jnp.max(sc, axis=-1, keepdims=True))
        else:
            m = jnp.max(s, axis=-1, keepdims=True)
        p = jnp.exp(s - m)
        l = jnp.sum(p, axis=-1, keepdims=True)
        acc = _dot(p.astype(BF16), vl)
        if has_ctx:
            pc = jnp.exp(sc - m)
            l = l + jnp.sum(pc, axis=-1, keepdims=True)
            acc = acc + _dot(pc.astype(BF16), vc_ref[...])
        outs.append(acc / l)
    o_ref[...] = jnp.where(first, outs[0], outs[1]).astype(BF16)


def _attn_ctx(q, k, v, *, n_batch, n_seq):
    rows = q.shape[0]
    spec = pl.BlockSpec((n_seq, LANES), lambda b, hp: (b, hp))
    return pl.pallas_call(
        functools.partial(_attn_kernel, n_local=n_seq, has_ctx=False, rows=0),
        grid=(n_batch, W_A // LANES),
        in_specs=[spec, spec, spec],
        out_specs=spec,
        out_shape=jax.ShapeDtypeStruct((rows, W_A), BF16),
        compiler_params=_cparams(("arbitrary", "arbitrary")),
        name="attn_context",
    )(q, k, v)


def _attn_latent(q, k, v, kc, vc, bias, *, n_batch, n_seq, n_ctx):
    rows = n_seq // GRID_W
    n_rg = rows // Q_ROWS
    tq = Q_ROWS * GRID_W
    n_local = KEY_ROWS * GRID_W
    img_spec = pl.BlockSpec((n_seq, LANES), lambda b, hp, rg: (b, hp))
    ctx_spec = pl.BlockSpec((n_ctx, LANES), lambda b, hp, rg: (b, hp))
    q_spec = pl.BlockSpec((tq, LANES), lambda b, hp, rg: (b * n_rg + rg, hp))

    def bias_map(b, hp, rg):
        cfg = jnp.where(rg == 0, 0, jnp.where(rg == n_rg - 1, 2, 1))
        return (cfg, hp, 0, 0)

    return pl.pallas_call(
        functools.partial(_attn_kernel, n_local=n_local, has_ctx=True, rows=rows),
        grid=(n_batch, W_A // LANES, n_rg),
        in_specs=[q_spec, img_spec, img_spec, ctx_spec, ctx_spec,
                  pl.BlockSpec((1, 2, tq, n_local), bias_map)],
        out_specs=q_spec,
        out_shape=jax.ShapeDtypeStruct((n_batch * n_seq, W_A), BF16),
        compiler_params=_cparams(("arbitrary", "arbitrary", "arbitrary")),
        name="attn_latent",
    )(q, k, v, kc, vc, bias)


def _latent_bias_table(rpb, rows):
    n_heads = rpb.shape[0]
    qcol = np.arange(GRID_W)[:, None]
    kcol = np.arange(GRID_W)[None, :]
    cs = np.clip(qcol - WIN_C // 2, 0, GRID_W - WIN_C)
    col_valid = (kcol >= cs) & (kcol < cs + WIN_C)
    dc = np.clip(kcol - qcol + (WIN_C - 1), 0, 2 * WIN_C - 2)
    pick = (dc[:, :, None] == np.arange(2 * WIN_C - 1)).astype(np.float32)
    col_tab = jnp.einsum("hrc,qkc->hrqk", rpb.astype(F32), pick, precision=lax.Precision.HIGHEST)
    col_tab = jnp.where(col_valid[None, None], col_tab, NEG)
    tables = []
    for r0 in (0, 2 * Q_ROWS, rows - Q_ROWS):
        ks = min(max(r0 - WIN_R // 2, 0), rows - KEY_ROWS)
        qrow = r0 + np.arange(Q_ROWS)[:, None]
        krow = ks + np.arange(KEY_ROWS)[None, :]
        rs = np.clip(qrow - WIN_R // 2, 0, rows - WIN_R)
        row_valid = (krow >= rs) & (krow < rs + WIN_R)
        dr = np.clip(krow - qrow + (WIN_R - 1), 0, 2 * WIN_R - 2)
        blocks = [jnp.stack([col_tab[:, dr[i, k]] if row_valid[i, k]
                             else jnp.full((n_heads, GRID_W, GRID_W), NEG, F32)
                             for k in range(KEY_ROWS)], axis=2)
                  for i in range(Q_ROWS)]
        b = jnp.stack(blocks, axis=1)
        tables.append(b.reshape(n_heads, Q_ROWS * GRID_W, KEY_ROWS * GRID_W))
    return jnp.stack(tables)


def _out_kernel(*refs, n_in, router):
    in_refs = refs[:n_in]
    x_ref, mod_ref, g_ref, w_ref = refs[n_in:n_in + 4]
    rest = refs[n_in + 4:]
    if router:
        rw_ref, x1_ref, h2_ref, gates_ref, rankc_ref, rankr_ref = rest
    else:
        x1_ref, h2_ref = rest
    y = None
    off = 0
    for r in in_refs:
        w = r.shape[1]
        part = _dot(r[...], w_ref[off:off + w, :])
        y = part if y is None else y + part
        off += w
    x1 = x_ref[...] + mod_ref[0, 2:3, :] * _rms(y, g_ref[1:2, :])
    x1_ref[...] = x1
    h2 = _pre(x1, g_ref[2:3, :], mod_ref[0, 3:4, :], mod_ref[0, 4:5, :])
    h2_ref[...] = h2.astype(BF16)
    if router:
        h_hi = h2.astype(BF16)
        h_lo = (h2 - h_hi.astype(F32)).astype(BF16)
        rw = rw_ref[...]
        r_hi = rw.astype(BF16)
        r_lo = (rw - r_hi.astype(F32)).astype(BF16)
        logits = _dot_nt(r_hi, h_hi) + (_dot_nt(r_hi, h_lo) + _dot_nt(r_lo, h_hi))
        tm = logits.shape[1]
        row = lax.broadcasted_iota(jnp.int32, logits.shape, 0).astype(F32)
        logits = jnp.where(row < N_EXPERTS, logits, -jnp.inf)
        m1 = jnp.max(logits, axis=0, keepdims=True)
        i1 = jnp.min(jnp.where(logits == m1, row, float(LANES)), axis=0, keepdims=True)
        rest_l = jnp.where(row == i1, -jnp.inf, logits)
        m2 = jnp.max(rest_l, axis=0, keepdims=True)
        i2 = jnp.min(jnp.where(rest_l == m2, row, float(LANES)), axis=0, keepdims=True)
        e2 = jnp.exp(m2 - m1)
        w1 = 1.0 / (1.0 + e2)
        w2 = e2 / (1.0 + e2)
        sel1 = row == i1
        sel2 = row == i2
        member = jnp.where(sel1, 1.0, 0.0) + jnp.where(sel2, 1.0, 0.0)
        gates = jnp.where(sel1, w1, 0.0) + jnp.where(sel2, w2, 0.0)
        before = (lax.broadcasted_iota(jnp.int32, (tm, tm), 0)
                  < lax.broadcasted_iota(jnp.int32, (tm, tm), 1))
        rank = _dot(member.astype(BF16), jnp.where(before, 1.0, 0.0).astype(BF16))
        rank = jnp.where(member > 0.0, rank, -1.0)
        gates_ref[...] = gates.T
        rankc_ref[...] = rank.T.astype(jnp.int32)
        rankr_ref[...] = rank[0:N_EXPERTS, :].astype(jnp.int32)


def _out_proj(ins, x, mod, g, w_out, router_w, *, tm, rows_per_batch):
    rows = x.shape[0]
    tpb = rows_per_batch // tm
    nb = mod.shape[0]
    router = router_w is not None
    row_spec = lambda w: pl.BlockSpec((tm, w), lambda i: (i, 0))
    in_specs = [row_spec(a.shape[1]) for a in ins] + [
        row_spec(D_MODEL),
        pl.BlockSpec((1, 6, D_MODEL), lambda i: ((i // tpb) % nb, 0, 0)),
        _const_spec((4, D_MODEL)),
        _const_spec((D_MODEL, D_MODEL)),
    ]
    args = list(ins) + [x, mod, g, w_out]
    out_shape = [jax.ShapeDtypeStruct((rows, D_MODEL), F32), jax.ShapeDtypeStruct((rows, D_MODEL), BF16)]
    out_specs = [row_spec(D_MODEL), row_spec(D_MODEL)]
    if router:
        in_specs.append(_const_spec((LANES, D_MODEL)))
        args.append(router_w)
        out_shape += [jax.ShapeDtypeStruct((rows, LANES), F32),
                      jax.ShapeDtypeStruct((rows, LANES), jnp.int32),
                      jax.ShapeDtypeStruct((N_EXPERTS, rows), jnp.int32)]
        out_specs += [row_spec(LANES), row_spec(LANES),
                      pl.BlockSpec((N_EXPERTS, tm), lambda i: (0, i))]
    return pl.pallas_call(
        functools.partial(_out_kernel, n_in=len(ins), router=router),
        grid=(rows // tm,),
        in_specs=in_specs,
        out_specs=out_specs,
        out_shape=out_shape,
        compiler_params=_cparams(("arbitrary",)),
        name="out_proj_norms_router" if router else "out_proj_norms",
    )(*args)


def _ffn_kernel(*refs, gated, n_e):
    if gated:
        h_ref, x1_ref, mod_ref, g_ref, gates_ref, wg_ref, wu_ref, wd_ref, o_ref, acc_ref = refs
    else:
        h_ref, x1_ref, mod_ref, g_ref, wg_ref, wu_ref, wd_ref, o_ref, acc_ref = refs
    e = pl.program_id(1)
    fc = pl.program_id(2)

    @pl.when((e == 0) & (fc == 0))
    def _():
        acc_ref[...] = jnp.zeros_like(acc_ref)

    h = h_ref[...]
    t = _silu(_dot(h, wg_ref[0])) * _dot(h, wu_ref[0])
    y = _dot(t.astype(BF16), wd_ref[0])
    if gated:
        gates = gates_ref[...]
        lane = lax.broadcasted_iota(jnp.int32, gates.shape, 1)
        y = y * jnp.sum(jnp.where(lane == e, gates, 0.0), axis=-1, keepdims=True)
    acc_ref[...] += y

    @pl.when((e == n_e - 1) & (fc == FF_CHUNKS - 1))
    def _():
        o_ref[...] = x1_ref[...] + mod_ref[0, 5:6, :] * _rms(acc_ref[...], g_ref[3:4, :])


def _ffn(h2, x1, mod, g, gates, wg, wu, wd, *, tm, rows_per_batch):
    rows = h2.shape[0]
    tpb = rows_per_batch // tm
    nb = mod.shape[0]
    n_e = wg.shape[0]
    gated = gates is not None
    row_spec = lambda w: pl.BlockSpec((tm, w), lambda i, e, f: (i, 0))
    in_specs = [row_spec(D_MODEL), row_spec(D_MODEL),
                pl.BlockSpec((1, 6, D_MODEL), lambda i, e, f: ((i // tpb) % nb, 0, 0)),
                _const_spec((4, D_MODEL))]
    args = [h2, x1, mod, g]
    if gated:
        in_specs.append(row_spec(LANES))
        args.append(gates)
    in_specs += [
        pl.BlockSpec((1, D_MODEL, FF_CHUNK), lambda i, e, f: (e, 0, f)),
        pl.BlockSpec((1, D_MODEL, FF_CHUNK), lambda i, e, f: (e, 0, f)),
        pl.BlockSpec((1, FF_CHUNK, D_MODEL), lambda i, e, f: (e, f, 0)),
    ]
    args += [wg, wu, wd]
    return pl.pallas_call(
        functools.partial(_ffn_kernel, gated=gated, n_e=n_e),
        grid=(rows // tm, n_e, FF_CHUNKS),
        in_specs=in_specs,
        out_specs=row_spec(D_MODEL),
        out_shape=jax.ShapeDtypeStruct((rows, D_MODEL), F32),
        scratch_shapes=[pltpu.VMEM((tm, D_MODEL), F32)],
        compiler_params=_cparams(("arbitrary", "arbitrary", "arbitrary")),
        name="moe_swiglu_post" if gated else "swiglu_post",
    )(*args)


def _moe_plan(rank_row, rank_col):
    n_tok = rank_row.shape[1]
    n_src = n_tok // MOE_SRC
    n_tiles = 2 * n_tok // MOE_TILE + N_EXPERTS
    member = rank_row >= 0
    cnt_blk = member.reshape(N_EXPERTS, n_src, MOE_SRC).sum(-1).astype(jnp.int32)
    cum = jnp.concatenate([jnp.zeros((N_EXPERTS, 1), jnp.int32), jnp.cumsum(cnt_blk, axis=1)], axis=1)
    cnt = cum[:, -1]
    tiles_e = (cnt + MOE_TILE - 1) // MOE_TILE
    tile_end = jnp.cumsum(tiles_e)
    tile_base = tile_end - tiles_e
    slot0 = tile_base[:, None] * MOE_TILE + cum[:, :-1]
    pos_row = jnp.where(member, jnp.repeat(slot0, MOE_SRC, axis=1) + rank_row, -1)
    slot0_col = jnp.pad(slot0.T, ((0, 0), (0, LANES - N_EXPERTS)))
    pos_col = jnp.where(rank_col >= 0, jnp.repeat(slot0_col, MOE_SRC, axis=0) + rank_col, -1)
    d = jnp.arange(n_tiles, dtype=jnp.int32)
    tile_e = jnp.minimum(jnp.sum(d[:, None] >= tile_end[None, :], axis=1), N_EXPERTS - 1).astype(jnp.int32)
    valid = d < tile_end[-1]
    lo_slot = (d - tile_base[tile_e]) * MOE_TILE
    hi_slot = jnp.minimum(lo_slot + MOE_TILE, cnt[tile_e])
    cum_d = cum[tile_e]
    tile_lo = jnp.sum(cum_d[:, 1:] <= lo_slot[:, None], axis=1).astype(jnp.int32)
    tile_hi = jnp.sum(cum_d[:, :-1] < hi_slot[:, None], axis=1).astype(jnp.int32) - 1
    tile_n = jnp.where(valid, tile_hi - tile_lo + 1, 0).astype(jnp.int32)
    tile_lo = jnp.where(valid, tile_lo, 0)
    first = slot0.T
    c_t0 = first // MOE_CTILE
    c_nt = jnp.where(cnt_blk.T > 0, (first + cnt_blk.T - 1) // MOE_CTILE - c_t0 + 1, 0)
    return (pos_row, pos_col, tile_e, tile_lo, tile_n,
            c_t0.reshape(-1).astype(jnp.int32), c_nt.reshape(-1).astype(jnp.int32), n_tiles)


def _moe_ffn_kernel(te_ref, tl_ref, tn_ref, pos_ref, h_hbm, wg_ref, wu_ref, wd_ref, o_ref,
                    hbuf, sem, acc_ref, *, n_tiles):
    d = pl.program_id(0)
    e = te_ref[d]
    lo = tl_ref[d]
    n = tn_ref[d]

    def copy(blk, slot):
        start = pl.multiple_of(blk * MOE_SRC, MOE_SRC)
        return pltpu.make_async_copy(h_hbm.at[pl.ds(start, MOE_SRC), :], hbuf.at[slot], sem.at[slot])

    def start_first(first_blk, count):
        for k in range(MOE_RING):
            @pl.when(k < count)
            def _(k=k):
                copy(first_blk + k, k).start()

    @pl.when(d == 0)
    def _():
        start_first(lo, n)

    acc_ref[...] = jnp.zeros_like(acc_ref)
    slot_ids = d * MOE_TILE + lax.broadcasted_iota(jnp.int32, (MOE_TILE, 1), 0)

    def gather(j, carry):
        slot = j % MOE_RING
        copy(lo + j, slot).wait()
        start = pl.multiple_of((lo + j) * MOE_SRC, MOE_SRC)
        pos = pos_ref[pl.ds(e, 1), pl.ds(start, MOE_SRC)]
        pick = jnp.where(pos == slot_ids, 1.0, 0.0).astype(BF16)
        acc_ref[...] += _dot(pick, hbuf[slot])

        @pl.when(j + MOE_RING < n)
        def _():
            copy(lo + j + MOE_RING, slot).start()

        return carry

    lax.fori_loop(0, n, gather, 0)

    @pl.when(d + 1 < n_tiles)
    def _():
        start_first(tl_ref[d + 1], tn_ref[d + 1])

    @pl.when(n > 0)
    def _():
        x = acc_ref[...].astype(BF16)
        y = None
        for fc in range(FF_CHUNKS):
            cols = slice(fc * FF_CHUNK, (fc + 1) * FF_CHUNK)
            t = _silu(_dot(x, wg_ref[0, :, cols])) * _dot(x, wu_ref[0, :, cols])
            part = _dot(t.astype(BF16), wd_ref[0, cols, :])
            y = part if y is None else y + part
        o_ref[...] = y.astype(BF16)

    @pl.when(n == 0)
    def _():
        o_ref[...] = jnp.zeros_like(o_ref)


def _moe_ffn(h2, pos_row, tile_e, tile_lo, tile_n, wg, wu, wd, *, n_tiles):
    n_tok = h2.shape[0]
    w_spec = lambda shape: pl.BlockSpec((1,) + shape, lambda d, te, tl, tn: (te[d], 0, 0))
    grid_spec = pltpu.PrefetchScalarGridSpec(
        num_scalar_prefetch=3,
        grid=(n_tiles,),
        in_specs=[
            pl.BlockSpec((N_EXPERTS, n_tok), lambda d, te, tl, tn: (0, 0), pipeline_mode=pl.Buffered(1)),
            pl.BlockSpec(memory_space=pl.ANY),
            w_spec((D_MODEL, D_FF)), w_spec((D_MODEL, D_FF)), w_spec((D_FF, D_MODEL)),
        ],
        out_specs=pl.BlockSpec((MOE_TILE, D_MODEL), lambda d, te, tl, tn: (d, 0)),
        scratch_shapes=[pltpu.VMEM((MOE_RING, MOE_SRC, D_MODEL), BF16),
                        pltpu.SemaphoreType.DMA((MOE_RING,)),
                        pltpu.VMEM((MOE_TILE, D_MODEL), F32)],
    )
    return pl.pallas_call(
        functools.partial(_moe_ffn_kernel, n_tiles=n_tiles),
        grid_spec=grid_spec,
        out_shape=jax.ShapeDtypeStruct((n_tiles * MOE_TILE, D_MODEL), BF16),
        compiler_params=_cparams(("arbitrary",)),
        name="moe_dispatch_swiglu",
    )(tile_e, tile_lo, tile_n, pos_row, h2, wg, wu, wd)


def _moe_combine_kernel(t0_ref, nt_ref, pos_ref, gates_ref, ys_hbm, x1_ref, mod_ref, g_ref, o_ref,
                        ybuf, sem, acc_ref):
    s = pl.program_id(0)
    per_set = N_EXPERTS * MOE_CMAX

    def copy(blk, e, k):
        start = pl.multiple_of((t0_ref[blk * N_EXPERTS + e] + k) * MOE_CTILE, MOE_CTILE)
        idx = (blk % 2) * per_set + e * MOE_CMAX + k
        return pltpu.make_async_copy(ys_hbm.at[pl.ds(start, MOE_CTILE), :], ybuf.at[idx], sem.at[idx])

    def start_block(blk):
        for e in range(N_EXPERTS):
            for k in range(MOE_CMAX):
                @pl.when(nt_ref[blk * N_EXPERTS + e] > k)
                def _(e=e, k=k):
                    copy(blk, e, k).start()

    @pl.when(s == 0)
    def _():
        start_block(s)

    @pl.when(s + 1 < pl.num_programs(0))
    def _():
        start_block(s + 1)

    acc_ref[...] = jnp.zeros_like(acc_ref)
    lane = lax.broadcasted_iota(jnp.int32, (1, MOE_CTILE), 1)
    for e in range(N_EXPERTS):
        for k in range(MOE_CMAX):
            @pl.when(nt_ref[s * N_EXPERTS + e] > k)
            def _(e=e, k=k):
                copy(s, e, k).wait()
                base = (t0_ref[s * N_EXPERTS + e] + k) * MOE_CTILE
                pick = jnp.where(pos_ref[:, e:e + 1] == base + lane, 1.0, 0.0).astype(BF16)
                acc_ref[...] += gates_ref[:, e:e + 1] * _dot(
                    pick, ybuf[(s % 2) * per_set + e * MOE_CMAX + k])

    o_ref[...] = x1_ref[...] + mod_ref[0, 5:6, :] * _rms(acc_ref[...], g_ref[3:4, :])


def _moe_combine(ys, pos_col, gates, c_t0, c_nt, x1, mod, g, *, rows_per_batch):
    n_tok = x1.shape[0]
    tpb = rows_per_batch // MOE_SRC
    nb = mod.shape[0]
    row_spec = lambda w: pl.BlockSpec((MOE_SRC, w), lambda s, t0, nt: (s, 0))
    grid_spec = pltpu.PrefetchScalarGridSpec(
        num_scalar_prefetch=2,
        grid=(n_tok // MOE_SRC,),
        in_specs=[
            row_spec(LANES), row_spec(LANES),
            pl.BlockSpec(memory_space=pl.ANY),
            row_spec(D_MODEL),
            pl.BlockSpec((1, 6, D_MODEL), lambda s, t0, nt: ((s // tpb) % nb, 0, 0)),
            pl.BlockSpec((4, D_MODEL), lambda s, t0, nt: (0, 0), pipeline_mode=pl.Buffered(1)),
        ],
        out_specs=row_spec(D_MODEL),
        scratch_shapes=[pltpu.VMEM((2 * N_EXPERTS * MOE_CMAX, MOE_CTILE, D_MODEL), BF16),
                        pltpu.SemaphoreType.DMA((2 * N_EXPERTS * MOE_CMAX,)),
                        pltpu.VMEM((MOE_SRC, D_MODEL), F32)],
    )
    return pl.pallas_call(
        _moe_combine_kernel,
        grid_spec=grid_spec,
        out_shape=jax.ShapeDtypeStruct((n_tok, D_MODEL), F32),
        compiler_params=_cparams(("arbitrary",)),
        name="moe_combine_post",
    )(c_t0, c_nt, pos_col, gates, ys, x1, mod, g)


def _l1_in_kernel(x_ref, mod_ref, g_ref, w_ref, gate_ref, xr_ref, xq_ref):
    h = _pre(x_ref[...], g_ref[0:1, :], mod_ref[0, 0:1, :], mod_ref[0, 1:2, :]).astype(BF16)
    z = _dot(h, w_ref[...])
    gate_ref[...] = z[:, 0:D_RNN].astype(BF16)
    xr_ref[...] = z[:, D_RNN:2 * D_RNN]
    xq_ref[...] = z[:, 2 * D_RNN:]


def _l1_in(x, mod, g, w_in, *, tm, rows_per_batch):
    rows = x.shape[0]
    tpb = rows_per_batch // tm
    nb = mod.shape[0]
    row_spec = lambda w: pl.BlockSpec((tm, w), lambda i: (i, 0))
    return pl.pallas_call(
        _l1_in_kernel,
        grid=(rows // tm,),
        in_specs=[row_spec(D_MODEL),
                  pl.BlockSpec((1, 6, D_MODEL), lambda i: ((i // tpb) % nb, 0, 0)),
                  _const_spec((4, D_MODEL)),
                  _const_spec((D_MODEL, w_in.shape[1]))],
        out_specs=[row_spec(D_RNN), row_spec(D_RNN), row_spec(D_POOL)],
        out_shape=[jax.ShapeDtypeStruct((rows, D_RNN), BF16),
                   jax.ShapeDtypeStruct((rows, D_RNN), F32),
                   jax.ShapeDtypeStruct((rows, D_POOL), F32)],
        compiler_params=_cparams(("arbitrary",)),
        name="l1_in_proj",
    )(x, mod, g, w_in)


def _with_halo(x_ref, prev_ref, next_ref, t_idx, n_tiles):
    x = x_ref[...]
    prev = jnp.where(t_idx > 0, prev_ref[:, 0], 0.0)
    nxt = jnp.where(t_idx < n_tiles - 1, next_ref[:, 0], 0.0)
    return jnp.concatenate([prev, x, nxt], axis=1)


def _scan_kernel(*refs, fwd, tt, n_tiles, n_seq):
    if fwd:
        (xr_ref, xrp_ref, xrn_ref, xq_ref, xqp_ref, xqn_ref, cw_ref, cb_ref, wgt_ref, ba_ref, bx_ref,
         lam_ref, h0_ref, pw_ref, ps_ref, hf_ref, yp_ref, hl_ref, a_scr, b_scr, h_scr, carry) = refs
    else:
        (xr_ref, xrp_ref, xrn_ref, cw_ref, cb_ref, wgt_ref, ba_ref, bx_ref, lam_ref, h0_ref,
         hf_ref, gate_ref, yr_ref, hl_ref, a_scr, b_scr, h_scr, carry) = refs
    i = pl.program_id(1)
    t_idx = i if fwd else n_tiles - 1 - i
    nb = SUBLANES

    ext = _with_halo(xr_ref, xrp_ref, xrn_ref, t_idx, n_tiles)
    xc = cb_ref[...][None]
    for j in range(CONV_W):
        lo = HALO + j - CONV_LEFT
        xc = xc + cw_ref[j:j + 1, :][None] * ext[:, lo:lo + tt, :]
    xc = xc.reshape(nb * tt, D_RNN)
    xb = xc.astype(BF16)
    r0 = _dot(xb[:, :RNN_HALF], wgt_ref[0])
    r1 = _dot(xb[:, RNN_HALF:], wgt_ref[1])
    ra = jnp.concatenate([r0[:, :RNN_HALF], r1[:, :RNN_HALF]], axis=1) + ba_ref[...]
    ri = jnp.concatenate([r0[:, RNN_HALF:], r1[:, RNN_HALF:]], axis=1) + bx_ref[...]
    nl = -lam_ref[...]
    softplus = jnp.maximum(nl, 0.0) + jnp.log1p(jnp.exp(-jnp.abs(nl)))
    log_a = -RG_C * _sigmoid(ra) * softplus
    a = jnp.exp(log_a)
    th = jnp.tanh(log_a)
    b = jnp.sqrt(-2.0 * th / (1.0 - th)) * (_sigmoid(ri) * xc)
    n_lc = D_RNN // LANES
    for bi in range(nb):
        for lc in range(n_lc):
            a_scr[lc, pl.ds(bi, tt, stride=nb), :] = a[bi * tt:(bi + 1) * tt, lc * LANES:(lc + 1) * LANES]
            b_scr[lc, pl.ds(bi, tt, stride=nb), :] = b[bi * tt:(bi + 1) * tt, lc * LANES:(lc + 1) * LANES]

    @pl.when(i == 0)
    def _():
        for lc in range(n_lc):
            carry[lc] = h0_ref[:, lc * LANES:(lc + 1) * LANES]

    def step(s, h):
        t = s if fwd else tt - 1 - s
        row = pl.multiple_of(t * nb, nb)
        h = a_scr[:, pl.ds(row, nb), :] * h + b_scr[:, pl.ds(row, nb), :]
        h_scr[:, pl.ds(row, nb), :] = h
        return h

    h_last = lax.fori_loop(0, tt, step, carry[...], unroll=8)
    carry[...] = h_last
    for lc in range(n_lc):
        hl_ref[:, lc * LANES:(lc + 1) * LANES] = h_last[lc]

    def unscan(bi):
        return jnp.concatenate([h_scr[lc, pl.ds(bi, tt, stride=nb), :] for lc in range(n_lc)], axis=1)

    if fwd:
        for bi in range(nb):
            hf_ref[bi] = unscan(bi)
        e = _with_halo(xq_ref, xqp_ref, xqn_ref, t_idx, n_tiles)
        length = tt + 2 * HALO
        a2 = e[:, 0:length - 1] + e[:, 1:length]
        a4 = a2[:, 0:length - 3] + a2[:, 2:length - 1]
        a8 = a4[:, 0:length - 7] + a4[:, 4:length - 3]
        a16 = a8[:, 0:length - 15] + a8[:, 8:length - 7]
        lane = lax.broadcasted_iota(jnp.int32, (1, 1, D_POOL), 2)
        wsum = jnp.where(lane < POOL_C, a2[:, HALO - 1:HALO - 1 + tt],
                         jnp.where(lane < 2 * POOL_C, a4[:, HALO - 2:HALO - 2 + tt],
                                   jnp.where(lane < 3 * POOL_C, a8[:, HALO - 4:HALO - 4 + tt],
                                             a16[:, 0:tt])))
        half = jnp.where(lane < POOL_C, POOL_WINDOWS[0] // 2,
                         jnp.where(lane < 2 * POOL_C, POOL_WINDOWS[1] // 2,
                                   jnp.where(lane < 3 * POOL_C, POOL_WINDOWS[2] // 2,
                                             POOL_WINDOWS[3] // 2)))
        tg = t_idx * tt + lax.broadcasted_iota(jnp.int32, (1, tt, D_POOL), 1)
        cnt = (jnp.minimum(tg + half, n_seq) - jnp.maximum(tg - half, 0)).astype(F32)
        dlt = (wsum / cnt - xq_ref[...]).astype(BF16).reshape(nb * tt, D_POOL)
        yp = _dot(dlt, pw_ref[...]) * ps_ref[...]
        yp_ref[...] = yp.reshape(nb, tt, D_POOL).astype(BF16)
    else:
        for bi in range(nb):
            hb = unscan(bi)
            yr_ref[bi] = ((hf_ref[bi] + hb) * _gelu(gate_ref[bi].astype(F32))).astype(BF16)


def _scan(fwd, xr, xq, conv_w, conv_b, w_gates, ba, bx, lam, h0, pool_w, pool_scale, h_f, gate, *, tt):
    n_batch, n_seq, _ = xr.shape
    n_bg = n_batch // SUBLANES
    n_tiles = n_seq // tt
    hb = tt // HALO
    n_hb = n_seq // HALO

    def t_of(i):
        return i if fwd else n_tiles - 1 - i

    def tile(c):
        return pl.BlockSpec((SUBLANES, tt, c), lambda b, i: (b, t_of(i), 0))

    def prev(c):
        return pl.BlockSpec((SUBLANES, 1, HALO, c),
                            lambda b, i: (b, jnp.maximum(t_of(i) * hb - 1, 0), 0, 0))

    def nxt(c):
        return pl.BlockSpec((SUBLANES, 1, HALO, c),
                            lambda b, i: (b, jnp.minimum((t_of(i) + 1) * hb, n_hb - 1), 0, 0))

    state = pl.BlockSpec((SUBLANES, D_RNN), lambda b, i: (b, 0))
    xr4 = xr.reshape(n_batch, n_hb, HALO, D_RNN)
    common = [_const_spec((CONV_W, D_RNN)), _const_spec((1, D_RNN)),
              _const_spec((2, RNN_HALF, D_RNN)), _const_spec((1, D_RNN)), _const_spec((1, D_RNN)),
              _const_spec((1, D_RNN)), state]
    common_args = [conv_w, conv_b, w_gates, ba, bx, lam, h0]
    n_lc = D_RNN // LANES
    scratch = ([pltpu.VMEM((n_lc, SUBLANES * tt, LANES), F32)] * 3
               + [pltpu.VMEM((n_lc, SUBLANES, LANES), F32)])
    if fwd:
        xq4 = xq.reshape(n_batch, n_hb, HALO, D_POOL)
        in_specs = ([tile(D_RNN), prev(D_RNN), nxt(D_RNN), tile(D_POOL), prev(D_POOL), nxt(D_POOL)]
                    + common + [_const_spec((D_POOL, D_POOL)), _const_spec((1, D_POOL))])
        args = [xr, xr4, xr4, xq, xq4, xq4] + common_args + [pool_w, pool_scale]
        out_specs = [tile(D_RNN), tile(D_POOL), state]
        out_shape = [jax.ShapeDtypeStruct((n_batch, n_seq, D_RNN), F32),
                     jax.ShapeDtypeStruct((n_batch, n_seq, D_POOL), BF16),
                     jax.ShapeDtypeStruct((n_batch, D_RNN), F32)]
    else:
        in_specs = [tile(D_RNN), prev(D_RNN), nxt(D_RNN)] + common + [tile(D_RNN), tile(D_RNN)]
        args = [xr, xr4, xr4] + common_args + [h_f, gate]
        out_specs = [tile(D_RNN), state]
        out_shape = [jax.ShapeDtypeStruct((n_batch, n_seq, D_RNN), BF16),
                     jax.ShapeDtypeStruct((n_batch, D_RNN), F32)]
    return pl.pallas_call(
        functools.partial(_scan_kernel, fwd=fwd, tt=tt, n_tiles=n_tiles, n_seq=n_seq),
        grid=(n_bg, n_tiles),
        in_specs=in_specs,
        out_specs=out_specs,
        out_shape=out_shape,
        scratch_shapes=scratch,
        compiler_params=_cparams(("arbitrary", "arbitrary")),
        name="rglru_forward_pool" if fwd else "rglru_backward_combine",
    )(*args)


def _gate_weights(wa, wx):
    per_half = RNN_BLOCKS // 2

    def half_dense(w, k):
        blocks = [w[k * per_half + b] for b in range(per_half)]
        rows = []
        for bi, blk in enumerate(blocks):
            row = [blk if bj == bi else jnp.zeros_like(blk) for bj in range(per_half)]
            rows.append(jnp.concatenate(row, axis=1))
        return jnp.concatenate(rows, axis=0)

    return jnp.stack([jnp.concatenate([half_dense(wa, k), half_dense(wx, k)], axis=1)
                      for k in range(2)]).astype(BF16)


def _pool_weights(pool_w):
    n = pool_w.shape[0]
    rows = []
    for i in range(n):
        rows.append(jnp.concatenate([pool_w[i] if j == i else jnp.zeros_like(pool_w[i])
                                     for j in range(n)], axis=1))
    return jnp.concatenate(rows, axis=0).astype(BF16)


def kernel(x_prompt, x_sample, cache_k, cache_v, state_h, c, c_ctx, w_ada, b_ada, norm_g, w_in_even,
           w_out_even, na_rpb, sgu_w, sgu_b, w_in_odd, w_out_odd, conv_w, conv_b, rg_wa, rg_ba, rg_wx,
           rg_bx, rg_lam, pool_w, pool_scale, ffn_wg, ffn_wu, ffn_wd, router_w, moe_wg, moe_wu, moe_wd):
    bp, n_p, _ = x_prompt.shape
    bs, n_s, _ = x_sample.shape
    n_ctx = cache_k.shape[3]

    cond = jnp.concatenate([c_ctx[None, :], c, jnp.zeros((2 * SUBLANES - 1 - bs, D_MODEL), F32)], axis=0)
    mods = _modulation(cond, w_ada, b_ada).reshape(w_ada.shape[0], cond.shape[0], 6, D_MODEL)

    streams = [
        dict(x=x_prompt.reshape(bp * n_p, D_MODEL), nb=bp, n=n_p, tm=256, latent=False),
        dict(x=x_sample.reshape(bs * n_s, D_MODEL), nb=bs, n=n_s, tm=512, latent=True),
    ]
    new_k = new_v = new_h = None

    g = norm_g[0]
    w_in = w_in_even[0].astype(BF16)
    w_out = w_out_even[0].astype(BF16)
    w_s = sgu_w[0].astype(BF16)
    b_s_t = sgu_b[0].T
    wg, wu, wd = (w[0:1].astype(BF16) for w in (ffn_wg, ffn_wu, ffn_wd))
    bias = _latent_bias_table(na_rpb[0], n_s // GRID_W)
    kc = cache_k[:, 0].transpose(0, 2, 1, 3).reshape(bs * n_ctx, W_A).astype(BF16)
    vc = cache_v[:, 0].transpose(0, 2, 1, 3).reshape(bs * n_ctx, W_A).astype(BF16)
    for st in streams:
        mod = mods[0, 1:1 + bs] if st["latent"] else mods[0, 0:1]
        outs = _l0_in(st["x"], mod, g, w_in, w_s, b_s_t, tm=st["tm"], rows_per_batch=st["n"],
                      emit_f32=not st["latent"])
        q, k, v, sg = outs[:4]
        if st["latent"]:
            oa = _attn_latent(q, k, v, kc, vc, bias, n_batch=st["nb"], n_seq=st["n"], n_ctx=n_ctx)
        else:
            oa = _attn_ctx(q, k, v, n_batch=st["nb"], n_seq=st["n"])
            heads = lambda a: a.reshape(bp, n_p, N_HEADS, HEAD_DIM).transpose(0, 2, 1, 3)[:, None]
            new_k, new_v = heads(outs[4]), heads(outs[5])
        x1, h2 = _out_proj([oa, sg], st["x"], mod, g, w_out, None, tm=st["tm"], rows_per_batch=st["n"])
        st["x"] = _ffn(h2, x1, mod, g, None, wg, wu, wd, tm=st["tm"], rows_per_batch=st["n"])

    g = norm_g[1]
    w_in = w_in_odd[0].astype(BF16)
    w_out = w_out_odd[0].astype(BF16)
    w_gates = [_gate_weights(rg_wa[0, d], rg_wx[0, d]) for d in range(2)]
    pw = _pool_weights(pool_w[0])
    ps = pool_scale[0][None, :]
    rw = jnp.pad(router_w[0].T, ((0, LANES - N_EXPERTS), (0, 0)))
    wg, wu, wd = (w[0].astype(BF16) for w in (moe_wg, moe_wu, moe_wd))
    for st in streams:
        mod = mods[1, 1:1 + bs] if st["latent"] else mods[1, 0:1]
        nb, n = st["nb"], st["n"]
        gate, xr, xq = _l1_in(st["x"], mod, g, w_in, tm=st["tm"], rows_per_batch=n)
        gate, xr, xq = (a.reshape(nb, n, a.shape[1]) for a in (gate, xr, xq))
        h0 = state_h[:, 0].astype(F32) if st["latent"] else jnp.zeros((nb, 2, D_RNN), F32)
        scan_args = lambda d: (conv_w[0], conv_b[0][None, :], w_gates[d], rg_ba[0, d][None, :],
                               rg_bx[0, d][None, :], rg_lam[0, d][None, :], h0[:, d])
        h_f, y_pool, h_f_last = _scan(True, xr, xq, *scan_args(0), pw, ps, None, None, tt=128)
        y_rec, h_b_last = _scan(False, xr, None, *scan_args(1), None, None, h_f, gate, tt=128)
        if not st["latent"]:
            new_h = jnp.stack([h_f_last, h_b_last], axis=1)[:, None]
        x1, h2, gates, rank_col, rank_row = _out_proj(
            [y_rec.reshape(nb * n, D_RNN), y_pool.reshape(nb * n, D_POOL)],
            st["x"], mod, g, w_out, rw, tm=MOE_SRC, rows_per_batch=n)
        pos_row, pos_col, tile_e, tile_lo, tile_n, c_t0, c_nt, n_tiles = _moe_plan(rank_row, rank_col)
        ys = _moe_ffn(h2, pos_row, tile_e, tile_lo, tile_n, wg, wu, wd, n_tiles=n_tiles)
        st["x"] = _moe_combine(ys, pos_col, gates, c_t0, c_nt, x1, mod, g, rows_per_batch=n)

    y_prompt = streams[0]["x"].reshape(bp, n_p, D_MODEL)
    y_sample = streams[1]["x"].reshape(bs, n_s, D_MODEL)
    return (y_prompt, y_sample, new_k, new_v, new_h)
```

```python
import functools

import jax
import jax.numpy as jnp
import numpy as np
from jax import lax
from jax.experimental import pallas as pl
from jax.experimental.pallas import tpu as pltpu

F32 = jnp.float32
BF16 = jnp.bfloat16

D_MODEL = 1024
GRID_W = 64
HEAD_DIM = 64
N_HEADS = 8
W_A = N_HEADS * HEAD_DIM
WIN_R = 8
WIN_C = 16
N_GROUPS_B = 4
W_B = 512
CHUNK = 128
D_RNN = 768
RNN_BLOCKS = 8
RNN_BW = D_RNN // RNN_BLOCKS
RNN_HALF = D_RNN // 2
CONV_W = 4
CONV_LEFT = 2
RG_C = 8.0
D_POOL = 256
POOL_WINDOWS = (2, 4, 8, 16)
POOL_C = D_POOL // len(POOL_WINDOWS)
D_FF = 2816
N_EXPERTS = 8
EPS = 1e-6
NEG = -1e30

LANES = 128
SUBLANES = 8
VMEM_LIMIT = 56 * 1024 * 1024
Q_ROWS = 4
KEY_ROWS = Q_ROWS + WIN_R - 1
HALO = 8
FF_CHUNKS = 2
FF_CHUNK = D_FF // FF_CHUNKS
MOE_SRC = 256
MOE_TILE = 256
MOE_RING = 8
MOE_CTILE = 128
MOE_CMAX = (MOE_SRC - 1 + MOE_CTILE - 1) // MOE_CTILE + 1


def _cparams(sem):
    return pltpu.CompilerParams(dimension_semantics=sem, vmem_limit_bytes=VMEM_LIMIT)


def _const_spec(shape):
    nd = len(shape)
    return pl.BlockSpec(shape, lambda *_: (0,) * nd, pipeline_mode=pl.Buffered(1))


def _dot(a, b):
    return jnp.dot(a, b, preferred_element_type=F32)


def _dot_nt(a, b):
    return lax.dot_general(a, b, (((1,), (1,)), ((), ())), preferred_element_type=F32)


def _rms(x, g):
    ms = jnp.mean(x * x, axis=-1, keepdims=True)
    return x * lax.rsqrt(ms + EPS) * g


def _pre(x, g, shift, scale):
    return _rms(x, g) * (1.0 + scale) + shift


def _gelu(x):
    return jax.nn.gelu(x, approximate=True)


def _sigmoid(x):
    return 1.0 / (1.0 + jnp.exp(-x))


def _silu(x):
    return x * _sigmoid(x)


def _mod_kernel(c_ref, w_ref, b_ref, o_ref):
    s = _silu(c_ref[...])
    o_ref[0] = jnp.dot(s, w_ref[0], preferred_element_type=F32,
                       precision=lax.Precision.HIGHEST) + b_ref[0]


def _modulation(cond, w_ada, b_ada):
    depth, d, n = w_ada.shape
    rows = cond.shape[0]
    bn = 768
    return pl.pallas_call(
        _mod_kernel,
        grid=(depth, n // bn),
        in_specs=[
            pl.BlockSpec((rows, d), lambda i, j: (0, 0)),
            pl.BlockSpec((1, d, bn), lambda i, j: (i, 0, j)),
            pl.BlockSpec((1, 1, bn), lambda i, j: (i, 0, j)),
        ],
        out_specs=pl.BlockSpec((1, rows, bn), lambda i, j: (i, 0, j)),
        out_shape=jax.ShapeDtypeStruct((depth, rows, n), F32),
        compiler_params=_cparams(("arbitrary", "arbitrary")),
        name="adaln_modulation",
    )(cond, w_ada, b_ada.reshape(depth, 1, n))


def _l0_in_kernel(x_ref, mod_ref, g_ref, w_ref, ws_ref, bs_ref, *out_refs, tm, emit_f32):
    q_ref, k_ref, v_ref, sg_ref = out_refs[:4]
    x = x_ref[...]
    h = _pre(x, g_ref[0:1, :], mod_ref[0, 0:1, :], mod_ref[0, 1:2, :]).astype(BF16)
    z = _dot(h, w_ref[...])
    q_ref[...] = z[:, 0:W_A].astype(BF16)
    k_ref[...] = z[:, W_A:2 * W_A].astype(BF16)
    v_ref[...] = z[:, 2 * W_A:3 * W_A].astype(BF16)
    if emit_f32:
        out_refs[4][...] = z[:, W_A:2 * W_A]
        out_refs[5][...] = z[:, 2 * W_A:3 * W_A]
    u = _gelu(z[:, 3 * W_A:3 * W_A + W_B])
    gf = _gelu(z[:, 3 * W_A + W_B:])
    n_chunks = tm // CHUNK
    for gi in range(N_GROUPS_B):
        gg = gf[:, gi * LANES:(gi + 1) * LANES]
        mu = jnp.mean(gg, axis=-1, keepdims=True)
        dd = gg - mu
        var = jnp.mean(dd * dd, axis=-1, keepdims=True)
        gn = (dd * lax.rsqrt(var + EPS)).astype(BF16)
        rhs = jnp.concatenate([gn[c * CHUNK:(c + 1) * CHUNK, :] for c in range(n_chunks)], axis=1)
        s = _dot(ws_ref[gi], rhs) + bs_ref[:, gi:gi + 1]
        for c in range(n_chunks):
            uu = u[c * CHUNK:(c + 1) * CHUNK, gi * LANES:(gi + 1) * LANES]
            sg_ref[c * CHUNK:(c + 1) * CHUNK, gi * LANES:(gi + 1) * LANES] = (
                uu * s[:, c * LANES:(c + 1) * LANES]).astype(BF16)


def _l0_in(x, mod, g, w_in, w_s, b_s_t, *, tm, rows_per_batch, emit_f32):
    rows = x.shape[0]
    tpb = rows_per_batch // tm
    nb = mod.shape[0]
    d_in = w_in.shape[1]
    row_spec = lambda w: pl.BlockSpec((tm, w), lambda i: (i, 0))
    out_shape = [jax.ShapeDtypeStruct((rows, W_A), BF16)] * 3 + [jax.ShapeDtypeStruct((rows, W_B), BF16)]
    out_specs = [row_spec(W_A)] * 3 + [row_spec(W_B)]
    if emit_f32:
        out_shape += [jax.ShapeDtypeStruct((rows, W_A), F32)] * 2
        out_specs += [row_spec(W_A)] * 2
    return pl.pallas_call(
        functools.partial(_l0_in_kernel, tm=tm, emit_f32=emit_f32),
        grid=(rows // tm,),
        in_specs=[
            row_spec(D_MODEL),
            pl.BlockSpec((1, 6, D_MODEL), lambda i: ((i // tpb) % nb, 0, 0)),
            _const_spec((4, D_MODEL)),
            _const_spec((D_MODEL, d_in)),
            _const_spec((N_GROUPS_B, CHUNK, CHUNK)),
            _const_spec((CHUNK, N_GROUPS_B)),
        ],
        out_specs=out_specs,
        out_shape=out_shape,
        compiler_params=_cparams(("arbitrary",)),
        name="l0_in_proj_sgu",
    )(x, mod, g, w_in, w_s, b_s_t)


def _attn_kernel(*refs, n_local, has_ctx, rows):
    if has_ctx:
        q_ref, k_ref, v_ref, kc_ref, vc_ref, bias_ref, o_ref = refs
        r0 = pl.program_id(1) * Q_ROWS
        ks = jnp.clip(r0 - WIN_R // 2, 0, rows - KEY_ROWS)
        start = pl.multiple_of(ks * GRID_W, GRID_W)
        local = pl.ds(start, n_local)
    else:
        q_ref, k_ref, v_ref, o_ref = refs
        local = slice(None)
    first = lax.broadcasted_iota(jnp.int32, (1, LANES), 1) < HEAD_DIM
    for hp in range(W_A // LANES):
        lanes = slice(hp * LANES, (hp + 1) * LANES)
        kl = k_ref[local, lanes]
        vl = v_ref[local, lanes]
        q = q_ref[:, lanes] * jnp.asarray(HEAD_DIM ** -0.5, BF16)
        outs = []
        for half in range(2):
            qh = jnp.where(first if half == 0 else jnp.logical_not(first), q, jnp.zeros_like(q))
            s = _dot_nt(qh, kl)
            if has_ctx:
                s = s + bias_ref[0, 2 * hp + half]
                sc = _dot_nt(qh, kc_ref[:, lanes])
                m = jnp.maximum(jnp.max(s, axis=-1, keepdims=True), jnp.max(sc, axis=-1, keepdims=True))
            else:
                m = jnp.max(s, axis=-1, keepdims=True)
            p = jnp.exp(s - m)
            l = jnp.sum(p, axis=-1, keepdims=True)
            acc = _dot(p.astype(BF16), vl)
            if has_ctx:
                pc = jnp.exp(sc - m)
                l = l + jnp.sum(pc, axis=-1, keepdims=True)
                acc = acc + _dot(pc.astype(BF16), vc_ref[:, lanes])
            outs.append(acc / l)
        o_ref[:, lanes] = jnp.where(first, outs[0], outs[1]).astype(BF16)


def _attn_ctx(q, k, v, *, n_batch, n_seq):
    rows = q.shape[0]
    spec = pl.BlockSpec((n_seq, W_A), lambda b: (b, 0))
    return pl.pallas_call(
        functools.partial(_attn_kernel, n_local=n_seq, has_ctx=False, rows=0),
        grid=(n_batch,),
        in_specs=[spec, spec, spec],
        out_specs=spec,
        out_shape=jax.ShapeDtypeStruct((rows, W_A), BF16),
        compiler_params=_cparams(("arbitrary",)),
        name="attn_context",
    )(q, k, v)


def _attn_latent(q, k, v, kc, vc, bias, *, n_batch, n_seq, n_ctx):
    rows = n_seq // GRID_W
    n_rg = rows // Q_ROWS
    tq = Q_ROWS * GRID_W
    n_local = KEY_ROWS * GRID_W
    img_spec = pl.BlockSpec((n_seq, W_A), lambda b, rg: (b, 0))
    ctx_spec = pl.BlockSpec((n_ctx, W_A), lambda b, rg: (b, 0))
    q_spec = pl.BlockSpec((tq, W_A), lambda b, rg: (b * n_rg + rg, 0))

    def bias_map(b, rg):
        cfg = jnp.where(rg == 0, 0, jnp.where(rg == n_rg - 1, 2, 1))
        return (cfg, 0, 0, 0)

    return pl.pallas_call(
        functools.partial(_attn_kernel, n_local=n_local, has_ctx=True, rows=rows),
        grid=(n_batch, n_rg),
        in_specs=[q_spec, img_spec, img_spec, ctx_spec, ctx_spec,
                  pl.BlockSpec((1, N_HEADS, tq, n_local), bias_map)],
        out_specs=q_spec,
        out_shape=jax.ShapeDtypeStruct((n_batch * n_seq, W_A), BF16),
        compiler_params=_cparams(("arbitrary", "arbitrary")),
        name="attn_latent",
    )(q, k, v, kc, vc, bias)


def _latent_bias_table(rpb, rows):
    n_heads = rpb.shape[0]
    qcol = np.arange(GRID_W)[:, None]
    kcol = np.arange(GRID_W)[None, :]
    cs = np.clip(qcol - WIN_C // 2, 0, GRID_W - WIN_C)
    col_valid = (kcol >= cs) & (kcol < cs + WIN_C)
    dc = np.clip(kcol - qcol + (WIN_C - 1), 0, 2 * WIN_C - 2)
    pick = (dc[:, :, None] == np.arange(2 * WIN_C - 1)).astype(np.float32)
    col_tab = jnp.einsum("hrc,qkc->hrqk", rpb.astype(F32), pick, precision=lax.Precision.HIGHEST)
    col_tab = jnp.where(col_valid[None, None], col_tab, NEG)
    tables = []
    for r0 in (0, 2 * Q_ROWS, rows - Q_ROWS):
        ks = min(max(r0 - WIN_R // 2, 0), rows - KEY_ROWS)
        qrow = r0 + np.arange(Q_ROWS)[:, None]
        krow = ks + np.arange(KEY_ROWS)[None, :]
        rs = np.clip(qrow - WIN_R // 2, 0, rows - WIN_R)
        row_valid = (krow >= rs) & (krow < rs + WIN_R)
        dr = np.clip(krow - qrow + (WIN_R - 1), 0, 2 * WIN_R - 2)
        blocks = [jnp.stack([col_tab[:, dr[i, k]] if row_valid[i, k]
                             else jnp.full((n_heads, GRID_W, GRID_W), NEG, F32)
                             for k in range(KEY_ROWS)], axis=2)
                  for i in range(Q_ROWS)]
        b = jnp.stack(blocks, axis=1)
        tables.append(b.reshape(n_heads, Q_ROWS * GRID_W, KEY_ROWS * GRID_W))
    return jnp.stack(tables)


def _out_kernel(*refs, n_in, router):
    in_refs = refs[:n_in]
    x_ref, mod_ref, g_ref, w_ref = refs[n_in:n_in + 4]
    rest = refs[n_in + 4:]
    if router:
        rw_ref, x1_ref, h2_ref, gates_ref, rankc_ref, rankr_ref = rest
    else:
        x1_ref, h2_ref = rest
    y = None
    off = 0
    for r in in_refs:
        w = r.shape[1]
        part = _dot(r[...], w_ref[off:off + w, :])
        y = part if y is None else y + part
        off += w
    x1 = x_ref[...] + mod_ref[0, 2:3, :] * _rms(y, g_ref[1:2, :])
    x1_ref[...] = x1
    h2 = _pre(x1, g_ref[2:3, :], mod_ref[0, 3:4, :], mod_ref[0, 4:5, :])
    h2_ref[...] = h2.astype(BF16)
    for sb in range(h2.shape[0] // MOE_SRC if router else 0):
        blk = slice(sb * MOE_SRC, (sb + 1) * MOE_SRC)
        hs = h2[blk, :]
        h_hi = hs.astype(BF16)
        h_lo = (hs - h_hi.astype(F32)).astype(BF16)
        rw = rw_ref[...]
        r_hi = rw.astype(BF16)
        r_lo = (rw - r_hi.astype(F32)).astype(BF16)
        logits = _dot_nt(r_hi, h_hi) + (_dot_nt(r_hi, h_lo) + _dot_nt(r_lo, h_hi))
        tm = MOE_SRC
        row = lax.broadcasted_iota(jnp.int32, logits.shape, 0).astype(F32)
        logits = jnp.where(row < N_EXPERTS, logits, -jnp.inf)
        m1 = jnp.max(logits, axis=0, keepdims=True)
        i1 = jnp.min(jnp.where(logits == m1, row, float(LANES)), axis=0, keepdims=True)
        rest_l = jnp.where(row == i1, -jnp.inf, logits)
        m2 = jnp.max(rest_l, axis=0, keepdims=True)
        i2 = jnp.min(jnp.where(rest_l == m2, row, float(LANES)), axis=0, keepdims=True)
        e2 = jnp.exp(m2 - m1)
        w1 = 1.0 / (1.0 + e2)
        w2 = e2 / (1.0 + e2)
        sel1 = row == i1
        sel2 = row == i2
        member = jnp.where(sel1, 1.0, 0.0) + jnp.where(sel2, 1.0, 0.0)
        gates = jnp.where(sel1, w1, 0.0) + jnp.where(sel2, w2, 0.0)
        before = (lax.broadcasted_iota(jnp.int32, (tm, tm), 0)
                  < lax.broadcasted_iota(jnp.int32, (tm, tm), 1))
        rank = _dot(member.astype(BF16), jnp.where(before, 1.0, 0.0).astype(BF16))
        rank = jnp.where(member > 0.0, rank, -1.0)
        gates_ref[blk, :] = gates.T
        rankc_ref[blk, :] = rank.T.astype(jnp.int32)
        rankr_ref[:, blk] = rank[0:N_EXPERTS, :].astype(jnp.int32)


def _out_proj(ins, x, mod, g, w_out, router_w, *, tm, rows_per_batch):
    rows = x.shape[0]
    tpb = rows_per_batch // tm
    nb = mod.shape[0]
    router = router_w is not None
    row_spec = lambda w: pl.BlockSpec((tm, w), lambda i: (i, 0))
    in_specs = [row_spec(a.shape[1]) for a in ins] + [
        row_spec(D_MODEL),
        pl.BlockSpec((1, 6, D_MODEL), lambda i: ((i // tpb) % nb, 0, 0)),
        _const_spec((4, D_MODEL)),
        _const_spec((D_MODEL, D_MODEL)),
    ]
    args = list(ins) + [x, mod, g, w_out]
    out_shape = [jax.ShapeDtypeStruct((rows, D_MODEL), F32), jax.ShapeDtypeStruct((rows, D_MODEL), BF16)]
    out_specs = [row_spec(D_MODEL), row_spec(D_MODEL)]
    if router:
        in_specs.append(_const_spec((LANES, D_MODEL)))
        args.append(router_w)
        out_shape += [jax.ShapeDtypeStruct((rows, LANES), F32),
                      jax.ShapeDtypeStruct((rows, LANES), jnp.int32),
                      jax.ShapeDtypeStruct((N_EXPERTS, rows), jnp.int32)]
        out_specs += [row_spec(LANES), row_spec(LANES),
                      pl.BlockSpec((N_EXPERTS, tm), lambda i: (0, i))]
    return pl.pallas_call(
        functools.partial(_out_kernel, n_in=len(ins), router=router),
        grid=(rows // tm,),
        in_specs=in_specs,
        out_specs=out_specs,
        out_shape=out_shape,
        compiler_params=_cparams(("arbitrary",)),
        name="out_proj_norms_router" if router else "out_proj_norms",
    )(*args)


def _ffn_kernel(*refs, gated, n_e):
    if gated:
        h_ref, x1_ref, mod_ref, g_ref, gates_ref, wg_ref, wu_ref, wd_ref, o_ref, acc_ref = refs
    else:
        h_ref, x1_ref, mod_ref, g_ref, wg_ref, wu_ref, wd_ref, o_ref, acc_ref = refs
    e = pl.program_id(1)
    fc = pl.program_id(2)

    @pl.when((e == 0) & (fc == 0))
    def _():
        acc_ref[...] = jnp.zeros_like(acc_ref)

    h = h_ref[...]
    t = _silu(_dot(h, wg_ref[0])) * _dot(h, wu_ref[0])
    y = _dot(t.astype(BF16), wd_ref[0])
    if gated:
        gates = gates_ref[...]
        lane = lax.broadcasted_iota(jnp.int32, gates.shape, 1)
        y = y * jnp.sum(jnp.where(lane == e, gates, 0.0), axis=-1, keepdims=True)
    acc_ref[...] += y

    @pl.when((e == n_e - 1) & (fc == FF_CHUNKS - 1))
    def _():
        o_ref[...] = x1_ref[...] + mod_ref[0, 5:6, :] * _rms(acc_ref[...], g_ref[3:4, :])


def _ffn(h2, x1, mod, g, gates, wg, wu, wd, *, tm, rows_per_batch):
    rows = h2.shape[0]
    tpb = rows_per_batch // tm
    nb = mod.shape[0]
    n_e = wg.shape[0]
    gated = gates is not None
    row_spec = lambda w: pl.BlockSpec((tm, w), lambda i, e, f: (i, 0))
    in_specs = [row_spec(D_MODEL), row_spec(D_MODEL),
                pl.BlockSpec((1, 6, D_MODEL), lambda i, e, f: ((i // tpb) % nb, 0, 0)),
                _const_spec((4, D_MODEL))]
    args = [h2, x1, mod, g]
    if gated:
        in_specs.append(row_spec(LANES))
        args.append(gates)
    in_specs += [
        pl.BlockSpec((1, D_MODEL, FF_CHUNK), lambda i, e, f: (e, 0, f)),
        pl.BlockSpec((1, D_MODEL, FF_CHUNK), lambda i, e, f: (e, 0, f)),
        pl.BlockSpec((1, FF_CHUNK, D_MODEL), lambda i, e, f: (e, f, 0)),
    ]
    args += [wg, wu, wd]
    return pl.pallas_call(
        functools.partial(_ffn_kernel, gated=gated, n_e=n_e),
        grid=(rows // tm, n_e, FF_CHUNKS),
        in_specs=in_specs,
        out_specs=row_spec(D_MODEL),
        out_shape=jax.ShapeDtypeStruct((rows, D_MODEL), F32),
        scratch_shapes=[pltpu.VMEM((tm, D_MODEL), F32)],
        compiler_params=_cparams(("arbitrary", "arbitrary", "arbitrary")),
        name="moe_swiglu_post" if gated else "swiglu_post",
    )(*args)


def _moe_plan(rank_row, rank_col):
    n_tok = rank_row.shape[1]
    n_src = n_tok // MOE_SRC
    n_tiles = 2 * n_tok // MOE_TILE + N_EXPERTS
    member = rank_row >= 0
    cnt_blk = member.reshape(N_EXPERTS, n_src, MOE_SRC).sum(-1).astype(jnp.int32)
    cum = jnp.concatenate([jnp.zeros((N_EXPERTS, 1), jnp.int32), jnp.cumsum(cnt_blk, axis=1)], axis=1)
    cnt = cum[:, -1]
    tiles_e = (cnt + MOE_TILE - 1) // MOE_TILE
    tile_end = jnp.cumsum(tiles_e)
    tile_base = tile_end - tiles_e
    slot0 = tile_base[:, None] * MOE_TILE + cum[:, :-1]
    pos_row = jnp.where(member, jnp.repeat(slot0, MOE_SRC, axis=1) + rank_row, -1)
    pos_row = jnp.pad(pos_row, ((0, 0), (0, MOE_RING * MOE_SRC)), constant_values=-1)
    slot0_col = jnp.pad(slot0.T, ((0, 0), (0, LANES - N_EXPERTS)))
    pos_col = jnp.where(rank_col >= 0, jnp.repeat(slot0_col, MOE_SRC, axis=0) + rank_col, -1)
    d = jnp.arange(n_tiles, dtype=jnp.int32)
    tile_e = jnp.minimum(jnp.sum(d[:, None] >= tile_end[None, :], axis=1), N_EXPERTS - 1).astype(jnp.int32)
    valid = d < tile_end[-1]
    lo_slot = (d - tile_base[tile_e]) * MOE_TILE
    hi_slot = jnp.minimum(lo_slot + MOE_TILE, cnt[tile_e])
    cum_d = cum[tile_e]
    tile_lo = jnp.sum(cum_d[:, 1:] <= lo_slot[:, None], axis=1).astype(jnp.int32)
    tile_hi = jnp.sum(cum_d[:, :-1] < hi_slot[:, None], axis=1).astype(jnp.int32) - 1
    tile_n = jnp.where(valid, tile_hi - tile_lo + 1, 0).astype(jnp.int32)
    tile_lo = jnp.where(valid, tile_lo, 0)
    first = slot0.T
    c_t0 = first // MOE_CTILE
    c_nt = jnp.where(cnt_blk.T > 0, (first + cnt_blk.T - 1) // MOE_CTILE - c_t0 + 1, 0)
    return (pos_row, pos_col, tile_e, tile_lo, tile_n,
            c_t0.reshape(-1).astype(jnp.int32), c_nt.reshape(-1).astype(jnp.int32), n_tiles)


def _moe_ffn_kernel(te_ref, tl_ref, tn_ref, pos_ref, h_hbm, wg_ref, wu_ref, wd_ref, o_ref,
                    hbuf, sem, acc_ref, *, n_tiles):
    d = pl.program_id(0)
    e = te_ref[d]
    lo = tl_ref[d]
    n = tn_ref[d]

    def copy(blk, slot):
        start = pl.multiple_of(blk * MOE_SRC, MOE_SRC)
        return pltpu.make_async_copy(h_hbm.at[pl.ds(start, MOE_SRC), :], hbuf.at[slot], sem.at[slot])

    def start_first(first_blk, count):
        for k in range(MOE_RING):
            @pl.when(k < count)
            def _(k=k):
                copy(first_blk + k, k).start()

    @pl.when(d == 0)
    def _():
        hbuf[...] = jnp.zeros_like(hbuf)
        start_first(lo, n)

    slot_ids = d * MOE_TILE + lax.broadcasted_iota(jnp.int32, (MOE_TILE, 1), 0)

    def picks(first_blk, n_blk):
        start = pl.multiple_of(first_blk * MOE_SRC, MOE_SRC)
        pos = pos_ref[pl.ds(e, 1), pl.ds(start, n_blk * MOE_SRC)]
        return jnp.where(pos == slot_ids, 1.0, 0.0).astype(BF16)

    @pl.when(n > 0)
    def _():
        for k in range(MOE_RING):
            @pl.when(k < n)
            def _(k=k):
                copy(lo + k, k).wait()
        acc_ref[...] = _dot(picks(lo, MOE_RING), hbuf[...].reshape(MOE_RING * MOE_SRC, D_MODEL))

    def extra(j, carry):
        cp = copy(lo + j, 0)
        cp.start()
        cp.wait()
        acc_ref[...] += _dot(picks(lo + j, 1), hbuf[0])
        return carry

    lax.fori_loop(MOE_RING, n, extra, 0)

    @pl.when(d + 1 < n_tiles)
    def _():
        start_first(tl_ref[d + 1], tn_ref[d + 1])

    @pl.when(n > 0)
    def _():
        x = acc_ref[...].astype(BF16)
        y = None
        for fc in range(FF_CHUNKS):
            cols = slice(fc * FF_CHUNK, (fc + 1) * FF_CHUNK)
            t = _silu(_dot(x, wg_ref[0, :, cols])) * _dot(x, wu_ref[0, :, cols])
            part = _dot(t.astype(BF16), wd_ref[0, cols, :])
            y = part if y is None else y + part
        o_ref[...] = y.astype(BF16)

    @pl.when(n == 0)
    def _():
        o_ref[...] = jnp.zeros_like(o_ref)


def _moe_ffn(h2, pos_row, tile_e, tile_lo, tile_n, wg, wu, wd, *, n_tiles):
    n_tok = h2.shape[0]
    w_spec = lambda shape: pl.BlockSpec((1,) + shape, lambda d, te, tl, tn: (te[d], 0, 0))
    grid_spec = pltpu.PrefetchScalarGridSpec(
        num_scalar_prefetch=3,
        grid=(n_tiles,),
        in_specs=[
            pl.BlockSpec(pos_row.shape, lambda d, te, tl, tn: (0, 0), pipeline_mode=pl.Buffered(1)),
            pl.BlockSpec(memory_space=pl.ANY),
            w_spec((D_MODEL, D_FF)), w_spec((D_MODEL, D_FF)), w_spec((D_FF, D_MODEL)),
        ],
        out_specs=pl.BlockSpec((MOE_TILE, D_MODEL), lambda d, te, tl, tn: (d, 0)),
        scratch_shapes=[pltpu.VMEM((MOE_RING, MOE_SRC, D_MODEL), BF16),
                        pltpu.SemaphoreType.DMA((MOE_RING,)),
                        pltpu.VMEM((MOE_TILE, D_MODEL), F32)],
    )
    return pl.pallas_call(
        functools.partial(_moe_ffn_kernel, n_tiles=n_tiles),
        grid_spec=grid_spec,
        out_shape=jax.ShapeDtypeStruct((n_tiles * MOE_TILE, D_MODEL), BF16),
        compiler_params=_cparams(("arbitrary",)),
        name="moe_dispatch_swiglu",
    )(tile_e, tile_lo, tile_n, pos_row, h2, wg, wu, wd)


def _moe_combine_kernel(t0_ref, nt_ref, pos_ref, gates_ref, ys_hbm, x1_ref, mod_ref, g_ref, o_ref,
                        ybuf, sem, acc_ref):
    s = pl.program_id(0)
    per_set = N_EXPERTS * MOE_CMAX

    def copy(blk, e, k):
        start = pl.multiple_of((t0_ref[blk * N_EXPERTS + e] + k) * MOE_CTILE, MOE_CTILE)
        idx = (blk % 2) * per_set + e * MOE_CMAX + k
        return pltpu.make_async_copy(ys_hbm.at[pl.ds(start, MOE_CTILE), :], ybuf.at[idx], sem.at[idx])

    def start_block(blk):
        for e in range(N_EXPERTS):
            for k in range(MOE_CMAX):
                @pl.when(nt_ref[blk * N_EXPERTS + e] > k)
                def _(e=e, k=k):
                    copy(blk, e, k).start()

    @pl.when(s == 0)
    def _():
        start_block(s)

    @pl.when(s + 1 < pl.num_programs(0))
    def _():
        start_block(s + 1)

    acc_ref[...] = jnp.zeros_like(acc_ref)
    lane = lax.broadcasted_iota(jnp.int32, (1, MOE_CTILE), 1)
    for e in range(N_EXPERTS):
        for k in range(MOE_CMAX):
            @pl.when(nt_ref[s * N_EXPERTS + e] > k)
            def _(e=e, k=k):
                copy(s, e, k).wait()
                base = (t0_ref[s * N_EXPERTS + e] + k) * MOE_CTILE
                pick = jnp.where(pos_ref[:, e:e + 1] == base + lane, 1.0, 0.0).astype(BF16)
                acc_ref[...] += gates_ref[:, e:e + 1] * _dot(
                    pick, ybuf[(s % 2) * per_set + e * MOE_CMAX + k])

    o_ref[...] = x1_ref[...] + mod_ref[0, 5:6, :] * _rms(acc_ref[...], g_ref[3:4, :])


def _moe_combine(ys, pos_col, gates, c_t0, c_nt, x1, mod, g, *, rows_per_batch):
    n_tok = x1.shape[0]
    tpb = rows_per_batch // MOE_SRC
    nb = mod.shape[0]
    row_spec = lambda w: pl.BlockSpec((MOE_SRC, w), lambda s, t0, nt: (s, 0))
    grid_spec = pltpu.PrefetchScalarGridSpec(
        num_scalar_prefetch=2,
        grid=(n_tok // MOE_SRC,),
        in_specs=[
            row_spec(LANES), row_spec(LANES),
            pl.BlockSpec(memory_space=pl.ANY),
            row_spec(D_MODEL),
            pl.BlockSpec((1, 6, D_MODEL), lambda s, t0, nt: ((s // tpb) % nb, 0, 0)),
            pl.BlockSpec((4, D_MODEL), lambda s, t0, nt: (0, 0), pipeline_mode=pl.Buffered(1)),
        ],
        out_specs=row_spec(D_MODEL),
        scratch_shapes=[pltpu.VMEM((2 * N_EXPERTS * MOE_CMAX, MOE_CTILE, D_MODEL), BF16),
                        pltpu.SemaphoreType.DMA((2 * N_EXPERTS * MOE_CMAX,)),
                        pltpu.VMEM((MOE_SRC, D_MODEL), F32)],
    )
    return pl.pallas_call(
        _moe_combine_kernel,
        grid_spec=grid_spec,
        out_shape=jax.ShapeDtypeStruct((n_tok, D_MODEL), F32),
        compiler_params=_cparams(("arbitrary",)),
        name="moe_combine_post",
    )(c_t0, c_nt, pos_col, gates, ys, x1, mod, g)


def _l1_in_kernel(x_ref, mod_ref, g_ref, w_ref, gate_ref, xr_ref, xq_ref):
    h = _pre(x_ref[...], g_ref[0:1, :], mod_ref[0, 0:1, :], mod_ref[0, 1:2, :]).astype(BF16)
    z = _dot(h, w_ref[...])
    gate_ref[...] = z[:, 0:D_RNN].astype(BF16)
    xr_ref[...] = z[:, D_RNN:2 * D_RNN]
    xq_ref[...] = z[:, 2 * D_RNN:]


def _l1_in(x, mod, g, w_in, *, tm, rows_per_batch):
    rows = x.shape[0]
    tpb = rows_per_batch // tm
    nb = mod.shape[0]
    row_spec = lambda w: pl.BlockSpec((tm, w), lambda i: (i, 0))
    return pl.pallas_call(
        _l1_in_kernel,
        grid=(rows // tm,),
        in_specs=[row_spec(D_MODEL),
                  pl.BlockSpec((1, 6, D_MODEL), lambda i: ((i // tpb) % nb, 0, 0)),
                  _const_spec((4, D_MODEL)),
                  _const_spec((D_MODEL, w_in.shape[1]))],
        out_specs=[row_spec(D_RNN), row_spec(D_RNN), row_spec(D_POOL)],
        out_shape=[jax.ShapeDtypeStruct((rows, D_RNN), BF16),
                   jax.ShapeDtypeStruct((rows, D_RNN), F32),
                   jax.ShapeDtypeStruct((rows, D_POOL), F32)],
        compiler_params=_cparams(("arbitrary",)),
        name="l1_in_proj",
    )(x, mod, g, w_in)


def _with_halo(x_ref, prev_ref, next_ref, t_idx, n_tiles):
    x = x_ref[...]
    prev = jnp.where(t_idx > 0, prev_ref[:, 0], 0.0)
    nxt = jnp.where(t_idx < n_tiles - 1, next_ref[:, 0], 0.0)
    return jnp.concatenate([prev, x, nxt], axis=1)


def _scan_kernel(*refs, fwd, tt, n_tiles, n_seq):
    if fwd:
        (xr_ref, xrp_ref, xrn_ref, xq_ref, xqp_ref, xqn_ref, cw_ref, cb_ref, wgt_ref, ba_ref, bx_ref,
         lam_ref, h0_ref, pw_ref, ps_ref, hf_ref, yp_ref, hl_ref, a_scr, b_scr, h_scr, carry) = refs
    else:
        (xr_ref, xrp_ref, xrn_ref, cw_ref, cb_ref, wgt_ref, ba_ref, bx_ref, lam_ref, h0_ref,
         hf_ref, gate_ref, yr_ref, hl_ref, a_scr, b_scr, h_scr, carry) = refs
    i = pl.program_id(1)
    t_idx = i if fwd else n_tiles - 1 - i
    nb = SUBLANES

    ext = _with_halo(xr_ref, xrp_ref, xrn_ref, t_idx, n_tiles)
    xc = cb_ref[...][None]
    for j in range(CONV_W):
        lo = HALO + j - CONV_LEFT
        xc = xc + cw_ref[j:j + 1, :][None] * ext[:, lo:lo + tt, :]
    xc = xc.reshape(nb * tt, D_RNN)
    xb = xc.astype(BF16)
    r0 = _dot(xb[:, :RNN_HALF], wgt_ref[0])
    r1 = _dot(xb[:, RNN_HALF:], wgt_ref[1])
    ra = jnp.concatenate([r0[:, :RNN_HALF], r1[:, :RNN_HALF]], axis=1) + ba_ref[...]
    ri = jnp.concatenate([r0[:, RNN_HALF:], r1[:, RNN_HALF:]], axis=1) + bx_ref[...]
    nl = -lam_ref[...]
    softplus = jnp.maximum(nl, 0.0) + jnp.log1p(jnp.exp(-jnp.abs(nl)))
    a = jnp.exp((-0.5 * RG_C * softplus) * (1.0 + jnp.tanh(0.5 * ra)))
    b = jnp.sqrt(1.0 - a * a) * ((0.5 + 0.5 * jnp.tanh(0.5 * ri)) * xc)
    n_lc = D_RNN // LANES
    for bi in range(nb):
        for lc in range(n_lc):
            a_scr[lc, pl.ds(bi, tt, stride=nb), :] = a[bi * tt:(bi + 1) * tt, lc * LANES:(lc + 1) * LANES]
            b_scr[lc, pl.ds(bi, tt, stride=nb), :] = b[bi * tt:(bi + 1) * tt, lc * LANES:(lc + 1) * LANES]

    @pl.when(i == 0)
    def _():
        for lc in range(n_lc):
            carry[lc] = h0_ref[:, lc * LANES:(lc + 1) * LANES]

    def step(s, h):
        t = s if fwd else tt - 1 - s
        row = pl.multiple_of(t * nb, nb)
        h = a_scr[:, pl.ds(row, nb), :] * h + b_scr[:, pl.ds(row, nb), :]
        h_scr[:, pl.ds(row, nb), :] = h
        return h

    h_last = lax.fori_loop(0, tt, step, carry[...], unroll=8)
    carry[...] = h_last
    for lc in range(n_lc):
        hl_ref[:, lc * LANES:(lc + 1) * LANES] = h_last[lc]

    def unscan(bi):
        return jnp.concatenate([h_scr[lc, pl.ds(bi, tt, stride=nb), :] for lc in range(n_lc)], axis=1)

    if fwd:
        for bi in range(nb):
            hf_ref[bi] = unscan(bi)
        e = _with_halo(xq_ref, xqp_ref, xqn_ref, t_idx, n_tiles)
        length = tt + 2 * HALO
        a2 = e[:, 0:length - 1] + e[:, 1:length]
        a4 = a2[:, 0:length - 3] + a2[:, 2:length - 1]
        a8 = a4[:, 0:length - 7] + a4[:, 4:length - 3]
        a16 = a8[:, 0:length - 15] + a8[:, 8:length - 7]
        lane = lax.broadcasted_iota(jnp.int32, (1, 1, D_POOL), 2)
        wsum = jnp.where(lane < POOL_C, a2[:, HALO - 1:HALO - 1 + tt],
                         jnp.where(lane < 2 * POOL_C, a4[:, HALO - 2:HALO - 2 + tt],
                                   jnp.where(lane < 3 * POOL_C, a8[:, HALO - 4:HALO - 4 + tt],
                                             a16[:, 0:tt])))
        half = jnp.where(lane < POOL_C, POOL_WINDOWS[0] // 2,
                         jnp.where(lane < 2 * POOL_C, POOL_WINDOWS[1] // 2,
                                   jnp.where(lane < 3 * POOL_C, POOL_WINDOWS[2] // 2,
                                             POOL_WINDOWS[3] // 2)))
        tg = t_idx * tt + lax.broadcasted_iota(jnp.int32, (1, tt, D_POOL), 1)
        cnt = (jnp.minimum(tg + half, n_seq) - jnp.maximum(tg - half, 0)).astype(F32)
        dlt = (wsum / cnt - xq_ref[...]).astype(BF16).reshape(nb * tt, D_POOL)
        yp = _dot(dlt, pw_ref[...]) * ps_ref[...]
        yp_ref[...] = yp.reshape(nb, tt, D_POOL).astype(BF16)
    else:
        for bi in range(nb):
            hb = unscan(bi)
            yr_ref[bi] = ((hf_ref[bi] + hb) * _gelu(gate_ref[bi].astype(F32))).astype(BF16)


def _scan(fwd, xr, xq, conv_w, conv_b, w_gates, ba, bx, lam, h0, pool_w, pool_scale, h_f, gate, *, tt):
    n_batch, n_seq, _ = xr.shape
    n_bg = n_batch // SUBLANES
    n_tiles = n_seq // tt
    hb = tt // HALO
    n_hb = n_seq // HALO

    def t_of(i):
        return i if fwd else n_tiles - 1 - i

    def tile(c):
        return pl.BlockSpec((SUBLANES, tt, c), lambda b, i: (b, t_of(i), 0))

    def prev(c):
        return pl.BlockSpec((SUBLANES, 1, HALO, c),
                            lambda b, i: (b, jnp.maximum(t_of(i) * hb - 1, 0), 0, 0))

    def nxt(c):
        return pl.BlockSpec((SUBLANES, 1, HALO, c),
                            lambda b, i: (b, jnp.minimum((t_of(i) + 1) * hb, n_hb - 1), 0, 0))

    state = pl.BlockSpec((SUBLANES, D_RNN), lambda b, i: (b, 0))
    xr4 = xr.reshape(n_batch, n_hb, HALO, D_RNN)
    common = [_const_spec((CONV_W, D_RNN)), _const_spec((1, D_RNN)),
              _const_spec((2, RNN_HALF, D_RNN)), _const_spec((1, D_RNN)), _const_spec((1, D_RNN)),
              _const_spec((1, D_RNN)), state]
    common_args = [conv_w, conv_b, w_gates, ba, bx, lam, h0]
    n_lc = D_RNN // LANES
    scratch = ([pltpu.VMEM((n_lc, SUBLANES * tt, LANES), F32)] * 3
               + [pltpu.VMEM((n_lc, SUBLANES, LANES), F32)])
    if fwd:
        xq4 = xq.reshape(n_batch, n_hb, HALO, D_POOL)
        in_specs = ([tile(D_RNN), prev(D_RNN), nxt(D_RNN), tile(D_POOL), prev(D_POOL), nxt(D_POOL)]
                    + common + [_const_spec((D_POOL, D_POOL)), _const_spec((1, D_POOL))])
        args = [xr, xr4, xr4, xq, xq4, xq4] + common_args + [pool_w, pool_scale]
        out_specs = [tile(D_RNN), tile(D_POOL), state]
        out_shape = [jax.ShapeDtypeStruct((n_batch, n_seq, D_RNN), F32),
                     jax.ShapeDtypeStruct((n_batch, n_seq, D_POOL), BF16),
                     jax.ShapeDtypeStruct((n_batch, D_RNN), F32)]
    else:
        in_specs = [tile(D_RNN), prev(D_RNN), nxt(D_RNN)] + common + [tile(D_RNN), tile(D_RNN)]
        args = [xr, xr4, xr4] + common_args + [h_f, gate]
        out_specs = [tile(D_RNN), state]
        out_shape = [jax.ShapeDtypeStruct((n_batch, n_seq, D_RNN), BF16),
                     jax.ShapeDtypeStruct((n_batch, D_RNN), F32)]
    return pl.pallas_call(
        functools.partial(_scan_kernel, fwd=fwd, tt=tt, n_tiles=n_tiles, n_seq=n_seq),
        grid=(n_bg, n_tiles),
        in_specs=in_specs,
        out_specs=out_specs,
        out_shape=out_shape,
        scratch_shapes=scratch,
        compiler_params=_cparams(("arbitrary", "arbitrary")),
        name="rglru_forward_pool" if fwd else "rglru_backward_combine",
    )(*args)


def _gate_weights(wa, wx):
    per_half = RNN_BLOCKS // 2

    def half_dense(w, k):
        blocks = [w[k * per_half + b] for b in range(per_half)]
        rows = []
        for bi, blk in enumerate(blocks):
            row = [blk if bj == bi else jnp.zeros_like(blk) for bj in range(per_half)]
            rows.append(jnp.concatenate(row, axis=1))
        return jnp.concatenate(rows, axis=0)

    return jnp.stack([jnp.concatenate([half_dense(wa, k), half_dense(wx, k)], axis=1)
                      for k in range(2)]).astype(BF16)


def _pool_weights(pool_w):
    n = pool_w.shape[0]
    rows = []
    for i in range(n):
        rows.append(jnp.concatenate([pool_w[i] if j == i else jnp.zeros_like(pool_w[i])
                                     for j in range(n)], axis=1))
    return jnp.concatenate(rows, axis=0).astype(BF16)


def kernel(x_prompt, x_sample, cache_k, cache_v, state_h, c, c_ctx, w_ada, b_ada, norm_g, w_in_even,
           w_out_even, na_rpb, sgu_w, sgu_b, w_in_odd, w_out_odd, conv_w, conv_b, rg_wa, rg_ba, rg_wx,
           rg_bx, rg_lam, pool_w, pool_scale, ffn_wg, ffn_wu, ffn_wd, router_w, moe_wg, moe_wu, moe_wd):
    bp, n_p, _ = x_prompt.shape
    bs, n_s, _ = x_sample.shape
    n_ctx = cache_k.shape[3]

    cond = jnp.concatenate([c_ctx[None, :], c, jnp.zeros((2 * SUBLANES - 1 - bs, D_MODEL), F32)], axis=0)
    mods = _modulation(cond, w_ada, b_ada).reshape(w_ada.shape[0], cond.shape[0], 6, D_MODEL)

    streams = [
        dict(x=x_prompt.reshape(bp * n_p, D_MODEL), nb=bp, n=n_p, tm=256, latent=False),
        dict(x=x_sample.reshape(bs * n_s, D_MODEL), nb=bs, n=n_s, tm=512, latent=True),
    ]
    new_k = new_v = new_h = None

    g = norm_g[0]
    w_in = w_in_even[0].astype(BF16)
    w_out = w_out_even[0].astype(BF16)
    w_s = sgu_w[0].astype(BF16)
    b_s_t = sgu_b[0].T
    wg, wu, wd = (w[0:1].astype(BF16) for w in (ffn_wg, ffn_wu, ffn_wd))
    bias = _latent_bias_table(na_rpb[0], n_s // GRID_W)
    kc = cache_k[:, 0].transpose(0, 2, 1, 3).reshape(bs * n_ctx, W_A).astype(BF16)
    vc = cache_v[:, 0].transpose(0, 2, 1, 3).reshape(bs * n_ctx, W_A).astype(BF16)
    for st in streams:
        mod = mods[0, 1:1 + bs] if st["latent"] else mods[0, 0:1]
        outs = _l0_in(st["x"], mod, g, w_in, w_s, b_s_t, tm=st["tm"], rows_per_batch=st["n"],
                      emit_f32=not st["latent"])
        q, k, v, sg = outs[:4]
        if st["latent"]:
            oa = _attn_latent(q, k, v, kc, vc, bias, n_batch=st["nb"], n_seq=st["n"], n_ctx=n_ctx)
        else:
            oa = _attn_ctx(q, k, v, n_batch=st["nb"], n_seq=st["n"])
            heads = lambda a: a.reshape(bp, n_p, N_HEADS, HEAD_DIM).transpose(0, 2, 1, 3)[:, None]
            new_k, new_v = heads(outs[4]), heads(outs[5])
        x1, h2 = _out_proj([oa, sg], st["x"], mod, g, w_out, None, tm=st["tm"], rows_per_batch=st["n"])
        st["x"] = _ffn(h2, x1, mod, g, None, wg, wu, wd, tm=st["tm"], rows_per_batch=st["n"])

    g = norm_g[1]
    w_in = w_in_odd[0].astype(BF16)
    w_out = w_out_odd[0].astype(BF16)
    w_gates = [_gate_weights(rg_wa[0, d], rg_wx[0, d]) for d in range(2)]
    pw = _pool_weights(pool_w[0])
    ps = pool_scale[0][None, :]
    rw = jnp.pad(router_w[0].T, ((0, LANES - N_EXPERTS), (0, 0)))
    wg, wu, wd = (w[0].astype(BF16) for w in (moe_wg, moe_wu, moe_wd))
    for st in streams:
        mod = mods[1, 1:1 + bs] if st["latent"] else mods[1, 0:1]
        nb, n = st["nb"], st["n"]
        gate, xr, xq = _l1_in(st["x"], mod, g, w_in, tm=st["tm"], rows_per_batch=n)
        gate, xr, xq = (a.reshape(nb, n, a.shape[1]) for a in (gate, xr, xq))
        h0 = state_h[:, 0].astype(F32) if st["latent"] else jnp.zeros((nb, 2, D_RNN), F32)
        scan_args = lambda d: (conv_w[0], conv_b[0][None, :], w_gates[d], rg_ba[0, d][None, :],
                               rg_bx[0, d][None, :], rg_lam[0, d][None, :], h0[:, d])
        h_f, y_pool, h_f_last = _scan(True, xr, xq, *scan_args(0), pw, ps, None, None, tt=128)
        y_rec, h_b_last = _scan(False, xr, None, *scan_args(1), None, None, h_f, gate, tt=128)
        if not st["latent"]:
            new_h = jnp.stack([h_f_last, h_b_last], axis=1)[:, None]
        x1, h2, gates, rank_col, rank_row = _out_proj(
            [y_rec.reshape(nb * n, D_RNN), y_pool.reshape(nb * n, D_POOL)],
            st["x"], mod, g, w_out, rw, tm=st["tm"], rows_per_batch=n)
        pos_row, pos_col, tile_e, tile_lo, tile_n, c_t0, c_nt, n_tiles = _moe_plan(rank_row, rank_col)
        ys = _moe_ffn(h2, pos_row, tile_e, tile_lo, tile_n, wg, wu, wd, n_tiles=n_tiles)
        st["x"] = _moe_combine(ys, pos_col, gates, c_t0, c_nt, x1, mod, g, rows_per_batch=n)

    y_prompt = streams[0]["x"].reshape(bp, n_p, D_MODEL)
    y_sample = streams[1]["x"].reshape(bs, n_s, D_MODEL)
    return (y_prompt, y_sample, new_k, new_v, new_h)
```

```python
import functools

import jax
import jax.numpy as jnp
import numpy as np
from jax import lax
from jax.experimental import pallas as pl
from jax.experimental.pallas import tpu as pltpu

F32 = jnp.float32
BF16 = jnp.bfloat16

D_MODEL = 1024
GRID_W = 64
HEAD_DIM = 64
N_HEADS = 8
W_A = N_HEADS * HEAD_DIM
WIN_R = 8
WIN_C = 16
N_GROUPS_B = 4
W_B = 512
CHUNK = 128
D_RNN = 768
RNN_BLOCKS = 8
RNN_BW = D_RNN // RNN_BLOCKS
RNN_HALF = D_RNN // 2
CONV_W = 4
CONV_LEFT = 2
RG_C = 8.0
D_POOL = 256
POOL_WINDOWS = (2, 4, 8, 16)
POOL_C = D_POOL // len(POOL_WINDOWS)
D_FF = 2816
N_EXPERTS = 8
EPS = 1e-6
NEG = -1e30

LANES = 128
SUBLANES = 8
VMEM_LIMIT = 56 * 1024 * 1024
Q_ROWS = 4
KEY_ROWS = Q_ROWS + WIN_R - 1
HALO = 8
FF_CHUNKS = 2
FF_CHUNK = D_FF // FF_CHUNKS
MOE_SRC = 256
MOE_TILE = 256
MOE_RING = 8
MOE_CTILE = 128
MOE_CMAX = (MOE_SRC - 1 + MOE_CTILE - 1) // MOE_CTILE + 1


def _cparams(sem):
    return pltpu.CompilerParams(dimension_semantics=sem, vmem_limit_bytes=VMEM_LIMIT)


def _const_spec(shape):
    nd = len(shape)
    return pl.BlockSpec(shape, lambda *_: (0,) * nd, pipeline_mode=pl.Buffered(1))


def _dot(a, b):
    return jnp.dot(a, b, preferred_element_type=F32)


def _dot_nt(a, b):
    return lax.dot_general(a, b, (((1,), (1,)), ((), ())), preferred_element_type=F32)


def _rms(x, g):
    ms = jnp.mean(x * x, axis=-1, keepdims=True)
    return x * lax.rsqrt(ms + EPS) * g


def _pre(x, g, shift, scale):
    return _rms(x, g) * (1.0 + scale) + shift


def _gelu(x):
    return jax.nn.gelu(x, approximate=True)


def _sigmoid(x):
    return 1.0 / (1.0 + jnp.exp(-x))


def _silu(x):
    return x * _sigmoid(x)


def _mod_kernel(c_ref, w_ref, b_ref, o_ref):
    s = _silu(c_ref[...])
    o_ref[0] = jnp.dot(s, w_ref[0], preferred_element_type=F32,
                       precision=lax.Precision.HIGHEST) + b_ref[0]


def _modulation(cond, w_ada, b_ada):
    depth, d, n = w_ada.shape
    rows = cond.shape[0]
    bn = 768
    return pl.pallas_call(
        _mod_kernel,
        grid=(depth, n // bn),
        in_specs=[
            pl.BlockSpec((rows, d), lambda i, j: (0, 0)),
            pl.BlockSpec((1, d, bn), lambda i, j: (i, 0, j)),
            pl.BlockSpec((1, 1, bn), lambda i, j: (i, 0, j)),
        ],
        out_specs=pl.BlockSpec((1, rows, bn), lambda i, j: (i, 0, j)),
        out_shape=jax.ShapeDtypeStruct((depth, rows, n), F32),
        compiler_params=_cparams(("arbitrary", "arbitrary")),
        name="adaln_modulation",
    )(cond, w_ada, b_ada.reshape(depth, 1, n))


def _l0_in_kernel(x_ref, mod_ref, g_ref, w_ref, ws_ref, bs_ref, *out_refs, tm, emit_f32):
    q_ref, k_ref, v_ref, sg_ref = out_refs[:4]
    x = x_ref[...]
    h = _pre(x, g_ref[0:1, :], mod_ref[0, 0:1, :], mod_ref[0, 1:2, :]).astype(BF16)
    z = _dot(h, w_ref[...])
    q_ref[...] = z[:, 0:W_A].astype(BF16)
    k_ref[...] = z[:, W_A:2 * W_A].astype(BF16)
    v_ref[...] = z[:, 2 * W_A:3 * W_A].astype(BF16)
    if emit_f32:
        out_refs[4][...] = z[:, W_A:2 * W_A]
        out_refs[5][...] = z[:, 2 * W_A:3 * W_A]
    u = _gelu(z[:, 3 * W_A:3 * W_A + W_B])
    gf = _gelu(z[:, 3 * W_A + W_B:])
    n_chunks = tm // CHUNK
    for gi in range(N_GROUPS_B):
        gg = gf[:, gi * LANES:(gi + 1) * LANES]
        mu = jnp.mean(gg, axis=-1, keepdims=True)
        dd = gg - mu
        var = jnp.mean(dd * dd, axis=-1, keepdims=True)
        gn = (dd * lax.rsqrt(var + EPS)).astype(BF16)
        rhs = jnp.concatenate([gn[c * CHUNK:(c + 1) * CHUNK, :] for c in range(n_chunks)], axis=1)
        s = _dot(ws_ref[gi], rhs) + bs_ref[:, gi:gi + 1]
        for c in range(n_chunks):
            uu = u[c * CHUNK:(c + 1) * CHUNK, gi * LANES:(gi + 1) * LANES]
            sg_ref[c * CHUNK:(c + 1) * CHUNK, gi * LANES:(gi + 1) * LANES] = (
                uu * s[:, c * LANES:(c + 1) * LANES]).astype(BF16)


def _l0_in(x, mod, g, w_in, w_s, b_s_t, *, tm, rows_per_batch, emit_f32):
    rows = x.shape[0]
    tpb = rows_per_batch // tm
    nb = mod.shape[0]
    d_in = w_in.shape[1]
    row_spec = lambda w: pl.BlockSpec((tm, w), lambda i: (i, 0))
    out_shape = [jax.ShapeDtypeStruct((rows, W_A), BF16)] * 3 + [jax.ShapeDtypeStruct((rows, W_B), BF16)]
    out_specs = [row_spec(W_A)] * 3 + [row_spec(W_B)]
    if emit_f32:
        out_shape += [jax.ShapeDtypeStruct((rows, W_A), F32)] * 2
        out_specs += [row_spec(W_A)] * 2
    return pl.pallas_call(
        functools.partial(_l0_in_kernel, tm=tm, emit_f32=emit_f32),
        grid=(rows // tm,),
        in_specs=[
            row_spec(D_MODEL),
            pl.BlockSpec((1, 6, D_MODEL), lambda i: ((i // tpb) % nb, 0, 0)),
            _const_spec((4, D_MODEL)),
            _const_spec((D_MODEL, d_in)),
            _const_spec((N_GROUPS_B, CHUNK, CHUNK)),
            _const_spec((CHUNK, N_GROUPS_B)),
        ],
        out_specs=out_specs,
        out_shape=out_shape,
        compiler_params=_cparams(("arbitrary",)),
        name="l0_in_proj_sgu",
    )(x, mod, g, w_in, w_s, b_s_t)


def _attn_kernel(*refs, n_local, has_ctx, rows):
    if has_ctx:
        q_ref, k_ref, v_ref, kc_ref, vc_ref, bias_ref, o_ref = refs
        r0 = pl.program_id(1) * Q_ROWS
        ks = jnp.clip(r0 - WIN_R // 2, 0, rows - KEY_ROWS)
        start = pl.multiple_of(ks * GRID_W, GRID_W)
        local = pl.ds(start, n_local)
    else:
        q_ref, k_ref, v_ref, o_ref = refs
        local = slice(None)
    first = lax.broadcasted_iota(jnp.int32, (1, LANES), 1) < HEAD_DIM
    for hp in range(W_A // LANES):
        lanes = slice(hp * LANES, (hp + 1) * LANES)
        kl = k_ref[local, lanes]
        vl = v_ref[local, lanes]
        q = q_ref[:, lanes] * jnp.asarray(HEAD_DIM ** -0.5, BF16)
        outs = []
        for half in range(2):
            qh = jnp.where(first if half == 0 else jnp.logical_not(first), q, jnp.zeros_like(q))
            s = _dot_nt(qh, kl)
            if has_ctx:
                s = s + bias_ref[0, 2 * hp + half]
                sc = _dot_nt(qh, kc_ref[:, lanes])
                m = jnp.maximum(jnp.max(s, axis=-1, keepdims=True), jnp.max(sc, axis=-1, keepdims=True))
            else:
                m = jnp.max(s, axis=-1, keepdims=True)
            p = jnp.exp(s - m)
            l = jnp.sum(p, axis=-1, keepdims=True)
            acc = _dot(p.astype(BF16), vl)
            if has_ctx:
                pc = jnp.exp(sc - m)
                l = l + jnp.sum(pc, axis=-1, keepdims=True)
                acc = acc + _dot(pc.astype(BF16), vc_ref[:, lanes])
            outs.append(acc / l)
        o_ref[:, lanes] = jnp.where(first, outs[0], outs[1]).astype(BF16)


def _attn_ctx(q, k, v, *, n_batch, n_seq):
    rows = q.shape[0]
    spec = pl.BlockSpec((n_seq, W_A), lambda b: (b, 0))
    return pl.pallas_call(
        functools.partial(_attn_kernel, n_local=n_seq, has_ctx=False, rows=0),
        grid=(n_batch,),
        in_specs=[spec, spec, spec],
        out_specs=spec,
        out_shape=jax.ShapeDtypeStruct((rows, W_A), BF16),
        compiler_params=_cparams(("arbitrary",)),
        name="attn_context",
    )(q, k, v)


def _attn_latent(q, k, v, kc, vc, bias, *, n_batch, n_seq, n_ctx):
    rows = n_seq // GRID_W
    n_rg = rows // Q_ROWS
    tq = Q_ROWS * GRID_W
    n_local = KEY_ROWS * GRID_W
    img_spec = pl.BlockSpec((n_seq, W_A), lambda b, rg: (b, 0))
    ctx_spec = pl.BlockSpec((n_ctx, W_A), lambda b, rg: (b, 0))
    q_spec = pl.BlockSpec((tq, W_A), lambda b, rg: (b * n_rg + rg, 0))

    def bias_map(b, rg):
        cfg = jnp.where(rg == 0, 0, jnp.where(rg == n_rg - 1, 2, 1))
        return (cfg, 0, 0, 0)

    return pl.pallas_call(
        functools.partial(_attn_kernel, n_local=n_local, has_ctx=True, rows=rows),
        grid=(n_batch, n_rg),
        in_specs=[q_spec, img_spec, img_spec, ctx_spec, ctx_spec,
                  pl.BlockSpec((1, N_HEADS, tq, n_local), bias_map)],
        out_specs=q_spec,
        out_shape=jax.ShapeDtypeStruct((n_batch * n_seq, W_A), BF16),
        compiler_params=_cparams(("arbitrary", "arbitrary")),
        name="attn_latent",
    )(q, k, v, kc, vc, bias)


def _latent_bias_table(rpb, rows):
    n_heads = rpb.shape[0]
    qcol = np.arange(GRID_W)[:, None]
    kcol = np.arange(GRID_W)[None, :]
    cs = np.clip(qcol - WIN_C // 2, 0, GRID_W - WIN_C)
    col_valid = (kcol >= cs) & (kcol < cs + WIN_C)
    dc = np.clip(kcol - qcol + (WIN_C - 1), 0, 2 * WIN_C - 2)
    pick = (dc[:, :, None] == np.arange(2 * WIN_C - 1)).astype(np.float32)
    col_tab = jnp.einsum("hrc,qkc->hrqk", rpb.astype(F32), pick, precision=lax.Precision.HIGHEST)
    col_tab = jnp.where(col_valid[None, None], col_tab, NEG)
    tables = []
    for r0 in (0, 2 * Q_ROWS, rows - Q_ROWS):
        ks = min(max(r0 - WIN_R // 2, 0), rows - KEY_ROWS)
        qrow = r0 + np.arange(Q_ROWS)[:, None]
        krow = ks + np.arange(KEY_ROWS)[None, :]
        rs = np.clip(qrow - WIN_R // 2, 0, rows - WIN_R)
        row_valid = (krow >= rs) & (krow < rs + WIN_R)
        dr = np.clip(krow - qrow + (WIN_R - 1), 0, 2 * WIN_R - 2)
        blocks = [jnp.stack([col_tab[:, dr[i, k]] if row_valid[i, k]
                             else jnp.full((n_heads, GRID_W, GRID_W), NEG, F32)
                             for k in range(KEY_ROWS)], axis=2)
                  for i in range(Q_ROWS)]
        b = jnp.stack(blocks, axis=1)
        tables.append(b.reshape(n_heads, Q_ROWS * GRID_W, KEY_ROWS * GRID_W))
    return jnp.stack(tables)


def _out_kernel(*refs, n_in, router):
    in_refs = refs[:n_in]
    x_ref, mod_ref, g_ref, w_ref = refs[n_in:n_in + 4]
    rest = refs[n_in + 4:]
    if router:
        rw_ref, x1_ref, h2_ref, gates_ref, rankc_ref, rankr_ref = rest
    else:
        x1_ref, h2_ref = rest
    y = None
    off = 0
    for r in in_refs:
        w = r.shape[1]
        part = _dot(r[...], w_ref[off:off + w, :])
        y = part if y is None else y + part
        off += w
    x1 = x_ref[...] + mod_ref[0, 2:3, :] * _rms(y, g_ref[1:2, :])
    x1_ref[...] = x1
    h2 = _pre(x1, g_ref[2:3, :], mod_ref[0, 3:4, :], mod_ref[0, 4:5, :])
    h2_ref[...] = h2.astype(BF16)
    for sb in range(h2.shape[0] // MOE_SRC if router else 0):
        blk = slice(sb * MOE_SRC, (sb + 1) * MOE_SRC)
        hs = h2[blk, :]
        h_hi = hs.astype(BF16)
        h_lo = (hs - h_hi.astype(F32)).astype(BF16)
        rw = rw_ref[...]
        r_hi = rw.astype(BF16)
        r_lo = (rw - r_hi.astype(F32)).astype(BF16)
        logits = _dot_nt(r_hi, h_hi) + (_dot_nt(r_hi, h_lo) + _dot_nt(r_lo, h_hi))
        tm = MOE_SRC
        row = lax.broadcasted_iota(jnp.int32, logits.shape, 0).astype(F32)
        logits = jnp.where(row < N_EXPERTS, logits, -jnp.inf)
        m1 = jnp.max(logits, axis=0, keepdims=True)
        i1 = jnp.min(jnp.where(logits == m1, row, float(LANES)), axis=0, keepdims=True)
        rest_l = jnp.where(row == i1, -jnp.inf, logits)
        m2 = jnp.max(rest_l, axis=0, keepdims=True)
        i2 = jnp.min(jnp.where(rest_l == m2, row, float(LANES)), axis=0, keepdims=True)
        e2 = jnp.exp(m2 - m1)
        w1 = 1.0 / (1.0 + e2)
        w2 = e2 / (1.0 + e2)
        sel1 = row == i1
        sel2 = row == i2
        member = jnp.where(sel1, 1.0, 0.0) + jnp.where(sel2, 1.0, 0.0)
        gates = jnp.where(sel1, w1, 0.0) + jnp.where(sel2, w2, 0.0)
        before = (lax.broadcasted_iota(jnp.int32, (tm, tm), 0)
                  < lax.broadcasted_iota(jnp.int32, (tm, tm), 1))
        rank = _dot(member.astype(BF16), jnp.where(before, 1.0, 0.0).astype(BF16))
        rank = jnp.where(member > 0.0, rank, -1.0)
        gates_ref[blk, :] = gates.T
        rankc_ref[blk, :] = rank.T.astype(jnp.int32)
        rankr_ref[:, blk] = rank[0:N_EXPERTS, :].astype(jnp.int32)


def _out_proj(ins, x, mod, g, w_out, router_w, *, tm, rows_per_batch):
    rows = x.shape[0]
    tpb = rows_per_batch // tm
    nb = mod.shape[0]
    router = router_w is not None
    row_spec = lambda w: pl.BlockSpec((tm, w), lambda i: (i, 0))
    in_specs = [row_spec(a.shape[1]) for a in ins] + [
        row_spec(D_MODEL),
        pl.BlockSpec((1, 6, D_MODEL), lambda i: ((i // tpb) % nb, 0, 0)),
        _const_spec((4, D_MODEL)),
        _const_spec((D_MODEL, D_MODEL)),
    ]
    args = list(ins) + [x, mod, g, w_out]
    out_shape = [jax.ShapeDtypeStruct((rows, D_MODEL), F32), jax.ShapeDtypeStruct((rows, D_MODEL), BF16)]
    out_specs = [row_spec(D_MODEL), row_spec(D_MODEL)]
    if router:
        in_specs.append(_const_spec((LANES, D_MODEL)))
        args.append(router_w)
        out_shape += [jax.ShapeDtypeStruct((rows, LANES), F32),
                      jax.ShapeDtypeStruct((rows, LANES), jnp.int32),
                      jax.ShapeDtypeStruct((N_EXPERTS, rows), jnp.int32)]
        out_specs += [row_spec(LANES), row_spec(LANES),
                      pl.BlockSpec((N_EXPERTS, tm), lambda i: (0, i))]
    return pl.pallas_call(
        functools.partial(_out_kernel, n_in=len(ins), router=router),
        grid=(rows // tm,),
        in_specs=in_specs,
        out_specs=out_specs,
        out_shape=out_shape,
        compiler_params=_cparams(("arbitrary",)),
        name="out_proj_norms_router" if router else "out_proj_norms",
    )(*args)


def _ffn_kernel(*refs, gated, n_e):
    if gated:
        h_ref, x1_ref, mod_ref, g_ref, gates_ref, wg_ref, wu_ref, wd_ref, o_ref, acc_ref = refs
    else:
        h_ref, x1_ref, mod_ref, g_ref, wg_ref, wu_ref, wd_ref, o_ref, acc_ref = refs
    e = pl.program_id(1)
    fc = pl.program_id(2)

    @pl.when((e == 0) & (fc == 0))
    def _():
        acc_ref[...] = jnp.zeros_like(acc_ref)

    h = h_ref[...]
    t = _silu(_dot(h, wg_ref[0])) * _dot(h, wu_ref[0])
    y = _dot(t.astype(BF16), wd_ref[0])
    if gated:
        gates = gates_ref[...]
        lane = lax.broadcasted_iota(jnp.int32, gates.shape, 1)
        y = y * jnp.sum(jnp.where(lane == e, gates, 0.0), axis=-1, keepdims=True)
    acc_ref[...] += y

    @pl.when((e == n_e - 1) & (fc == FF_CHUNKS - 1))
    def _():
        o_ref[...] = x1_ref[...] + mod_ref[0, 5:6, :] * _rms(acc_ref[...], g_ref[3:4, :])


def _ffn(h2, x1, mod, g, gates, wg, wu, wd, *, tm, rows_per_batch):
    rows = h2.shape[0]
    tpb = rows_per_batch // tm
    nb = mod.shape[0]
    n_e = wg.shape[0]
    gated = gates is not None
    row_spec = lambda w: pl.BlockSpec((tm, w), lambda i, e, f: (i, 0))
    in_specs = [row_spec(D_MODEL), row_spec(D_MODEL),
                pl.BlockSpec((1, 6, D_MODEL), lambda i, e, f: ((i // tpb) % nb, 0, 0)),
                _const_spec((4, D_MODEL))]
    args = [h2, x1, mod, g]
    if gated:
        in_specs.append(row_spec(LANES))
        args.append(gates)
    in_specs += [
        pl.BlockSpec((1, D_MODEL, FF_CHUNK), lambda i, e, f: (e, 0, f)),
        pl.BlockSpec((1, D_MODEL, FF_CHUNK), lambda i, e, f: (e, 0, f)),
        pl.BlockSpec((1, FF_CHUNK, D_MODEL), lambda i, e, f: (e, f, 0)),
    ]
    args += [wg, wu, wd]
    return pl.pallas_call(
        functools.partial(_ffn_kernel, gated=gated, n_e=n_e),
        grid=(rows // tm, n_e, FF_CHUNKS),
        in_specs=in_specs,
        out_specs=row_spec(D_MODEL),
        out_shape=jax.ShapeDtypeStruct((rows, D_MODEL), F32),
        scratch_shapes=[pltpu.VMEM((tm, D_MODEL), F32)],
        compiler_params=_cparams(("arbitrary", "arbitrary", "arbitrary")),
        name="moe_swiglu_post" if gated else "swiglu_post",
    )(*args)


def _moe_plan(rank_row, rank_col):
    n_tok = rank_row.shape[1]
    n_src = n_tok // MOE_SRC
    n_tiles = 2 * n_tok // MOE_TILE + N_EXPERTS
    member = rank_row >= 0
    cnt_blk = member.reshape(N_EXPERTS, n_src, MOE_SRC).sum(-1).astype(jnp.int32)
    cum = jnp.concatenate([jnp.zeros((N_EXPERTS, 1), jnp.int32), jnp.cumsum(cnt_blk, axis=1)], axis=1)
    cnt = cum[:, -1]
    tiles_e = (cnt + MOE_TILE - 1) // MOE_TILE
    tile_end = jnp.cumsum(tiles_e)
    tile_base = tile_end - tiles_e
    slot0 = tile_base[:, None] * MOE_TILE + cum[:, :-1]
    pos_row = jnp.where(member, jnp.repeat(slot0, MOE_SRC, axis=1) + rank_row, -1)
    pos_row = jnp.pad(pos_row, ((0, 0), (0, MOE_RING * MOE_SRC)), constant_values=-1)
    slot0_col = jnp.pad(slot0.T, ((0, 0), (0, LANES - N_EXPERTS)))
    pos_col = jnp.where(rank_col >= 0, jnp.repeat(slot0_col, MOE_SRC, axis=0) + rank_col, -1)
    d = jnp.arange(n_tiles, dtype=jnp.int32)
    tile_e = jnp.minimum(jnp.sum(d[:, None] >= tile_end[None, :], axis=1), N_EXPERTS - 1).astype(jnp.int32)
    valid = d < tile_end[-1]
    lo_slot = (d - tile_base[tile_e]) * MOE_TILE
    hi_slot = jnp.minimum(lo_slot + MOE_TILE, cnt[tile_e])
    cum_d = cum[tile_e]
    tile_lo = jnp.sum(cum_d[:, 1:] <= lo_slot[:, None], axis=1).astype(jnp.int32)
    tile_hi = jnp.sum(cum_d[:, :-1] < hi_slot[:, None], axis=1).astype(jnp.int32) - 1
    tile_n = jnp.where(valid, tile_hi - tile_lo + 1, 0).astype(jnp.int32)
    tile_lo = jnp.where(valid, tile_lo, 0)
    first = slot0.T
    c_t0 = first // MOE_CTILE
    c_nt = jnp.where(cnt_blk.T > 0, (first + cnt_blk.T - 1) // MOE_CTILE - c_t0 + 1, 0)
    return (pos_row, pos_col, tile_e, tile_lo, tile_n,
            c_t0.reshape(-1).astype(jnp.int32), c_nt.reshape(-1).astype(jnp.int32), n_tiles)


def _moe_ffn_kernel(te_ref, tl_ref, tn_ref, pos_ref, h_hbm, wg_ref, wu_ref, wd_ref, o_ref,
                    hbuf, sem, acc_ref, *, n_tiles):
    d = pl.program_id(0)
    e = te_ref[d]
    lo = tl_ref[d]
    n = tn_ref[d]

    def copy(blk, slot):
        start = pl.multiple_of(blk * MOE_SRC, MOE_SRC)
        return pltpu.make_async_copy(h_hbm.at[pl.ds(start, MOE_SRC), :], hbuf.at[slot], sem.at[slot])

    def start_first(first_blk, count):
        for k in range(MOE_RING):
            @pl.when(k < count)
            def _(k=k):
                copy(first_blk + k, k).start()

    @pl.when(d == 0)
    def _():
        hbuf[...] = jnp.zeros_like(hbuf)
        start_first(lo, n)

    slot_ids = d * MOE_TILE + lax.broadcasted_iota(jnp.int32, (MOE_TILE, 1), 0)

    def picks(first_blk, n_blk):
        start = pl.multiple_of(first_blk * MOE_SRC, MOE_SRC)
        pos = pos_ref[pl.ds(e, 1), pl.ds(start, n_blk * MOE_SRC)]
        return jnp.where(pos == slot_ids, 1.0, 0.0).astype(BF16)

    @pl.when(n > 0)
    def _():
        for k in range(MOE_RING):
            @pl.when(k < n)
            def _(k=k):
                copy(lo + k, k).wait()
        acc_ref[...] = _dot(picks(lo, MOE_RING), hbuf[...].reshape(MOE_RING * MOE_SRC, D_MODEL))

    def extra(j, carry):
        cp = copy(lo + j, 0)
        cp.start()
        cp.wait()
        acc_ref[...] += _dot(picks(lo + j, 1), hbuf[0])
        return carry

    lax.fori_loop(MOE_RING, n, extra, 0)

    @pl.when(d + 1 < n_tiles)
    def _():
        start_first(tl_ref[d + 1], tn_ref[d + 1])

    @pl.when(n > 0)
    def _():
        x = acc_ref[...].astype(BF16)
        y = None
        for fc in range(FF_CHUNKS):
            cols = slice(fc * FF_CHUNK, (fc + 1) * FF_CHUNK)
            t = _silu(_dot(x, wg_ref[0, :, cols])) * _dot(x, wu_ref[0, :, cols])
            part = _dot(t.astype(BF16), wd_ref[0, cols, :])
            y = part if y is None else y + part
        o_ref[...] = y.astype(BF16)

    @pl.when(n == 0)
    def _():
        o_ref[...] = jnp.zeros_like(o_ref)


def _moe_ffn(h2, pos_row, tile_e, tile_lo, tile_n, wg, wu, wd, *, n_tiles):
    n_tok = h2.shape[0]
    w_spec = lambda shape: pl.BlockSpec((1,) + shape, lambda d, te, tl, tn: (te[d], 0, 0))
    grid_spec = pltpu.PrefetchScalarGridSpec(
        num_scalar_prefetch=3,
        grid=(n_tiles,),
        in_specs=[
            pl.BlockSpec(pos_row.shape, lambda d, te, tl, tn: (0, 0), pipeline_mode=pl.Buffered(1)),
            pl.BlockSpec(memory_space=pl.ANY),
            w_spec((D_MODEL, D_FF)), w_spec((D_MODEL, D_FF)), w_spec((D_FF, D_MODEL)),
        ],
        out_specs=pl.BlockSpec((MOE_TILE, D_MODEL), lambda d, te, tl, tn: (d, 0)),
        scratch_shapes=[pltpu.VMEM((MOE_RING, MOE_SRC, D_MODEL), BF16),
                        pltpu.SemaphoreType.DMA((MOE_RING,)),
                        pltpu.VMEM((MOE_TILE, D_MODEL), F32)],
    )
    return pl.pallas_call(
        functools.partial(_moe_ffn_kernel, n_tiles=n_tiles),
        grid_spec=grid_spec,
        out_shape=jax.ShapeDtypeStruct((n_tiles * MOE_TILE, D_MODEL), BF16),
        compiler_params=_cparams(("arbitrary",)),
        name="moe_dispatch_swiglu",
    )(tile_e, tile_lo, tile_n, pos_row, h2, wg, wu, wd)


def _moe_combine_kernel(t0_ref, nt_ref, pos_ref, gates_ref, ys_hbm, x1_ref, mod_ref, g_ref, o_ref,
                        ybuf, sem):
    s = pl.program_id(0)

    def copy(blk, e, k):
        start = pl.multiple_of((t0_ref[blk * N_EXPERTS + e] + k) * MOE_CTILE, MOE_CTILE)
        buf = (blk % 2) * N_EXPERTS + e
        return pltpu.make_async_copy(ys_hbm.at[pl.ds(start, MOE_CTILE), :],
                                     ybuf.at[buf, pl.ds(k * MOE_CTILE, MOE_CTILE), :],
                                     sem.at[buf * MOE_CMAX + k])

    def start_block(blk):
        for e in range(N_EXPERTS):
            for k in range(MOE_CMAX):
                @pl.when(nt_ref[blk * N_EXPERTS + e] > k)
                def _(e=e, k=k):
                    copy(blk, e, k).start()

    @pl.when(s == 0)
    def _():
        ybuf[...] = jnp.zeros_like(ybuf)
        start_block(s)

    @pl.when(s + 1 < pl.num_programs(0))
    def _():
        start_block(s + 1)

    for e in range(N_EXPERTS):
        for k in range(MOE_CMAX):
            @pl.when(nt_ref[s * N_EXPERTS + e] > k)
            def _(e=e, k=k):
                copy(s, e, k).wait()

    lane = lax.broadcasted_iota(jnp.int32, (1, MOE_CMAX * MOE_CTILE), 1)
    y = None
    for e in range(N_EXPERTS):
        base = t0_ref[s * N_EXPERTS + e] * MOE_CTILE
        pick = jnp.where(pos_ref[:, e:e + 1] == base + lane, 1.0, 0.0).astype(BF16)
        part = gates_ref[:, e:e + 1] * _dot(pick, ybuf[(s % 2) * N_EXPERTS + e])
        y = part if y is None else y + part

    o_ref[...] = x1_ref[...] + mod_ref[0, 5:6, :] * _rms(y, g_ref[3:4, :])


def _moe_combine(ys, pos_col, gates, c_t0, c_nt, x1, mod, g, *, rows_per_batch):
    n_tok = x1.shape[0]
    tpb = rows_per_batch // MOE_SRC
    nb = mod.shape[0]
    row_spec = lambda w: pl.BlockSpec((MOE_SRC, w), lambda s, t0, nt: (s, 0))
    grid_spec = pltpu.PrefetchScalarGridSpec(
        num_scalar_prefetch=2,
        grid=(n_tok // MOE_SRC,),
        in_specs=[
            row_spec(LANES), row_spec(LANES),
            pl.BlockSpec(memory_space=pl.ANY),
            row_spec(D_MODEL),
            pl.BlockSpec((1, 6, D_MODEL), lambda s, t0, nt: ((s // tpb) % nb, 0, 0)),
            pl.BlockSpec((4, D_MODEL), lambda s, t0, nt: (0, 0), pipeline_mode=pl.Buffered(1)),
        ],
        out_specs=row_spec(D_MODEL),
        scratch_shapes=[pltpu.VMEM((2 * N_EXPERTS, MOE_CMAX * MOE_CTILE, D_MODEL), BF16),
                        pltpu.SemaphoreType.DMA((2 * N_EXPERTS * MOE_CMAX,))],
    )
    return pl.pallas_call(
        _moe_combine_kernel,
        grid_spec=grid_spec,
        out_shape=jax.ShapeDtypeStruct((n_tok, D_MODEL), F32),
        compiler_params=_cparams(("arbitrary",)),
        name="moe_combine_post",
    )(c_t0, c_nt, pos_col, gates, ys, x1, mod, g)


def _l1_in_kernel(x_ref, mod_ref, g_ref, w_ref, gate_ref, xr_ref, xq_ref):
    h = _pre(x_ref[...], g_ref[0:1, :], mod_ref[0, 0:1, :], mod_ref[0, 1:2, :]).astype(BF16)
    z = _dot(h, w_ref[...])
    gate_ref[...] = z[:, 0:D_RNN].astype(BF16)
    xr_ref[...] = z[:, D_RNN:2 * D_RNN]
    xq_ref[...] = z[:, 2 * D_RNN:]


def _l1_in(x, mod, g, w_in, *, tm, rows_per_batch):
    rows = x.shape[0]
    tpb = rows_per_batch // tm
    nb = mod.shape[0]
    row_spec = lambda w: pl.BlockSpec((tm, w), lambda i: (i, 0))
    return pl.pallas_call(
        _l1_in_kernel,
        grid=(rows // tm,),
        in_specs=[row_spec(D_MODEL),
                  pl.BlockSpec((1, 6, D_MODEL), lambda i: ((i // tpb) % nb, 0, 0)),
                  _const_spec((4, D_MODEL)),
                  _const_spec((D_MODEL, w_in.shape[1]))],
        out_specs=[row_spec(D_RNN), row_spec(D_RNN), row_spec(D_POOL)],
        out_shape=[jax.ShapeDtypeStruct((rows, D_RNN), BF16),
                   jax.ShapeDtypeStruct((rows, D_RNN), F32),
                   jax.ShapeDtypeStruct((rows, D_POOL), F32)],
        compiler_params=_cparams(("arbitrary",)),
        name="l1_in_proj",
    )(x, mod, g, w_in)


def _with_halo(x_ref, prev_ref, next_ref, t_idx, n_tiles):
    x = x_ref[...]
    prev = jnp.where(t_idx > 0, prev_ref[:, 0], 0.0)
    nxt = jnp.where(t_idx < n_tiles - 1, next_ref[:, 0], 0.0)
    return jnp.concatenate([prev, x, nxt], axis=1)


def _scan_kernel(*refs, fwd, tt, n_tiles, n_seq):
    if fwd:
        (xr_ref, xrp_ref, xrn_ref, xq_ref, xqp_ref, xqn_ref, cw_ref, cb_ref, wgt_ref, ba_ref, bx_ref,
         lam_ref, h0_ref, pw_ref, ps_ref, hf_ref, yp_ref, hl_ref, a_scr, b_scr, h_scr, carry) = refs
    else:
        (xr_ref, xrp_ref, xrn_ref, cw_ref, cb_ref, wgt_ref, ba_ref, bx_ref, lam_ref, h0_ref,
         hf_ref, gate_ref, yr_ref, hl_ref, a_scr, b_scr, h_scr, carry) = refs
    i = pl.program_id(1)
    t_idx = i if fwd else n_tiles - 1 - i
    nb = SUBLANES

    ext = _with_halo(xr_ref, xrp_ref, xrn_ref, t_idx, n_tiles)
    xc = cb_ref[...][None]
    for j in range(CONV_W):
        lo = HALO + j - CONV_LEFT
        xc = xc + cw_ref[j:j + 1, :][None] * ext[:, lo:lo + tt, :]
    xc = xc.reshape(nb * tt, D_RNN)
    xb = xc.astype(BF16)
    r0 = _dot(xb[:, :RNN_HALF], wgt_ref[0])
    r1 = _dot(xb[:, RNN_HALF:], wgt_ref[1])
    ra = jnp.concatenate([r0[:, :RNN_HALF], r1[:, :RNN_HALF]], axis=1) + ba_ref[...]
    ri = jnp.concatenate([r0[:, RNN_HALF:], r1[:, RNN_HALF:]], axis=1) + bx_ref[...]
    nl = -lam_ref[...]
    softplus = jnp.maximum(nl, 0.0) + jnp.log1p(jnp.exp(-jnp.abs(nl)))
    a = jnp.exp((-0.5 * RG_C * softplus) * (1.0 + jnp.tanh(0.5 * ra)))
    b = jnp.sqrt(1.0 - a * a) * ((0.5 + 0.5 * jnp.tanh(0.5 * ri)) * xc)
    n_lc = D_RNN // LANES
    for bi in range(nb):
        for lc in range(n_lc):
            a_scr[lc, pl.ds(bi, tt, stride=nb), :] = a[bi * tt:(bi + 1) * tt, lc * LANES:(lc + 1) * LANES]
            b_scr[lc, pl.ds(bi, tt, stride=nb), :] = b[bi * tt:(bi + 1) * tt, lc * LANES:(lc + 1) * LANES]

    @pl.when(i == 0)
    def _():
        for lc in range(n_lc):
            carry[lc] = h0_ref[:, lc * LANES:(lc + 1) * LANES]

    def step(s, h):
        t = s if fwd else tt - 1 - s
        row = pl.multiple_of(t * nb, nb)
        h = a_scr[:, pl.ds(row, nb), :] * h + b_scr[:, pl.ds(row, nb), :]
        h_scr[:, pl.ds(row, nb), :] = h
        return h

    h_last = lax.fori_loop(0, tt, step, carry[...], unroll=8)
    carry[...] = h_last
    for lc in range(n_lc):
        hl_ref[:, lc * LANES:(lc + 1) * LANES] = h_last[lc]

    def unscan(bi):
        return jnp.concatenate([h_scr[lc, pl.ds(bi, tt, stride=nb), :] for lc in range(n_lc)], axis=1)

    if fwd:
        for bi in range(nb):
            hf_ref[bi] = unscan(bi)
        e = _with_halo(xq_ref, xqp_ref, xqn_ref, t_idx, n_tiles)
        length = tt + 2 * HALO
        a2 = e[:, 0:length - 1] + e[:, 1:length]
        a4 = a2[:, 0:length - 3] + a2[:, 2:length - 1]
        a8 = a4[:, 0:length - 7] + a4[:, 4:length - 3]
        a16 = a8[:, 0:length - 15] + a8[:, 8:length - 7]
        lane = lax.broadcasted_iota(jnp.int32, (1, 1, D_POOL), 2)
        wsum = jnp.where(lane < POOL_C, a2[:, HALO - 1:HALO - 1 + tt],
                         jnp.where(lane < 2 * POOL_C, a4[:, HALO - 2:HALO - 2 + tt],
                                   jnp.where(lane < 3 * POOL_C, a8[:, HALO - 4:HALO - 4 + tt],
                                             a16[:, 0:tt])))
        half = jnp.where(lane < POOL_C, POOL_WINDOWS[0] // 2,
                         jnp.where(lane < 2 * POOL_C, POOL_WINDOWS[1] // 2,
                                   jnp.where(lane < 3 * POOL_C, POOL_WINDOWS[2] // 2,
                                             POOL_WINDOWS[3] // 2)))
        tg = t_idx * tt + lax.broadcasted_iota(jnp.int32, (1, tt, D_POOL), 1)
        cnt = (jnp.minimum(tg + half, n_seq) - jnp.maximum(tg - half, 0)).astype(F32)
        dlt = (wsum / cnt - xq_ref[...]).astype(BF16).reshape(nb * tt, D_POOL)
        yp = _dot(dlt, pw_ref[...]) * ps_ref[...]
        yp_ref[...] = yp.reshape(nb, tt, D_POOL).astype(BF16)
    else:
        for bi in range(nb):
            hb = unscan(bi)
            yr_ref[bi] = ((hf_ref[bi] + hb) * _gelu(gate_ref[bi].astype(F32))).astype(BF16)


def _scan(fwd, xr, xq, conv_w, conv_b, w_gates, ba, bx, lam, h0, pool_w, pool_scale, h_f, gate, *, tt):
    n_batch, n_seq, _ = xr.shape
    n_bg = n_batch // SUBLANES
    n_tiles = n_seq // tt
    hb = tt // HALO
    n_hb = n_seq // HALO

    def t_of(i):
        return i if fwd else n_tiles - 1 - i

    def tile(c):
        return pl.BlockSpec((SUBLANES, tt, c), lambda b, i: (b, t_of(i), 0))

    def prev(c):
        return pl.BlockSpec((SUBLANES, 1, HALO, c),
                            lambda b, i: (b, jnp.maximum(t_of(i) * hb - 1, 0), 0, 0))

    def nxt(c):
        return pl.BlockSpec((SUBLANES, 1, HALO, c),
                            lambda b, i: (b, jnp.minimum((t_of(i) + 1) * hb, n_hb - 1), 0, 0))

    state = pl.BlockSpec((SUBLANES, D_RNN), lambda b, i: (b, 0))
    xr4 = xr.reshape(n_batch, n_hb, HALO, D_RNN)
    common = [_const_spec((CONV_W, D_RNN)), _const_spec((1, D_RNN)),
              _const_spec((2, RNN_HALF, D_RNN)), _const_spec((1, D_RNN)), _const_spec((1, D_RNN)),
              _const_spec((1, D_RNN)), state]
    common_args = [conv_w, conv_b, w_gates, ba, bx, lam, h0]
    n_lc = D_RNN // LANES
    scratch = ([pltpu.VMEM((n_lc, SUBLANES * tt, LANES), F32)] * 3
               + [pltpu.VMEM((n_lc, SUBLANES, LANES), F32)])
    if fwd:
        xq4 = xq.reshape(n_batch, n_hb, HALO, D_POOL)
        in_specs = ([tile(D_RNN), prev(D_RNN), nxt(D_RNN), tile(D_POOL), prev(D_POOL), nxt(D_POOL)]
                    + common + [_const_spec((D_POOL, D_POOL)), _const_spec((1, D_POOL))])
        args = [xr, xr4, xr4, xq, xq4, xq4] + common_args + [pool_w, pool_scale]
        out_specs = [tile(D_RNN), tile(D_POOL), state]
        out_shape = [jax.ShapeDtypeStruct((n_batch, n_seq, D_RNN), F32),
                     jax.ShapeDtypeStruct((n_batch, n_seq, D_POOL), BF16),
                     jax.ShapeDtypeStruct((n_batch, D_RNN), F32)]
    else:
        in_specs = [tile(D_RNN), prev(D_RNN), nxt(D_RNN)] + common + [tile(D_RNN), tile(D_RNN)]
        args = [xr, xr4, xr4] + common_args + [h_f, gate]
        out_specs = [tile(D_RNN), state]
        out_shape = [jax.ShapeDtypeStruct((n_batch, n_seq, D_RNN), BF16),
                     jax.ShapeDtypeStruct((n_batch, D_RNN), F32)]
    return pl.pallas_call(
        functools.partial(_scan_kernel, fwd=fwd, tt=tt, n_tiles=n_tiles, n_seq=n_seq),
        grid=(n_bg, n_tiles),
        in_specs=in_specs,
        out_specs=out_specs,
        out_shape=out_shape,
        scratch_shapes=scratch,
        compiler_params=_cparams(("arbitrary", "arbitrary")),
        name="rglru_forward_pool" if fwd else "rglru_backward_combine",
    )(*args)


def _gate_weights(wa, wx):
    per_half = RNN_BLOCKS // 2

    def half_dense(w, k):
        blocks = [w[k * per_half + b] for b in range(per_half)]
        rows = []
        for bi, blk in enumerate(blocks):
            row = [blk if bj == bi else jnp.zeros_like(blk) for bj in range(per_half)]
            rows.append(jnp.concatenate(row, axis=1))
        return jnp.concatenate(rows, axis=0)

    return jnp.stack([jnp.concatenate([half_dense(wa, k), half_dense(wx, k)], axis=1)
                      for k in range(2)]).astype(BF16)


def _pool_weights(pool_w):
    n = pool_w.shape[0]
    rows = []
    for i in range(n):
        rows.append(jnp.concatenate([pool_w[i] if j == i else jnp.zeros_like(pool_w[i])
                                     for j in range(n)], axis=1))
    return jnp.concatenate(rows, axis=0).astype(BF16)


def kernel(x_prompt, x_sample, cache_k, cache_v, state_h, c, c_ctx, w_ada, b_ada, norm_g, w_in_even,
           w_out_even, na_rpb, sgu_w, sgu_b, w_in_odd, w_out_odd, conv_w, conv_b, rg_wa, rg_ba, rg_wx,
           rg_bx, rg_lam, pool_w, pool_scale, ffn_wg, ffn_wu, ffn_wd, router_w, moe_wg, moe_wu, moe_wd):
    bp, n_p, _ = x_prompt.shape
    bs, n_s, _ = x_sample.shape
    n_ctx = cache_k.shape[3]

    cond = jnp.concatenate([c_ctx[None, :], c, jnp.zeros((2 * SUBLANES - 1 - bs, D_MODEL), F32)], axis=0)
    mods = _modulation(cond, w_ada, b_ada).reshape(w_ada.shape[0], cond.shape[0], 6, D_MODEL)

    streams = [
        dict(x=x_prompt.reshape(bp * n_p, D_MODEL), nb=bp, n=n_p, tm=256, latent=False),
        dict(x=x_sample.reshape(bs * n_s, D_MODEL), nb=bs, n=n_s, tm=512, latent=True),
    ]
    new_k = new_v = new_h = None

    g = norm_g[0]
    w_in = w_in_even[0].astype(BF16)
    w_out = w_out_even[0].astype(BF16)
    w_s = sgu_w[0].astype(BF16)
    b_s_t = sgu_b[0].T
    wg, wu, wd = (w[0:1].astype(BF16) for w in (ffn_wg, ffn_wu, ffn_wd))
    bias = _latent_bias_table(na_rpb[0], n_s // GRID_W)
    kc = cache_k[:, 0].transpose(0, 2, 1, 3).reshape(bs * n_ctx, W_A).astype(BF16)
    vc = cache_v[:, 0].transpose(0, 2, 1, 3).reshape(bs * n_ctx, W_A).astype(BF16)
    for st in streams:
        mod = mods[0, 1:1 + bs] if st["latent"] else mods[0, 0:1]
        outs = _l0_in(st["x"], mod, g, w_in, w_s, b_s_t, tm=st["tm"], rows_per_batch=st["n"],
                      emit_f32=not st["latent"])
        q, k, v, sg = outs[:4]
        if st["latent"]:
            oa = _attn_latent(q, k, v, kc, vc, bias, n_batch=st["nb"], n_seq=st["n"], n_ctx=n_ctx)
        else:
            oa = _attn_ctx(q, k, v, n_batch=st["nb"], n_seq=st["n"])
            heads = lambda a: a.reshape(bp, n_p, N_HEADS, HEAD_DIM).transpose(0, 2, 1, 3)[:, None]
            new_k, new_v = heads(outs[4]), heads(outs[5])
        x1, h2 = _out_proj([oa, sg], st["x"], mod, g, w_out, None, tm=st["tm"], rows_per_batch=st["n"])
        st["x"] = _ffn(h2, x1, mod, g, None, wg, wu, wd, tm=st["tm"], rows_per_batch=st["n"])

    g = norm_g[1]
    w_in = w_in_odd[0].astype(BF16)
    w_out = w_out_odd[0].astype(BF16)
    w_gates = [_gate_weights(rg_wa[0, d], rg_wx[0, d]) for d in range(2)]
    pw = _pool_weights(pool_w[0])
    ps = pool_scale[0][None, :]
    rw = jnp.pad(router_w[0].T, ((0, LANES - N_EXPERTS), (0, 0)))
    wg, wu, wd = (w[0].astype(BF16) for w in (moe_wg, moe_wu, moe_wd))
    for st in streams:
        mod = mods[1, 1:1 + bs] if st["latent"] else mods[1, 0:1]
        nb, n = st["nb"], st["n"]
        gate, xr, xq = _l1_in(st["x"], mod, g, w_in, tm=st["tm"], rows_per_batch=n)
        gate, xr, xq = (a.reshape(nb, n, a.shape[1]) for a in (gate, xr, xq))
        h0 = state_h[:, 0].astype(F32) if st["latent"] else jnp.zeros((nb, 2, D_RNN), F32)
        scan_args = lambda d: (conv_w[0], conv_b[0][None, :], w_gates[d], rg_ba[0, d][None, :],
                               rg_bx[0, d][None, :], rg_lam[0, d][None, :], h0[:, d])
        h_f, y_pool, h_f_last = _scan(True, xr, xq, *scan_args(0), pw, ps, None, None, tt=128)
        y_rec, h_b_last = _scan(False, xr, None, *scan_args(1), None, None, h_f, gate, tt=128)
        if not st["latent"]:
            new_h = jnp.stack([h_f_last, h_b_last], axis=1)[:, None]
        x1, h2, gates, rank_col, rank_row = _out_proj(
            [y_rec.reshape(nb * n, D_RNN), y_pool.reshape(nb * n, D_POOL)],
            st["x"], mod, g, w_out, rw, tm=st["tm"], rows_per_batch=n)
        pos_row, pos_col, tile_e, tile_lo, tile_n, c_t0, c_nt, n_tiles = _moe_plan(rank_row, rank_col)
        ys = _moe_ffn(h2, pos_row, tile_e, tile_lo, tile_n, wg, wu, wd, n_tiles=n_tiles)
        st["x"] = _moe_combine(ys, pos_col, gates, c_t0, c_nt, x1, mod, g, rows_per_batch=n)

    y_prompt = streams[0]["x"].reshape(bp, n_p, D_MODEL)
    y_sample = streams[1]["x"].reshape(bs, n_s, D_MODEL)
    return (y_prompt, y_sample, new_k, new_v, new_h)
```

```python
import functools

import jax
import jax.numpy as jnp
import numpy as np
from jax import lax
from jax.experimental import pallas as pl
from jax.experimental.pallas import tpu as pltpu

F32 = jnp.float32
BF16 = jnp.bfloat16

D_MODEL = 1024
GRID_W = 64
HEAD_DIM = 64
N_HEADS = 8
W_A = N_HEADS * HEAD_DIM
WIN_R = 8
WIN_C = 16
N_GROUPS_B = 4
W_B = 512
CHUNK = 128
D_RNN = 768
RNN_BLOCKS = 8
RNN_BW = D_RNN // RNN_BLOCKS
RNN_HALF = D_RNN // 2
CONV_W = 4
CONV_LEFT = 2
RG_C = 8.0
D_POOL = 256
POOL_WINDOWS = (2, 4, 8, 16)
POOL_C = D_POOL // len(POOL_WINDOWS)
D_FF = 2816
N_EXPERTS = 8
EPS = 1e-6
NEG = -1e30

LANES = 128
SUBLANES = 8
VMEM_LIMIT = 56 * 1024 * 1024
Q_ROWS = 4
KEY_ROWS = Q_ROWS + WIN_R - 1
HALO = 8
FF_CHUNKS = 2
FF_CHUNK = D_FF // FF_CHUNKS
MOE_SRC = 256
MOE_TILE = 256
MOE_RING = 8
MOE_RING_SIZES = (4, 5, 6, MOE_RING)
MOE_CTILE = 128
MOE_CMAX = (MOE_SRC - 1 + MOE_CTILE - 1) // MOE_CTILE + 1


def _cparams(sem):
    return pltpu.CompilerParams(dimension_semantics=sem, vmem_limit_bytes=VMEM_LIMIT)


def _const_spec(shape):
    nd = len(shape)
    return pl.BlockSpec(shape, lambda *_: (0,) * nd, pipeline_mode=pl.Buffered(1))


def _dot(a, b):
    return jnp.dot(a, b, preferred_element_type=F32)


def _dot_nt(a, b):
    return lax.dot_general(a, b, (((1,), (1,)), ((), ())), preferred_element_type=F32)


def _rms(x, g):
    ms = jnp.mean(x * x, axis=-1, keepdims=True)
    return x * lax.rsqrt(ms + EPS) * g


def _pre(x, g, shift, scale):
    return _rms(x, g) * (1.0 + scale) + shift


def _gelu(x):
    return jax.nn.gelu(x, approximate=True)


def _sigmoid(x):
    return 1.0 / (1.0 + jnp.exp(-x))


def _silu(x):
    return x * _sigmoid(x)


def _mod_kernel(c_ref, w_ref, b_ref, o_ref):
    s = _silu(c_ref[...])
    o_ref[0] = jnp.dot(s, w_ref[0], preferred_element_type=F32,
                       precision=lax.Precision.HIGHEST) + b_ref[0]


def _modulation(cond, w_ada, b_ada):
    depth, d, n = w_ada.shape
    rows = cond.shape[0]
    bn = 768
    return pl.pallas_call(
        _mod_kernel,
        grid=(depth, n // bn),
        in_specs=[
            pl.BlockSpec((rows, d), lambda i, j: (0, 0)),
            pl.BlockSpec((1, d, bn), lambda i, j: (i, 0, j)),
            pl.BlockSpec((1, 1, bn), lambda i, j: (i, 0, j)),
        ],
        out_specs=pl.BlockSpec((1, rows, bn), lambda i, j: (i, 0, j)),
        out_shape=jax.ShapeDtypeStruct((depth, rows, n), F32),
        compiler_params=_cparams(("arbitrary", "arbitrary")),
        name="adaln_modulation",
    )(cond, w_ada, b_ada.reshape(depth, 1, n))


def _l0_in_kernel(x_ref, mod_ref, g_ref, w_ref, ws_ref, bs_ref, *out_refs, tm, emit_f32):
    q_ref, k_ref, v_ref, sg_ref = out_refs[:4]
    x = x_ref[...]
    h = _pre(x, g_ref[0:1, :], mod_ref[0, 0:1, :], mod_ref[0, 1:2, :]).astype(BF16)
    z = _dot(h, w_ref[...])
    q_ref[...] = z[:, 0:W_A].astype(BF16)
    k_ref[...] = z[:, W_A:2 * W_A].astype(BF16)
    v_ref[...] = z[:, 2 * W_A:3 * W_A].astype(BF16)
    if emit_f32:
        out_refs[4][...] = z[:, W_A:2 * W_A]
        out_refs[5][...] = z[:, 2 * W_A:3 * W_A]
    u = _gelu(z[:, 3 * W_A:3 * W_A + W_B])
    gf = _gelu(z[:, 3 * W_A + W_B:])
    n_chunks = tm // CHUNK
    for gi in range(N_GROUPS_B):
        gg = gf[:, gi * LANES:(gi + 1) * LANES]
        mu = jnp.mean(gg, axis=-1, keepdims=True)
        dd = gg - mu
        var = jnp.mean(dd * dd, axis=-1, keepdims=True)
        gn = (dd * lax.rsqrt(var + EPS)).astype(BF16)
        rhs = jnp.concatenate([gn[c * CHUNK:(c + 1) * CHUNK, :] for c in range(n_chunks)], axis=1)
        s = _dot(ws_ref[gi], rhs) + bs_ref[:, gi:gi + 1]
        for c in range(n_chunks):
            uu = u[c * CHUNK:(c + 1) * CHUNK, gi * LANES:(gi + 1) * LANES]
            sg_ref[c * CHUNK:(c + 1) * CHUNK, gi * LANES:(gi + 1) * LANES] = (
                uu * s[:, c * LANES:(c + 1) * LANES]).astype(BF16)


def _l0_in(x, mod, g, w_in, w_s, b_s_t, *, tm, rows_per_batch, emit_f32):
    rows = x.shape[0]
    tpb = rows_per_batch // tm
    nb = mod.shape[0]
    d_in = w_in.shape[1]
    row_spec = lambda w: pl.BlockSpec((tm, w), lambda i: (i, 0))
    out_shape = [jax.ShapeDtypeStruct((rows, W_A), BF16)] * 3 + [jax.ShapeDtypeStruct((rows, W_B), BF16)]
    out_specs = [row_spec(W_A)] * 3 + [row_spec(W_B)]
    if emit_f32:
        out_shape += [jax.ShapeDtypeStruct((rows, W_A), F32)] * 2
        out_specs += [row_spec(W_A)] * 2
    return pl.pallas_call(
        functools.partial(_l0_in_kernel, tm=tm, emit_f32=emit_f32),
        grid=(rows // tm,),
        in_specs=[
            row_spec(D_MODEL),
            pl.BlockSpec((1, 6, D_MODEL), lambda i: ((i // tpb) % nb, 0, 0)),
            _const_spec((4, D_MODEL)),
            _const_spec((D_MODEL, d_in)),
            _const_spec((N_GROUPS_B, CHUNK, CHUNK)),
            _const_spec((CHUNK, N_GROUPS_B)),
        ],
        out_specs=out_specs,
        out_shape=out_shape,
        compiler_params=_cparams(("arbitrary",)),
        name="l0_in_proj_sgu",
    )(x, mod, g, w_in, w_s, b_s_t)


def _attn_kernel(*refs, n_local, has_ctx, rows):
    if has_ctx:
        q_ref, k_ref, v_ref, kc_ref, vc_ref, bias_ref, o_ref = refs
        r0 = pl.program_id(1) * Q_ROWS
        ks = jnp.clip(r0 - WIN_R // 2, 0, rows - KEY_ROWS)
        start = pl.multiple_of(ks * GRID_W, GRID_W)
        local = pl.ds(start, n_local)
    else:
        q_ref, k_ref, v_ref, o_ref = refs
        local = slice(None)
    first = lax.broadcasted_iota(jnp.int32, (1, LANES), 1) < HEAD_DIM
    for hp in range(W_A // LANES):
        lanes = slice(hp * LANES, (hp + 1) * LANES)
        kl = k_ref[local, lanes]
        vl = v_ref[local, lanes]
        q = q_ref[:, lanes] * jnp.asarray(HEAD_DIM ** -0.5, BF16)
        outs = []
        for half in range(2):
            qh = jnp.where(first if half == 0 else jnp.logical_not(first), q, jnp.zeros_like(q))
            s = _dot_nt(qh, kl)
            if has_ctx:
                s = s + bias_ref[0, 2 * hp + half]
                sc = _dot_nt(qh, kc_ref[:, lanes])
                m = jnp.maximum(jnp.max(s, axis=-1, keepdims=True), jnp.max(sc, axis=-1, keepdims=True))
            else:
                m = jnp.max(s, axis=-1, keepdims=True)
            p = jnp.exp(s - m)
            l = jnp.sum(p, axis=-1, keepdims=True)
            acc = _dot(p.astype(BF16), vl)
            if has_ctx:
                pc = jnp.exp(sc - m)
                l = l + jnp.sum(pc, axis=-1, keepdims=True)
                acc = acc + _dot(pc.astype(BF16), vc_ref[:, lanes])
            outs.append(acc / l)
        o_ref[:, lanes] = jnp.where(first, outs[0], outs[1]).astype(BF16)


def _attn_ctx(q, k, v, *, n_batch, n_seq):
    rows = q.shape[0]
    spec = pl.BlockSpec((n_seq, W_A), lambda b: (b, 0))
    return pl.pallas_call(
        functools.partial(_attn_kernel, n_local=n_seq, has_ctx=False, rows=0),
        grid=(n_batch,),
        in_specs=[spec, spec, spec],
        out_specs=spec,
        out_shape=jax.ShapeDtypeStruct((rows, W_A), BF16),
        compiler_params=_cparams(("arbitrary",)),
        name="attn_context",
    )(q, k, v)


def _attn_latent(q, k, v, kc, vc, bias, *, n_batch, n_seq, n_ctx):
    rows = n_seq // GRID_W
    n_rg = rows // Q_ROWS
    tq = Q_ROWS * GRID_W
    n_local = KEY_ROWS * GRID_W
    img_spec = pl.BlockSpec((n_seq, W_A), lambda b, rg: (b, 0))
    ctx_spec = pl.BlockSpec((n_ctx, W_A), lambda b, rg: (b, 0))
    q_spec = pl.BlockSpec((tq, W_A), lambda b, rg: (b * n_rg + rg, 0))

    def bias_map(b, rg):
        cfg = jnp.where(rg == 0, 0, jnp.where(rg == n_rg - 1, 2, 1))
        return (cfg, 0, 0, 0)

    return pl.pallas_call(
        functools.partial(_attn_kernel, n_local=n_local, has_ctx=True, rows=rows),
        grid=(n_batch, n_rg),
        in_specs=[q_spec, img_spec, img_spec, ctx_spec, ctx_spec,
                  pl.BlockSpec((1, N_HEADS, tq, n_local), bias_map)],
        out_specs=q_spec,
        out_shape=jax.ShapeDtypeStruct((n_batch * n_seq, W_A), BF16),
        compiler_params=_cparams(("arbitrary", "arbitrary")),
        name="attn_latent",
    )(q, k, v, kc, vc, bias)


def _latent_bias_table(rpb, rows):
    n_heads = rpb.shape[0]
    qcol = np.arange(GRID_W)[:, None]
    kcol = np.arange(GRID_W)[None, :]
    cs = np.clip(qcol - WIN_C // 2, 0, GRID_W - WIN_C)
    col_valid = (kcol >= cs) & (kcol < cs + WIN_C)
    dc = np.clip(kcol - qcol + (WIN_C - 1), 0, 2 * WIN_C - 2)
    pick = (dc[:, :, None] == np.arange(2 * WIN_C - 1)).astype(np.float32)
    col_tab = jnp.einsum("hrc,qkc->hrqk", rpb.astype(F32), pick, precision=lax.Precision.HIGHEST)
    col_tab = jnp.where(col_valid[None, None], col_tab, NEG)
    tables = []
    for r0 in (0, 2 * Q_ROWS, rows - Q_ROWS):
        ks = min(max(r0 - WIN_R // 2, 0), rows - KEY_ROWS)
        qrow = r0 + np.arange(Q_ROWS)[:, None]
        krow = ks + np.arange(KEY_ROWS)[None, :]
        rs = np.clip(qrow - WIN_R // 2, 0, rows - WIN_R)
        row_valid = (krow >= rs) & (krow < rs + WIN_R)
        dr = np.clip(krow - qrow + (WIN_R - 1), 0, 2 * WIN_R - 2)
        blocks = [jnp.stack([col_tab[:, dr[i, k]] if row_valid[i, k]
                             else jnp.full((n_heads, GRID_W, GRID_W), NEG, F32)
                             for k in range(KEY_ROWS)], axis=2)
                  for i in range(Q_ROWS)]
        b = jnp.stack(blocks, axis=1)
        tables.append(b.reshape(n_heads, Q_ROWS * GRID_W, KEY_ROWS * GRID_W))
    return jnp.stack(tables)


def _out_kernel(*refs, n_in):
    in_refs = refs[:n_in]
    x_ref, mod_ref, g_ref, w_ref, rw_ref, x1_ref, h2_ref, gates_ref, rankc_ref, rankr_ref = refs[n_in:]
    y = None
    off = 0
    for r in in_refs:
        w = r.shape[1]
        part = _dot(r[...], w_ref[off:off + w, :])
        y = part if y is None else y + part
        off += w
    x1 = x_ref[...] + mod_ref[0, 2:3, :] * _rms(y, g_ref[1:2, :])
    x1_ref[...] = x1
    h2 = _pre(x1, g_ref[2:3, :], mod_ref[0, 3:4, :], mod_ref[0, 4:5, :])
    h2_ref[...] = h2.astype(BF16)
    for sb in range(h2.shape[0] // MOE_SRC):
        blk = slice(sb * MOE_SRC, (sb + 1) * MOE_SRC)
        hs = h2[blk, :]
        h_hi = hs.astype(BF16)
        h_lo = (hs - h_hi.astype(F32)).astype(BF16)
        rw = rw_ref[...]
        r_hi = rw.astype(BF16)
        r_lo = (rw - r_hi.astype(F32)).astype(BF16)
        logits = _dot_nt(r_hi, h_hi) + (_dot_nt(r_hi, h_lo) + _dot_nt(r_lo, h_hi))
        tm = MOE_SRC
        row = lax.broadcasted_iota(jnp.int32, logits.shape, 0).astype(F32)
        logits = jnp.where(row < N_EXPERTS, logits, -jnp.inf)
        m1 = jnp.max(logits, axis=0, keepdims=True)
        i1 = jnp.min(jnp.where(logits == m1, row, float(LANES)), axis=0, keepdims=True)
        rest_l = jnp.where(row == i1, -jnp.inf, logits)
        m2 = jnp.max(rest_l, axis=0, keepdims=True)
        i2 = jnp.min(jnp.where(rest_l == m2, row, float(LANES)), axis=0, keepdims=True)
        e2 = jnp.exp(m2 - m1)
        w1 = 1.0 / (1.0 + e2)
        w2 = e2 / (1.0 + e2)
        sel1 = row == i1
        sel2 = row == i2
        member = jnp.where(sel1, 1.0, 0.0) + jnp.where(sel2, 1.0, 0.0)
        gates = jnp.where(sel1, w1, 0.0) + jnp.where(sel2, w2, 0.0)
        before = (lax.broadcasted_iota(jnp.int32, (tm, tm), 0)
                  < lax.broadcasted_iota(jnp.int32, (tm, tm), 1))
        rank = _dot(member.astype(BF16), jnp.where(before, 1.0, 0.0).astype(BF16))
        rank = jnp.where(member > 0.0, rank, -1.0)
        gates_ref[blk, :] = gates.T
        rankc_ref[blk, :] = rank.T.astype(jnp.int32)
        rankr_ref[:, blk] = rank[0:N_EXPERTS, :].astype(jnp.int32)


def _out_proj(ins, x, mod, g, w_out, router_w, *, tm, rows_per_batch):
    rows = x.shape[0]
    tpb = rows_per_batch // tm
    nb = mod.shape[0]
    row_spec = lambda w: pl.BlockSpec((tm, w), lambda i: (i, 0))
    in_specs = [row_spec(a.shape[1]) for a in ins] + [
        row_spec(D_MODEL),
        pl.BlockSpec((1, 6, D_MODEL), lambda i: ((i // tpb) % nb, 0, 0)),
        _const_spec((4, D_MODEL)),
        _const_spec((D_MODEL, D_MODEL)),
        _const_spec((LANES, D_MODEL)),
    ]
    out_shape = [jax.ShapeDtypeStruct((rows, D_MODEL), F32), jax.ShapeDtypeStruct((rows, D_MODEL), BF16),
                 jax.ShapeDtypeStruct((rows, LANES), F32),
                 jax.ShapeDtypeStruct((rows, LANES), jnp.int32),
                 jax.ShapeDtypeStruct((N_EXPERTS, rows), jnp.int32)]
    out_specs = [row_spec(D_MODEL), row_spec(D_MODEL), row_spec(LANES), row_spec(LANES),
                 pl.BlockSpec((N_EXPERTS, tm), lambda i: (0, i))]
    return pl.pallas_call(
        functools.partial(_out_kernel, n_in=len(ins)),
        grid=(rows // tm,),
        in_specs=in_specs,
        out_specs=out_specs,
        out_shape=out_shape,
        compiler_params=_cparams(("arbitrary",)),
        name="out_proj_norms_router",
    )(*ins, x, mod, g, w_out, router_w)


def _l0_tail_kernel(oa_ref, sg_ref, x_ref, mod_ref, g_ref, wo_ref, wg_ref, wu_ref, wd_ref,
                    mod1_ref, g1_ref, wi_ref, x2_ref, gate_ref, xr_ref, xq_ref, x1_scr, h2_scr, acc_ref):
    fc = pl.program_id(1)

    @pl.when(fc == 0)
    def _():
        y = _dot(oa_ref[...], wo_ref[0:W_A, :]) + _dot(sg_ref[...], wo_ref[W_A:, :])
        x1 = x_ref[...] + mod_ref[0, 2:3, :] * _rms(y, g_ref[1:2, :])
        x1_scr[...] = x1
        h2_scr[...] = _pre(x1, g_ref[2:3, :], mod_ref[0, 3:4, :], mod_ref[0, 4:5, :]).astype(BF16)
        acc_ref[...] = jnp.zeros_like(acc_ref)

    h = h2_scr[...]
    t = _silu(_dot(h, wg_ref[...])) * _dot(h, wu_ref[...])
    acc_ref[...] += _dot(t.astype(BF16), wd_ref[...])

    @pl.when(fc == FF_CHUNKS - 1)
    def _():
        x2 = x1_scr[...] + mod_ref[0, 5:6, :] * _rms(acc_ref[...], g_ref[3:4, :])
        x2_ref[...] = x2
        hn = _pre(x2, g1_ref[0:1, :], mod1_ref[0, 0:1, :], mod1_ref[0, 1:2, :]).astype(BF16)
        z = _dot(hn, wi_ref[...])
        gate_ref[...] = z[:, 0:D_RNN].astype(BF16)
        xr_ref[...] = z[:, D_RNN:2 * D_RNN]
        xq_ref[...] = z[:, 2 * D_RNN:]


def _l0_tail(oa, sg, x, mod, g, w_out, wg, wu, wd, mod1, g1, w_in1, *, tm, rows_per_batch):
    rows = x.shape[0]
    tpb = rows_per_batch // tm
    nb = mod.shape[0]
    row_spec = lambda w: pl.BlockSpec((tm, w), lambda i, f: (i, 0))
    mod_spec = pl.BlockSpec((1, 6, D_MODEL), lambda i, f: ((i // tpb) % nb, 0, 0))
    return pl.pallas_call(
        _l0_tail_kernel,
        grid=(rows // tm, FF_CHUNKS),
        in_specs=[row_spec(W_A), row_spec(W_B), row_spec(D_MODEL), mod_spec,
                  _const_spec((4, D_MODEL)), _const_spec((W_A + W_B, D_MODEL)),
                  pl.BlockSpec((D_MODEL, FF_CHUNK), lambda i, f: (0, f)),
                  pl.BlockSpec((D_MODEL, FF_CHUNK), lambda i, f: (0, f)),
                  pl.BlockSpec((FF_CHUNK, D_MODEL), lambda i, f: (f, 0)),
                  mod_spec, _const_spec((4, D_MODEL)), _const_spec((D_MODEL, w_in1.shape[1]))],
        out_specs=[row_spec(D_MODEL), row_spec(D_RNN), row_spec(D_RNN), row_spec(D_POOL)],
        out_shape=[jax.ShapeDtypeStruct((rows, D_MODEL), F32),
                   jax.ShapeDtypeStruct((rows, D_RNN), BF16),
                   jax.ShapeDtypeStruct((rows, D_RNN), F32),
                   jax.ShapeDtypeStruct((rows, D_POOL), F32)],
        scratch_shapes=[pltpu.VMEM((tm, D_MODEL), F32), pltpu.VMEM((tm, D_MODEL), BF16),
                        pltpu.VMEM((tm, D_MODEL), F32)],
        compiler_params=_cparams(("arbitrary", "arbitrary")),
        name="l0_tail_l1_in_proj",
    )(oa, sg, x, mod, g, w_out, wg, wu, wd, mod1, g1, w_in1)


def _moe_plan(rank_row, rank_col):
    n_tok = rank_row.shape[1]
    n_src = n_tok // MOE_SRC
    n_tiles = 2 * n_tok // MOE_TILE + N_EXPERTS
    member = rank_row >= 0
    cnt_blk = member.reshape(N_EXPERTS, n_src, MOE_SRC).sum(-1).astype(jnp.int32)
    cum = jnp.concatenate([jnp.zeros((N_EXPERTS, 1), jnp.int32), jnp.cumsum(cnt_blk, axis=1)], axis=1)
    cnt = cum[:, -1]
    tiles_e = (cnt + MOE_TILE - 1) // MOE_TILE
    tile_end = jnp.cumsum(tiles_e)
    tile_base = tile_end - tiles_e
    slot0 = tile_base[:, None] * MOE_TILE + cum[:, :-1]
    pos_row = jnp.where(member, jnp.repeat(slot0, MOE_SRC, axis=1) + rank_row, -1)
    pos_row = jnp.pad(pos_row, ((0, 0), (0, MOE_RING * MOE_SRC)), constant_values=-1)
    slot0_col = jnp.pad(slot0.T, ((0, 0), (0, LANES - N_EXPERTS)))
    pos_col = jnp.where(rank_col >= 0, jnp.repeat(slot0_col, MOE_SRC, axis=0) + rank_col, -1)
    d = jnp.arange(n_tiles, dtype=jnp.int32)
    tile_e = jnp.minimum(jnp.sum(d[:, None] >= tile_end[None, :], axis=1), N_EXPERTS - 1).astype(jnp.int32)
    valid = d < tile_end[-1]
    lo_slot = (d - tile_base[tile_e]) * MOE_TILE
    hi_slot = jnp.minimum(lo_slot + MOE_TILE, cnt[tile_e])
    cum_d = cum[tile_e]
    tile_lo = jnp.sum(cum_d[:, 1:] <= lo_slot[:, None], axis=1).astype(jnp.int32)
    tile_hi = jnp.sum(cum_d[:, :-1] < hi_slot[:, None], axis=1).astype(jnp.int32) - 1
    tile_n = jnp.where(valid, tile_hi - tile_lo + 1, 0).astype(jnp.int32)
    tile_lo = jnp.where(valid, tile_lo, 0)
    first = slot0.T
    c_t0 = first // MOE_CTILE
    c_nt = jnp.where(cnt_blk.T > 0, (first + cnt_blk.T - 1) // MOE_CTILE - c_t0 + 1, 0)
    return (pos_row, pos_col, tile_e, tile_lo, tile_n,
            c_t0.reshape(-1).astype(jnp.int32), c_nt.reshape(-1).astype(jnp.int32), n_tiles)


def _moe_ffn_kernel(te_ref, tl_ref, tn_ref, pos_ref, h_hbm, wg_ref, wu_ref, wd_ref, o_ref,
                    hbuf, sem, acc_ref, *, n_tiles):
    d = pl.program_id(0)
    e = te_ref[d]
    lo = tl_ref[d]
    n = tn_ref[d]

    def copy(blk, slot):
        start = pl.multiple_of(blk * MOE_SRC, MOE_SRC)
        return pltpu.make_async_copy(h_hbm.at[pl.ds(start, MOE_SRC), :], hbuf.at[slot], sem.at[slot])

    def start_first(first_blk, count):
        for k in range(MOE_RING):
            @pl.when(k < count)
            def _(k=k):
                copy(first_blk + k, k).start()

    @pl.when(d == 0)
    def _():
        hbuf[...] = jnp.zeros_like(hbuf)
        start_first(lo, n)

    slot_ids = d * MOE_TILE + lax.broadcasted_iota(jnp.int32, (MOE_TILE, 1), 0)

    def picks(first_blk, n_blk):
        start = pl.multiple_of(first_blk * MOE_SRC, MOE_SRC)
        pos = pos_ref[pl.ds(e, 1), pl.ds(start, n_blk * MOE_SRC)]
        return jnp.where(pos == slot_ids, 1.0, 0.0).astype(BF16)

    for k in range(MOE_RING):
        @pl.when(k < n)
        def _(k=k):
            copy(lo + k, k).wait()

    prev = 0
    for n_blk in MOE_RING_SIZES:
        last = n_blk == MOE_RING_SIZES[-1]

        @pl.when((n > prev) if last else ((n > prev) & (n <= n_blk)))
        def _(n_blk=n_blk):
            acc_ref[...] = _dot(picks(lo, n_blk), hbuf[0:n_blk].reshape(n_blk * MOE_SRC, D_MODEL))

        prev = n_blk

    def extra(j, carry):
        cp = copy(lo + j, 0)
        cp.start()
        cp.wait()
        acc_ref[...] += _dot(picks(lo + j, 1), hbuf[0])
        return carry

    lax.fori_loop(MOE_RING, n, extra, 0)

    @pl.when(d + 1 < n_tiles)
    def _():
        start_first(tl_ref[d + 1], tn_ref[d + 1])

    @pl.when(n > 0)
    def _():
        x = acc_ref[...].astype(BF16)
        y = None
        for fc in range(FF_CHUNKS):
            cols = slice(fc * FF_CHUNK, (fc + 1) * FF_CHUNK)
            t = _silu(_dot(x, wg_ref[0, :, cols])) * _dot(x, wu_ref[0, :, cols])
            part = _dot(t.astype(BF16), wd_ref[0, cols, :])
            y = part if y is None else y + part
        o_ref[...] = y.astype(BF16)

    @pl.when(n == 0)
    def _():
        o_ref[...] = jnp.zeros_like(o_ref)


def _moe_ffn(h2, pos_row, tile_e, tile_lo, tile_n, wg, wu, wd, *, n_tiles):
    n_tok = h2.shape[0]
    w_spec = lambda shape: pl.BlockSpec((1,) + shape, lambda d, te, tl, tn: (te[d], 0, 0))
    grid_spec = pltpu.PrefetchScalarGridSpec(
        num_scalar_prefetch=3,
        grid=(n_tiles,),
        in_specs=[
            pl.BlockSpec(pos_row.shape, lambda d, te, tl, tn: (0, 0), pipeline_mode=pl.Buffered(1)),
            pl.BlockSpec(memory_space=pl.ANY),
            w_spec((D_MODEL, D_FF)), w_spec((D_MODEL, D_FF)), w_spec((D_FF, D_MODEL)),
        ],
        out_specs=pl.BlockSpec((MOE_TILE, D_MODEL), lambda d, te, tl, tn: (d, 0)),
        scratch_shapes=[pltpu.VMEM((MOE_RING, MOE_SRC, D_MODEL), BF16),
                        pltpu.SemaphoreType.DMA((MOE_RING,)),
                        pltpu.VMEM((MOE_TILE, D_MODEL), F32)],
    )
    return pl.pallas_call(
        functools.partial(_moe_ffn_kernel, n_tiles=n_tiles),
        grid_spec=grid_spec,
        out_shape=jax.ShapeDtypeStruct((n_tiles * MOE_TILE, D_MODEL), BF16),
        compiler_params=_cparams(("arbitrary",)),
        name="moe_dispatch_swiglu",
    )(tile_e, tile_lo, tile_n, pos_row, h2, wg, wu, wd)


def _moe_combine_kernel(t0_ref, nt_ref, pos_ref, gates_ref, ys_hbm, x1_ref, mod_ref, g_ref, o_ref,
                        ybuf, sem):
    s = pl.program_id(0)

    def copy(blk, e, k):
        start = pl.multiple_of((t0_ref[blk * N_EXPERTS + e] + k) * MOE_CTILE, MOE_CTILE)
        buf = (blk % 2) * N_EXPERTS + e
        return pltpu.make_async_copy(ys_hbm.at[pl.ds(start, MOE_CTILE), :],
                                     ybuf.at[buf, pl.ds(k * MOE_CTILE, MOE_CTILE), :],
                                     sem.at[buf * MOE_CMAX + k])

    def start_block(blk):
        for e in range(N_EXPERTS):
            for k in range(MOE_CMAX):
                @pl.when(nt_ref[blk * N_EXPERTS + e] > k)
                def _(e=e, k=k):
                    copy(blk, e, k).start()

    @pl.when(s == 0)
    def _():
        ybuf[...] = jnp.zeros_like(ybuf)
        start_block(s)

    @pl.when(s + 1 < pl.num_programs(0))
    def _():
        start_block(s + 1)

    for e in range(N_EXPERTS):
        for k in range(MOE_CMAX):
            @pl.when(nt_ref[s * N_EXPERTS + e] > k)
            def _(e=e, k=k):
                copy(s, e, k).wait()

    lane = lax.broadcasted_iota(jnp.int32, (1, MOE_CMAX * MOE_CTILE), 1)
    y = None
    for e in range(N_EXPERTS):
        base = t0_ref[s * N_EXPERTS + e] * MOE_CTILE
        pick = jnp.where(pos_ref[:, e:e + 1] == base + lane, 1.0, 0.0).astype(BF16)
        part = gates_ref[:, e:e + 1] * _dot(pick, ybuf[(s % 2) * N_EXPERTS + e])
        y = part if y is None else y + part

    o_ref[...] = x1_ref[...] + mod_ref[0, 5:6, :] * _rms(y, g_ref[3:4, :])


def _moe_combine(ys, pos_col, gates, c_t0, c_nt, x1, mod, g, *, rows_per_batch):
    n_tok = x1.shape[0]
    tpb = rows_per_batch // MOE_SRC
    nb = mod.shape[0]
    row_spec = lambda w: pl.BlockSpec((MOE_SRC, w), lambda s, t0, nt: (s, 0))
    grid_spec = pltpu.PrefetchScalarGridSpec(
        num_scalar_prefetch=2,
        grid=(n_tok // MOE_SRC,),
        in_specs=[
            row_spec(LANES), row_spec(LANES),
            pl.BlockSpec(memory_space=pl.ANY),
            row_spec(D_MODEL),
            pl.BlockSpec((1, 6, D_MODEL), lambda s, t0, nt: ((s // tpb) % nb, 0, 0)),
            pl.BlockSpec((4, D_MODEL), lambda s, t0, nt: (0, 0), pipeline_mode=pl.Buffered(1)),
        ],
        out_specs=row_spec(D_MODEL),
        scratch_shapes=[pltpu.VMEM((2 * N_EXPERTS, MOE_CMAX * MOE_CTILE, D_MODEL), BF16),
                        pltpu.SemaphoreType.DMA((2 * N_EXPERTS * MOE_CMAX,))],
    )
    return pl.pallas_call(
        _moe_combine_kernel,
        grid_spec=grid_spec,
        out_shape=jax.ShapeDtypeStruct((n_tok, D_MODEL), F32),
        compiler_params=_cparams(("arbitrary",)),
        name="moe_combine_post",
    )(c_t0, c_nt, pos_col, gates, ys, x1, mod, g)


def _with_halo(x_ref, prev_ref, next_ref, t_idx, n_tiles):
    x = x_ref[...]
    prev = jnp.where(t_idx > 0, prev_ref[:, 0], 0.0)
    nxt = jnp.where(t_idx < n_tiles - 1, next_ref[:, 0], 0.0)
    return jnp.concatenate([prev, x, nxt], axis=1)


def _scan_kernel(*refs, fwd, tt, n_tiles, n_seq):
    if fwd:
        (xr_ref, xrp_ref, xrn_ref, xq_ref, xqp_ref, xqn_ref, cw_ref, cb_ref, wgt_ref, ba_ref, bx_ref,
         lam_ref, h0_ref, pw_ref, ps_ref, hf_ref, yp_ref, hl_ref, a_scr, b_scr, h_scr, carry) = refs
    else:
        (xr_ref, xrp_ref, xrn_ref, cw_ref, cb_ref, wgt_ref, ba_ref, bx_ref, lam_ref, h0_ref,
         hf_ref, gate_ref, yr_ref, hl_ref, a_scr, b_scr, h_scr, carry) = refs
    i = pl.program_id(1)
    t_idx = i if fwd else n_tiles - 1 - i
    nb = SUBLANES

    ext = _with_halo(xr_ref, xrp_ref, xrn_ref, t_idx, n_tiles)
    xc = cb_ref[...][None]
    for j in range(CONV_W):
        lo = HALO + j - CONV_LEFT
        xc = xc + cw_ref[j:j + 1, :][None] * ext[:, lo:lo + tt, :]
    xc = xc.reshape(nb * tt, D_RNN)
    xb = xc.astype(BF16)
    r0 = _dot(xb[:, :RNN_HALF], wgt_ref[0])
    r1 = _dot(xb[:, RNN_HALF:], wgt_ref[1])
    ra = jnp.concatenate([r0[:, :RNN_HALF], r1[:, :RNN_HALF]], axis=1) + ba_ref[...]
    ri = jnp.concatenate([r0[:, RNN_HALF:], r1[:, RNN_HALF:]], axis=1) + bx_ref[...]
    nl = -lam_ref[...]
    softplus = jnp.maximum(nl, 0.0) + jnp.log1p(jnp.exp(-jnp.abs(nl)))
    a = jnp.exp((-0.5 * RG_C * softplus) * (1.0 + jnp.tanh(0.5 * ra)))
    b = jnp.sqrt(1.0 - a * a) * ((0.5 + 0.5 * jnp.tanh(0.5 * ri)) * xc)
    n_lc = D_RNN // LANES
    for bi in range(nb):
        for lc in range(n_lc):
            a_scr[lc, pl.ds(bi, tt, stride=nb), :] = a[bi * tt:(bi + 1) * tt, lc * LANES:(lc + 1) * LANES]
            b_scr[lc, pl.ds(bi, tt, stride=nb), :] = b[bi * tt:(bi + 1) * tt, lc * LANES:(lc + 1) * LANES]

    @pl.when(i == 0)
    def _():
        for lc in range(n_lc):
            carry[lc] = h0_ref[:, lc * LANES:(lc + 1) * LANES]

    def step(s, h):
        t = s if fwd else tt - 1 - s
        row = pl.multiple_of(t * nb, nb)
        h = a_scr[:, pl.ds(row, nb), :] * h + b_scr[:, pl.ds(row, nb), :]
        h_scr[:, pl.ds(row, nb), :] = h
        return h

    h_last = lax.fori_loop(0, tt, step, carry[...], unroll=8)
    carry[...] = h_last
    for lc in range(n_lc):
        hl_ref[:, lc * LANES:(lc + 1) * LANES] = h_last[lc]

    def unscan(bi):
        return jnp.concatenate([h_scr[lc, pl.ds(bi, tt, stride=nb), :] for lc in range(n_lc)], axis=1)

    if fwd:
        for bi in range(nb):
            hf_ref[bi] = unscan(bi)
        e = _with_halo(xq_ref, xqp_ref, xqn_ref, t_idx, n_tiles)
        length = tt + 2 * HALO
        a2 = e[:, 0:length - 1] + e[:, 1:length]
        a4 = a2[:, 0:length - 3] + a2[:, 2:length - 1]
        a8 = a4[:, 0:length - 7] + a4[:, 4:length - 3]
        a16 = a8[:, 0:length - 15] + a8[:, 8:length - 7]
        lane = lax.broadcasted_iota(jnp.int32, (1, 1, D_POOL), 2)
        wsum = jnp.where(lane < POOL_C, a2[:, HALO - 1:HALO - 1 + tt],
                         jnp.where(lane < 2 * POOL_C, a4[:, HALO - 2:HALO - 2 + tt],
                                   jnp.where(lane < 3 * POOL_C, a8[:, HALO - 4:HALO - 4 + tt],
                                             a16[:, 0:tt])))
        half = jnp.where(lane < POOL_C, POOL_WINDOWS[0] // 2,
                         jnp.where(lane < 2 * POOL_C, POOL_WINDOWS[1] // 2,
                                   jnp.where(lane < 3 * POOL_C, POOL_WINDOWS[2] // 2,
                                             POOL_WINDOWS[3] // 2)))
        tg = t_idx * tt + lax.broadcasted_iota(jnp.int32, (1, tt, D_POOL), 1)
        cnt = (jnp.minimum(tg + half, n_seq) - jnp.maximum(tg - half, 0)).astype(F32)
        dlt = (wsum / cnt - xq_ref[...]).astype(BF16).reshape(nb * tt, D_POOL)
        yp = _dot(dlt, pw_ref[...]) * ps_ref[...]
        yp_ref[...] = yp.reshape(nb, tt, D_POOL).astype(BF16)
    else:
        for bi in range(nb):
            hb = unscan(bi)
            yr_ref[bi] = ((hf_ref[bi] + hb) * _gelu(gate_ref[bi].astype(F32))).astype(BF16)


def _scan(fwd, xr, xq, conv_w, conv_b, w_gates, ba, bx, lam, h0, pool_w, pool_scale, h_f, gate, *, tt):
    n_batch, n_seq, _ = xr.shape
    n_bg = n_batch // SUBLANES
    n_tiles = n_seq // tt
    hb = tt // HALO
    n_hb = n_seq // HALO

    def t_of(i):
        return i if fwd else n_tiles - 1 - i

    def tile(c):
        return pl.BlockSpec((SUBLANES, tt, c), lambda b, i: (b, t_of(i), 0))

    def prev(c):
        return pl.BlockSpec((SUBLANES, 1, HALO, c),
                            lambda b, i: (b, jnp.maximum(t_of(i) * hb - 1, 0), 0, 0))

    def nxt(c):
        return pl.BlockSpec((SUBLANES, 1, HALO, c),
                            lambda b, i: (b, jnp.minimum((t_of(i) + 1) * hb, n_hb - 1), 0, 0))

    state = pl.BlockSpec((SUBLANES, D_RNN), lambda b, i: (b, 0))
    xr4 = xr.reshape(n_batch, n_hb, HALO, D_RNN)
    common = [_const_spec((CONV_W, D_RNN)), _const_spec((1, D_RNN)),
              _const_spec((2, RNN_HALF, D_RNN)), _const_spec((1, D_RNN)), _const_spec((1, D_RNN)),
              _const_spec((1, D_RNN)), state]
    common_args = [conv_w, conv_b, w_gates, ba, bx, lam, h0]
    n_lc = D_RNN // LANES
    scratch = ([pltpu.VMEM((n_lc, SUBLANES * tt, LANES), F32)] * 3
               + [pltpu.VMEM((n_lc, SUBLANES, LANES), F32)])
    if fwd:
        xq4 = xq.reshape(n_batch, n_hb, HALO, D_POOL)
        in_specs = ([tile(D_RNN), prev(D_RNN), nxt(D_RNN), tile(D_POOL), prev(D_POOL), nxt(D_POOL)]
                    + common + [_const_spec((D_POOL, D_POOL)), _const_spec((1, D_POOL))])
        args = [xr, xr4, xr4, xq, xq4, xq4] + common_args + [pool_w, pool_scale]
        out_specs = [tile(D_RNN), tile(D_POOL), state]
        out_shape = [jax.ShapeDtypeStruct((n_batch, n_seq, D_RNN), F32),
                     jax.ShapeDtypeStruct((n_batch, n_seq, D_POOL), BF16),
                     jax.ShapeDtypeStruct((n_batch, D_RNN), F32)]
    else:
        in_specs = [tile(D_RNN), prev(D_RNN), nxt(D_RNN)] + common + [tile(D_RNN), tile(D_RNN)]
        args = [xr, xr4, xr4] + common_args + [h_f, gate]
        out_specs = [tile(D_RNN), state]
        out_shape = [jax.ShapeDtypeStruct((n_batch, n_seq, D_RNN), BF16),
                     jax.ShapeDtypeStruct((n_batch, D_RNN), F32)]
    return pl.pallas_call(
        functools.partial(_scan_kernel, fwd=fwd, tt=tt, n_tiles=n_tiles, n_seq=n_seq),
        grid=(n_bg, n_tiles),
        in_specs=in_specs,
        out_specs=out_specs,
        out_shape=out_shape,
        scratch_shapes=scratch,
        compiler_params=_cparams(("arbitrary", "arbitrary")),
        name="rglru_forward_pool" if fwd else "rglru_backward_combine",
    )(*args)


def _gate_weights(wa, wx):
    per_half = RNN_BLOCKS // 2

    def half_dense(w, k):
        blocks = [w[k * per_half + b] for b in range(per_half)]
        rows = []
        for bi, blk in enumerate(blocks):
            row = [blk if bj == bi else jnp.zeros_like(blk) for bj in range(per_half)]
            rows.append(jnp.concatenate(row, axis=1))
        return jnp.concatenate(rows, axis=0)

    return jnp.stack([jnp.concatenate([half_dense(wa, k), half_dense(wx, k)], axis=1)
                      for k in range(2)]).astype(BF16)


def _pool_weights(pool_w):
    n = pool_w.shape[0]
    rows = []
    for i in range(n):
        rows.append(jnp.concatenate([pool_w[i] if j == i else jnp.zeros_like(pool_w[i])
                                     for j in range(n)], axis=1))
    return jnp.concatenate(rows, axis=0).astype(BF16)


def kernel(x_prompt, x_sample, cache_k, cache_v, state_h, c, c_ctx, w_ada, b_ada, norm_g, w_in_even,
           w_out_even, na_rpb, sgu_w, sgu_b, w_in_odd, w_out_odd, conv_w, conv_b, rg_wa, rg_ba, rg_wx,
           rg_bx, rg_lam, pool_w, pool_scale, ffn_wg, ffn_wu, ffn_wd, router_w, moe_wg, moe_wu, moe_wd):
    bp, n_p, _ = x_prompt.shape
    bs, n_s, _ = x_sample.shape
    n_ctx = cache_k.shape[3]

    cond = jnp.concatenate([c_ctx[None, :], c, jnp.zeros((2 * SUBLANES - 1 - bs, D_MODEL), F32)], axis=0)
    mods = _modulation(cond, w_ada, b_ada).reshape(w_ada.shape[0], cond.shape[0], 6, D_MODEL)

    streams = [
        dict(x=x_prompt.reshape(bp * n_p, D_MODEL), nb=bp, n=n_p, tm=256, latent=False),
        dict(x=x_sample.reshape(bs * n_s, D_MODEL), nb=bs, n=n_s, tm=512, latent=True),
    ]
    new_k = new_v = new_h = None

    g = norm_g[0]
    w_in = w_in_even[0].astype(BF16)
    w_out = w_out_even[0].astype(BF16)
    w_s = sgu_w[0].astype(BF16)
    b_s_t = sgu_b[0].T
    wg, wu, wd = (w[0].astype(BF16) for w in (ffn_wg, ffn_wu, ffn_wd))
    w_in1 = w_in_odd[0].astype(BF16)
    bias = _latent_bias_table(na_rpb[0], n_s // GRID_W)
    kc = cache_k[:, 0].transpose(0, 2, 1, 3).reshape(bs * n_ctx, W_A).astype(BF16)
    vc = cache_v[:, 0].transpose(0, 2, 1, 3).reshape(bs * n_ctx, W_A).astype(BF16)
    for st in streams:
        mod = mods[0, 1:1 + bs] if st["latent"] else mods[0, 0:1]
        outs = _l0_in(st["x"], mod, g, w_in, w_s, b_s_t, tm=st["tm"], rows_per_batch=st["n"],
                      emit_f32=not st["latent"])
        q, k, v, sg = outs[:4]
        if st["latent"]:
            oa = _attn_latent(q, k, v, kc, vc, bias, n_batch=st["nb"], n_seq=st["n"], n_ctx=n_ctx)
        else:
            oa = _attn_ctx(q, k, v, n_batch=st["nb"], n_seq=st["n"])
            heads = lambda a: a.reshape(bp, n_p, N_HEADS, HEAD_DIM).transpose(0, 2, 1, 3)[:, None]
            new_k, new_v = heads(outs[4]), heads(outs[5])
        mod1 = mods[1, 1:1 + bs] if st["latent"] else mods[1, 0:1]
        st["x"], st["gate"], st["xr"], st["xq"] = _l0_tail(
            oa, sg, st["x"], mod, g, w_out, wg, wu, wd, mod1, norm_g[1], w_in1,
            tm=st["tm"], rows_per_batch=st["n"])

    g = norm_g[1]
    w_out = w_out_odd[0].astype(BF16)
    w_gates = [_gate_weights(rg_wa[0, d], rg_wx[0, d]) for d in range(2)]
    pw = _pool_weights(pool_w[0])
    ps = pool_scale[0][None, :]
    rw = jnp.pad(router_w[0].T, ((0, LANES - N_EXPERTS), (0, 0)))
    wg, wu, wd = (w[0].astype(BF16) for w in (moe_wg, moe_wu, moe_wd))
    for st in streams:
        mod = mods[1, 1:1 + bs] if st["latent"] else mods[1, 0:1]
        nb, n = st["nb"], st["n"]
        gate, xr, xq = (st[key].reshape(nb, n, -1) for key in ("gate", "xr", "xq"))
        h0 = state_h[:, 0].astype(F32) if st["latent"] else jnp.zeros((nb, 2, D_RNN), F32)
        scan_args = lambda d: (conv_w[0], conv_b[0][None, :], w_gates[d], rg_ba[0, d][None, :],
                               rg_bx[0, d][None, :], rg_lam[0, d][None, :], h0[:, d])
        h_f, y_pool, h_f_last = _scan(True, xr, xq, *scan_args(0), pw, ps, None, None, tt=128)
        y_rec, h_b_last = _scan(False, xr, None, *scan_args(1), None, None, h_f, gate, tt=128)
        if not st["latent"]:
            new_h = jnp.stack([h_f_last, h_b_last], axis=1)[:, None]
        x1, h2, gates, rank_col, rank_row = _out_proj(
            [y_rec.reshape(nb * n, D_RNN), y_pool.reshape(nb * n, D_POOL)],
            st["x"], mod, g, w_out, rw, tm=st["tm"], rows_per_batch=n)
        pos_row, pos_col, tile_e, tile_lo, tile_n, c_t0, c_nt, n_tiles = _moe_plan(rank_row, rank_col)
        ys = _moe_ffn(h2, pos_row, tile_e, tile_lo, tile_n, wg, wu, wd, n_tiles=n_tiles)
        st["x"] = _moe_combine(ys, pos_col, gates, c_t0, c_nt, x1, mod, g, rows_per_batch=n)

    y_prompt = streams[0]["x"].reshape(bp, n_p, D_MODEL)
    y_sample = streams[1]["x"].reshape(bs, n_s, D_MODEL)
    return (y_prompt, y_sample, new_k, new_v, new_h)
```

```python
import functools

import jax
import jax.numpy as jnp
import numpy as np
from jax import lax
from jax.experimental import pallas as pl
from jax.experimental.pallas import tpu as pltpu

F32 = jnp.float32
BF16 = jnp.bfloat16

D_MODEL = 1024
GRID_W = 64
HEAD_DIM = 64
N_HEADS = 8
W_A = N_HEADS * HEAD_DIM
WIN_R = 8
WIN_C = 16
N_GROUPS_B = 4
W_B = 512
CHUNK = 128
D_RNN = 768
RNN_BLOCKS = 8
RNN_BW = D_RNN // RNN_BLOCKS
RNN_HALF = D_RNN // 2
CONV_W = 4
CONV_LEFT = 2
RG_C = 8.0
D_POOL = 256
POOL_WINDOWS = (2, 4, 8, 16)
POOL_C = D_POOL // len(POOL_WINDOWS)
D_FF = 2816
N_EXPERTS = 8
EPS = 1e-6
NEG = -1e30

LANES = 128
SUBLANES = 8
VMEM_LIMIT = 56 * 1024 * 1024
Q_ROWS = 4
KEY_ROWS = Q_ROWS + WIN_R - 1
HALO = 8
FF_CHUNKS = 2
FF_CHUNK = D_FF // FF_CHUNKS
MOE_SRC = 256
MOE_TILE = 256
MOE_RING = 8
MOE_RING_SIZES = (4, 5, 6, MOE_RING)
MOE_CTILE = 128
MOE_CMAX = (MOE_SRC - 1 + MOE_CTILE - 1) // MOE_CTILE + 1


def _cparams(sem):
    return pltpu.CompilerParams(dimension_semantics=sem, vmem_limit_bytes=VMEM_LIMIT)


def _const_spec(shape):
    nd = len(shape)
    return pl.BlockSpec(shape, lambda *_: (0,) * nd, pipeline_mode=pl.Buffered(1))


def _dot(a, b):
    return jnp.dot(a, b, preferred_element_type=F32)


def _dot_nt(a, b):
    return lax.dot_general(a, b, (((1,), (1,)), ((), ())), preferred_element_type=F32)


def _rms(x, g):
    ms = jnp.mean(x * x, axis=-1, keepdims=True)
    return x * lax.rsqrt(ms + EPS) * g


def _pre(x, g, shift, scale):
    return _rms(x, g) * (1.0 + scale) + shift


def _gelu(x):
    return jax.nn.gelu(x, approximate=True)


def _sigmoid(x):
    return 1.0 / (1.0 + jnp.exp(-x))


def _silu(x):
    return x * _sigmoid(x)


def _mod_kernel(c_ref, w_ref, b_ref, o_ref):
    s = _silu(c_ref[...])
    o_ref[0] = jnp.dot(s, w_ref[0], preferred_element_type=F32,
                       precision=lax.Precision.HIGHEST) + b_ref[0]


def _modulation(cond, w_ada, b_ada):
    depth, d, n = w_ada.shape
    rows = cond.shape[0]
    bn = 768
    return pl.pallas_call(
        _mod_kernel,
        grid=(depth, n // bn),
        in_specs=[
            pl.BlockSpec((rows, d), lambda i, j: (0, 0)),
            pl.BlockSpec((1, d, bn), lambda i, j: (i, 0, j)),
            pl.BlockSpec((1, 1, bn), lambda i, j: (i, 0, j)),
        ],
        out_specs=pl.BlockSpec((1, rows, bn), lambda i, j: (i, 0, j)),
        out_shape=jax.ShapeDtypeStruct((depth, rows, n), F32),
        compiler_params=_cparams(("arbitrary", "arbitrary")),
        name="adaln_modulation",
    )(cond, w_ada, b_ada.reshape(depth, 1, n))


def _l0_in_kernel(x_ref, mod_ref, g_ref, w_ref, ws_ref, bs_ref, *out_refs, tm, emit_f32):
    q_ref, k_ref, v_ref, sg_ref = out_refs[:4]
    x = x_ref[...]
    h = _pre(x, g_ref[0:1, :], mod_ref[0, 0:1, :], mod_ref[0, 1:2, :]).astype(BF16)
    z = _dot(h, w_ref[...])
    q_ref[...] = z[:, 0:W_A].astype(BF16)
    k_ref[...] = z[:, W_A:2 * W_A].astype(BF16)
    v_ref[...] = z[:, 2 * W_A:3 * W_A].astype(BF16)
    if emit_f32:
        out_refs[4][...] = z[:, W_A:2 * W_A]
        out_refs[5][...] = z[:, 2 * W_A:3 * W_A]
    u = _gelu(z[:, 3 * W_A:3 * W_A + W_B])
    gf = _gelu(z[:, 3 * W_A + W_B:])
    n_chunks = tm // CHUNK
    for gi in range(N_GROUPS_B):
        gg = gf[:, gi * LANES:(gi + 1) * LANES]
        mu = jnp.mean(gg, axis=-1, keepdims=True)
        dd = gg - mu
        var = jnp.mean(dd * dd, axis=-1, keepdims=True)
        gn = (dd * lax.rsqrt(var + EPS)).astype(BF16)
        rhs = jnp.concatenate([gn[c * CHUNK:(c + 1) * CHUNK, :] for c in range(n_chunks)], axis=1)
        s = _dot(ws_ref[gi], rhs) + bs_ref[:, gi:gi + 1]
        for c in range(n_chunks):
            uu = u[c * CHUNK:(c + 1) * CHUNK, gi * LANES:(gi + 1) * LANES]
            sg_ref[c * CHUNK:(c + 1) * CHUNK, gi * LANES:(gi + 1) * LANES] = (
                uu * s[:, c * LANES:(c + 1) * LANES]).astype(BF16)


def _l0_in(x, mod, g, w_in, w_s, b_s_t, *, tm, rows_per_batch, emit_f32):
    rows = x.shape[0]
    tpb = rows_per_batch // tm
    nb = mod.shape[0]
    d_in = w_in.shape[1]
    row_spec = lambda w: pl.BlockSpec((tm, w), lambda i: (i, 0))
    out_shape = [jax.ShapeDtypeStruct((rows, W_A), BF16)] * 3 + [jax.ShapeDtypeStruct((rows, W_B), BF16)]
    out_specs = [row_spec(W_A)] * 3 + [row_spec(W_B)]
    if emit_f32:
        out_shape += [jax.ShapeDtypeStruct((rows, W_A), F32)] * 2
        out_specs += [row_spec(W_A)] * 2
    return pl.pallas_call(
        functools.partial(_l0_in_kernel, tm=tm, emit_f32=emit_f32),
        grid=(rows // tm,),
        in_specs=[
            row_spec(D_MODEL),
            pl.BlockSpec((1, 6, D_MODEL), lambda i: ((i // tpb) % nb, 0, 0)),
            _const_spec((4, D_MODEL)),
            _const_spec((D_MODEL, d_in)),
            _const_spec((N_GROUPS_B, CHUNK, CHUNK)),
            _const_spec((CHUNK, N_GROUPS_B)),
        ],
        out_specs=out_specs,
        out_shape=out_shape,
        compiler_params=_cparams(("arbitrary",)),
        name="l0_in_proj_sgu",
    )(x, mod, g, w_in, w_s, b_s_t)


def _attn_kernel(*refs, n_local, has_ctx, rows):
    if has_ctx:
        q_ref, k_ref, v_ref, kc_ref, vc_ref, bias_ref, o_ref = refs
        r0 = pl.program_id(1) * Q_ROWS
        ks = jnp.clip(r0 - WIN_R // 2, 0, rows - KEY_ROWS)
        start = pl.multiple_of(ks * GRID_W, GRID_W)
        local = pl.ds(start, n_local)
    else:
        q_ref, k_ref, v_ref, o_ref = refs
        local = slice(None)
    first = lax.broadcasted_iota(jnp.int32, (1, LANES), 1) < HEAD_DIM
    for hp in range(W_A // LANES):
        lanes = slice(hp * LANES, (hp + 1) * LANES)
        kl = k_ref[local, lanes]
        vl = v_ref[local, lanes]
        q = q_ref[:, lanes] * jnp.asarray(HEAD_DIM ** -0.5, BF16)
        outs = []
        for half in range(2):
            qh = jnp.where(first if half == 0 else jnp.logical_not(first), q, jnp.zeros_like(q))
            s = _dot_nt(qh, kl)
            if has_ctx:
                s = s + bias_ref[0, 2 * hp + half]
                sc = _dot_nt(qh, kc_ref[:, lanes])
                m = jnp.maximum(jnp.max(s, axis=-1, keepdims=True), jnp.max(sc, axis=-1, keepdims=True))
            else:
                m = jnp.max(s, axis=-1, keepdims=True)
            p = jnp.exp(s - m)
            l = jnp.sum(p, axis=-1, keepdims=True)
            acc = _dot(p.astype(BF16), vl)
            if has_ctx:
                pc = jnp.exp(sc - m)
                l = l + jnp.sum(pc, axis=-1, keepdims=True)
                acc = acc + _dot(pc.astype(BF16), vc_ref[:, lanes])
            outs.append(acc / l)
        o_ref[:, lanes] = jnp.where(first, outs[0], outs[1]).astype(BF16)


def _attn_ctx(q, k, v, *, n_batch, n_seq):
    rows = q.shape[0]
    spec = pl.BlockSpec((n_seq, W_A), lambda b: (b, 0))
    return pl.pallas_call(
        functools.partial(_attn_kernel, n_local=n_seq, has_ctx=False, rows=0),
        grid=(n_batch,),
        in_specs=[spec, spec, spec],
        out_specs=spec,
        out_shape=jax.ShapeDtypeStruct((rows, W_A), BF16),
        compiler_params=_cparams(("arbitrary",)),
        name="attn_context",
    )(q, k, v)


def _attn_latent(q, k, v, kc, vc, bias, *, n_batch, n_seq, n_ctx):
    rows = n_seq // GRID_W
    n_rg = rows // Q_ROWS
    tq = Q_ROWS * GRID_W
    n_local = KEY_ROWS * GRID_W
    img_spec = pl.BlockSpec((n_seq, W_A), lambda b, rg: (b, 0))
    ctx_spec = pl.BlockSpec((n_ctx, W_A), lambda b, rg: (b, 0))
    q_spec = pl.BlockSpec((tq, W_A), lambda b, rg: (b * n_rg + rg, 0))

    def bias_map(b, rg):
        cfg = jnp.where(rg == 0, 0, jnp.where(rg == n_rg - 1, 2, 1))
        return (cfg, 0, 0, 0)

    return pl.pallas_call(
        functools.partial(_attn_kernel, n_local=n_local, has_ctx=True, rows=rows),
        grid=(n_batch, n_rg),
        in_specs=[q_spec, img_spec, img_spec, ctx_spec, ctx_spec,
                  pl.BlockSpec((1, N_HEADS, tq, n_local), bias_map)],
        out_specs=q_spec,
        out_shape=jax.ShapeDtypeStruct((n_batch * n_seq, W_A), BF16),
        compiler_params=_cparams(("arbitrary", "arbitrary")),
        name="attn_latent",
    )(q, k, v, kc, vc, bias)


def _latent_bias_table(rpb, rows):
    n_heads = rpb.shape[0]
    qcol = np.arange(GRID_W)[:, None]
    kcol = np.arange(GRID_W)[None, :]
    cs = np.clip(qcol - WIN_C // 2, 0, GRID_W - WIN_C)
    col_valid = (kcol >= cs) & (kcol < cs + WIN_C)
    dc = np.clip(kcol - qcol + (WIN_C - 1), 0, 2 * WIN_C - 2)
    pick = (dc[:, :, None] == np.arange(2 * WIN_C - 1)).astype(np.float32)
    col_tab = jnp.einsum("hrc,qkc->hrqk", rpb.astype(F32), pick, precision=lax.Precision.HIGHEST)
    col_tab = jnp.where(col_valid[None, None], col_tab, NEG)
    tables = []
    for r0 in (0, 2 * Q_ROWS, rows - Q_ROWS):
        ks = min(max(r0 - WIN_R // 2, 0), rows - KEY_ROWS)
        qrow = r0 + np.arange(Q_ROWS)[:, None]
        krow = ks + np.arange(KEY_ROWS)[None, :]
        rs = np.clip(qrow - WIN_R // 2, 0, rows - WIN_R)
        row_valid = (krow >= rs) & (krow < rs + WIN_R)
        dr = np.clip(krow - qrow + (WIN_R - 1), 0, 2 * WIN_R - 2)
        blocks = [jnp.stack([col_tab[:, dr[i, k]] if row_valid[i, k]
                             else jnp.full((n_heads, GRID_W, GRID_W), NEG, F32)
                             for k in range(KEY_ROWS)], axis=2)
                  for i in range(Q_ROWS)]
        b = jnp.stack(blocks, axis=1)
        tables.append(b.reshape(n_heads, Q_ROWS * GRID_W, KEY_ROWS * GRID_W))
    return jnp.stack(tables)


def _out_kernel(*refs, n_in):
    in_refs = refs[:n_in]
    x_ref, mod_ref, g_ref, w_ref, rw_ref, x1_ref, h2_ref, gates_ref, rankc_ref, rankr_ref = refs[n_in:]
    y = None
    off = 0
    for r in in_refs:
        w = r.shape[1]
        part = _dot(r[...], w_ref[off:off + w, :])
        y = part if y is None else y + part
        off += w
    x1 = x_ref[...] + mod_ref[0, 2:3, :] * _rms(y, g_ref[1:2, :])
    x1_ref[...] = x1
    h2 = _pre(x1, g_ref[2:3, :], mod_ref[0, 3:4, :], mod_ref[0, 4:5, :])
    h2_ref[...] = h2.astype(BF16)
    for sb in range(h2.shape[0] // MOE_SRC):
        blk = slice(sb * MOE_SRC, (sb + 1) * MOE_SRC)
        hs = h2[blk, :]
        h_hi = hs.astype(BF16)
        h_lo = (hs - h_hi.astype(F32)).astype(BF16)
        rw = rw_ref[...]
        r_hi = rw.astype(BF16)
        r_lo = (rw - r_hi.astype(F32)).astype(BF16)
        logits = _dot_nt(r_hi, h_hi) + (_dot_nt(r_hi, h_lo) + _dot_nt(r_lo, h_hi))
        tm = MOE_SRC
        row = lax.broadcasted_iota(jnp.int32, logits.shape, 0).astype(F32)
        logits = jnp.where(row < N_EXPERTS, logits, -jnp.inf)
        m1 = jnp.max(logits, axis=0, keepdims=True)
        i1 = jnp.min(jnp.where(logits == m1, row, float(LANES)), axis=0, keepdims=True)
        rest_l = jnp.where(row == i1, -jnp.inf, logits)
        m2 = jnp.max(rest_l, axis=0, keepdims=True)
        i2 = jnp.min(jnp.where(rest_l == m2, row, float(LANES)), axis=0, keepdims=True)
        e2 = jnp.exp(m2 - m1)
        w1 = 1.0 / (1.0 + e2)
        w2 = e2 / (1.0 + e2)
        sel1 = row == i1
        sel2 = row == i2
        member = jnp.where(sel1, 1.0, 0.0) + jnp.where(sel2, 1.0, 0.0)
        gates = jnp.where(sel1, w1, 0.0) + jnp.where(sel2, w2, 0.0)
        before = (lax.broadcasted_iota(jnp.int32, (tm, tm), 0)
                  < lax.broadcasted_iota(jnp.int32, (tm, tm), 1))
        rank = _dot(member.astype(BF16), jnp.where(before, 1.0, 0.0).astype(BF16))
        rank = jnp.where(member > 0.0, rank, -1.0)
        gates_ref[blk, :] = gates.T
        rankc_ref[blk, :] = rank.T.astype(jnp.int32)
        rankr_ref[:, blk] = rank[0:N_EXPERTS, :].astype(jnp.int32)


def _out_proj(ins, x, mod, g, w_out, router_w, *, tm, rows_per_batch):
    rows = x.shape[0]
    tpb = rows_per_batch // tm
    nb = mod.shape[0]
    row_spec = lambda w: pl.BlockSpec((tm, w), lambda i: (i, 0))
    in_specs = [row_spec(a.shape[1]) for a in ins] + [
        row_spec(D_MODEL),
        pl.BlockSpec((1, 6, D_MODEL), lambda i: ((i // tpb) % nb, 0, 0)),
        _const_spec((4, D_MODEL)),
        _const_spec((D_MODEL, D_MODEL)),
        _const_spec((LANES, D_MODEL)),
    ]
    out_shape = [jax.ShapeDtypeStruct((rows, D_MODEL), F32), jax.ShapeDtypeStruct((rows, D_MODEL), BF16),
                 jax.ShapeDtypeStruct((rows, LANES), F32),
                 jax.ShapeDtypeStruct((rows, LANES), jnp.int32),
                 jax.ShapeDtypeStruct((N_EXPERTS, rows), jnp.int32)]
    out_specs = [row_spec(D_MODEL), row_spec(D_MODEL), row_spec(LANES), row_spec(LANES),
                 pl.BlockSpec((N_EXPERTS, tm), lambda i: (0, i))]
    return pl.pallas_call(
        functools.partial(_out_kernel, n_in=len(ins)),
        grid=(rows // tm,),
        in_specs=in_specs,
        out_specs=out_specs,
        out_shape=out_shape,
        compiler_params=_cparams(("arbitrary",)),
        name="out_proj_norms_router",
    )(*ins, x, mod, g, w_out, router_w)


def _l0_tail_kernel(oa_ref, sg_ref, x_ref, mod_ref, g_ref, wo_ref, wg_ref, wu_ref, wd_ref,
                    mod1_ref, g1_ref, wi_ref, x2_ref, gate_ref, xr_ref, xq_ref, x1_scr, h2_scr, acc_ref):
    fc = pl.program_id(1)

    @pl.when(fc == 0)
    def _():
        y = _dot(oa_ref[...], wo_ref[0:W_A, :]) + _dot(sg_ref[...], wo_ref[W_A:, :])
        x1 = x_ref[...] + mod_ref[0, 2:3, :] * _rms(y, g_ref[1:2, :])
        x1_scr[...] = x1
        h2_scr[...] = _pre(x1, g_ref[2:3, :], mod_ref[0, 3:4, :], mod_ref[0, 4:5, :]).astype(BF16)
        acc_ref[...] = jnp.zeros_like(acc_ref)

    h = h2_scr[...]
    t = _silu(_dot(h, wg_ref[...])) * _dot(h, wu_ref[...])
    acc_ref[...] += _dot(t.astype(BF16), wd_ref[...])

    @pl.when(fc == FF_CHUNKS - 1)
    def _():
        x2 = x1_scr[...] + mod_ref[0, 5:6, :] * _rms(acc_ref[...], g_ref[3:4, :])
        x2_ref[...] = x2
        hn = _pre(x2, g1_ref[0:1, :], mod1_ref[0, 0:1, :], mod1_ref[0, 1:2, :]).astype(BF16)
        z = _dot(hn, wi_ref[...])
        gate_ref[...] = z[:, 0:D_RNN].astype(BF16)
        xr_ref[...] = z[:, D_RNN:2 * D_RNN]
        xq_ref[...] = z[:, 2 * D_RNN:]


def _l0_tail(oa, sg, x, mod, g, w_out, wg, wu, wd, mod1, g1, w_in1, *, tm, rows_per_batch):
    rows = x.shape[0]
    tpb = rows_per_batch // tm
    nb = mod.shape[0]
    row_spec = lambda w: pl.BlockSpec((tm, w), lambda i, f: (i, 0))
    mod_spec = pl.BlockSpec((1, 6, D_MODEL), lambda i, f: ((i // tpb) % nb, 0, 0))
    return pl.pallas_call(
        _l0_tail_kernel,
        grid=(rows // tm, FF_CHUNKS),
        in_specs=[row_spec(W_A), row_spec(W_B), row_spec(D_MODEL), mod_spec,
                  _const_spec((4, D_MODEL)), _const_spec((W_A + W_B, D_MODEL)),
                  pl.BlockSpec((D_MODEL, FF_CHUNK), lambda i, f: (0, f)),
                  pl.BlockSpec((D_MODEL, FF_CHUNK), lambda i, f: (0, f)),
                  pl.BlockSpec((FF_CHUNK, D_MODEL), lambda i, f: (f, 0)),
                  mod_spec, _const_spec((4, D_MODEL)), _const_spec((D_MODEL, w_in1.shape[1]))],
        out_specs=[row_spec(D_MODEL), row_spec(D_RNN), row_spec(D_RNN), row_spec(D_POOL)],
        out_shape=[jax.ShapeDtypeStruct((rows, D_MODEL), F32),
                   jax.ShapeDtypeStruct((rows, D_RNN), BF16),
                   jax.ShapeDtypeStruct((rows, D_RNN), F32),
                   jax.ShapeDtypeStruct((rows, D_POOL), F32)],
        scratch_shapes=[pltpu.VMEM((tm, D_MODEL), F32), pltpu.VMEM((tm, D_MODEL), BF16),
                        pltpu.VMEM((tm, D_MODEL), F32)],
        compiler_params=_cparams(("arbitrary", "arbitrary")),
        name="l0_tail_l1_in_proj",
    )(oa, sg, x, mod, g, w_out, wg, wu, wd, mod1, g1, w_in1)


def _moe_plan(rank_row):
    n_tok = rank_row.shape[1]
    n_src = n_tok // MOE_SRC
    n_tiles = 2 * n_tok // MOE_TILE + N_EXPERTS
    member = rank_row >= 0
    cnt_blk = member.reshape(N_EXPERTS, n_src, MOE_SRC).sum(-1).astype(jnp.int32)
    cum = jnp.concatenate([jnp.zeros((N_EXPERTS, 1), jnp.int32), jnp.cumsum(cnt_blk, axis=1)], axis=1)
    cnt = cum[:, -1]
    tiles_e = (cnt + MOE_TILE - 1) // MOE_TILE
    tile_end = jnp.cumsum(tiles_e)
    tile_base = tile_end - tiles_e
    slot0 = tile_base[:, None] * MOE_TILE + cum[:, :-1]
    pos_row = jnp.where(member, jnp.repeat(slot0, MOE_SRC, axis=1) + rank_row, -1)
    pos_row = jnp.pad(pos_row, ((0, 0), (0, MOE_RING * MOE_SRC)), constant_values=-1)
    d = jnp.arange(n_tiles, dtype=jnp.int32)
    tile_e = jnp.minimum(jnp.sum(d[:, None] >= tile_end[None, :], axis=1), N_EXPERTS - 1).astype(jnp.int32)
    valid = d < tile_end[-1]
    lo_slot = (d - tile_base[tile_e]) * MOE_TILE
    hi_slot = jnp.minimum(lo_slot + MOE_TILE, cnt[tile_e])
    cum_d = cum[tile_e]
    tile_lo = jnp.sum(cum_d[:, 1:] <= lo_slot[:, None], axis=1).astype(jnp.int32)
    tile_hi = jnp.sum(cum_d[:, :-1] < hi_slot[:, None], axis=1).astype(jnp.int32) - 1
    tile_n = jnp.where(valid, tile_hi - tile_lo + 1, 0).astype(jnp.int32)
    tile_lo = jnp.where(valid, tile_lo, 0)
    first = slot0.T
    c_t0 = first // MOE_CTILE
    c_nt = jnp.where(cnt_blk.T > 0, (first + cnt_blk.T - 1) // MOE_CTILE - c_t0 + 1, 0)
    flat = lambda a: a.reshape(-1).astype(jnp.int32)
    return pos_row, tile_e, tile_lo, tile_n, flat(c_t0), flat(c_nt), flat(first), n_tiles


def _moe_ffn_kernel(te_ref, tl_ref, tn_ref, pos_ref, h_hbm, wg_ref, wu_ref, wd_ref, o_ref,
                    hbuf, sem, acc_ref, *, n_tiles):
    d = pl.program_id(0)
    e = te_ref[d]
    lo = tl_ref[d]
    n = tn_ref[d]

    def copy(blk, slot):
        start = pl.multiple_of(blk * MOE_SRC, MOE_SRC)
        return pltpu.make_async_copy(h_hbm.at[pl.ds(start, MOE_SRC), :], hbuf.at[slot], sem.at[slot])

    def start_first(first_blk, count):
        for k in range(MOE_RING):
            @pl.when(k < count)
            def _(k=k):
                copy(first_blk + k, k).start()

    @pl.when(d == 0)
    def _():
        hbuf[...] = jnp.zeros_like(hbuf)
        start_first(lo, n)

    slot_ids = d * MOE_TILE + lax.broadcasted_iota(jnp.int32, (MOE_TILE, 1), 0)

    def picks(first_blk, n_blk):
        start = pl.multiple_of(first_blk * MOE_SRC, MOE_SRC)
        pos = pos_ref[pl.ds(e, 1), pl.ds(start, n_blk * MOE_SRC)]
        return jnp.where(pos == slot_ids, 1.0, 0.0).astype(BF16)

    for k in range(MOE_RING):
        @pl.when(k < n)
        def _(k=k):
            copy(lo + k, k).wait()

    prev = 0
    for n_blk in MOE_RING_SIZES:
        last = n_blk == MOE_RING_SIZES[-1]

        @pl.when((n > prev) if last else ((n > prev) & (n <= n_blk)))
        def _(n_blk=n_blk):
            acc_ref[...] = _dot(picks(lo, n_blk), hbuf[0:n_blk].reshape(n_blk * MOE_SRC, D_MODEL))

        prev = n_blk

    def extra(j, carry):
        cp = copy(lo + j, 0)
        cp.start()
        cp.wait()
        acc_ref[...] += _dot(picks(lo + j, 1), hbuf[0])
        return carry

    lax.fori_loop(MOE_RING, n, extra, 0)

    @pl.when(d + 1 < n_tiles)
    def _():
        start_first(tl_ref[d + 1], tn_ref[d + 1])

    @pl.when(n > 0)
    def _():
        x = acc_ref[...].astype(BF16)
        y = None
        for fc in range(FF_CHUNKS):
            cols = slice(fc * FF_CHUNK, (fc + 1) * FF_CHUNK)
            t = _silu(_dot(x, wg_ref[0, :, cols])) * _dot(x, wu_ref[0, :, cols])
            part = _dot(t.astype(BF16), wd_ref[0, cols, :])
            y = part if y is None else y + part
        o_ref[...] = y.astype(BF16)

    @pl.when(n == 0)
    def _():
        o_ref[...] = jnp.zeros_like(o_ref)


def _moe_ffn(h2, pos_row, tile_e, tile_lo, tile_n, wg, wu, wd, *, n_tiles):
    n_tok = h2.shape[0]
    w_spec = lambda shape: pl.BlockSpec((1,) + shape, lambda d, te, tl, tn: (te[d], 0, 0))
    grid_spec = pltpu.PrefetchScalarGridSpec(
        num_scalar_prefetch=3,
        grid=(n_tiles,),
        in_specs=[
            pl.BlockSpec(pos_row.shape, lambda d, te, tl, tn: (0, 0), pipeline_mode=pl.Buffered(1)),
            pl.BlockSpec(memory_space=pl.ANY),
            w_spec((D_MODEL, D_FF)), w_spec((D_MODEL, D_FF)), w_spec((D_FF, D_MODEL)),
        ],
        out_specs=pl.BlockSpec((MOE_TILE, D_MODEL), lambda d, te, tl, tn: (d, 0)),
        scratch_shapes=[pltpu.VMEM((MOE_RING, MOE_SRC, D_MODEL), BF16),
                        pltpu.SemaphoreType.DMA((MOE_RING,)),
                        pltpu.VMEM((MOE_TILE, D_MODEL), F32)],
    )
    return pl.pallas_call(
        functools.partial(_moe_ffn_kernel, n_tiles=n_tiles),
        grid_spec=grid_spec,
        out_shape=jax.ShapeDtypeStruct((n_tiles * MOE_TILE, D_MODEL), BF16),
        compiler_params=_cparams(("arbitrary",)),
        name="moe_dispatch_swiglu",
    )(tile_e, tile_lo, tile_n, pos_row, h2, wg, wu, wd)


def _moe_combine_kernel(t0_ref, nt_ref, first_ref, rank_ref, gates_ref, ys_hbm, x1_ref, mod_ref, g_ref,
                        o_ref, ybuf, sem):
    s = pl.program_id(0)

    def copy(blk, e, k):
        start = pl.multiple_of((t0_ref[blk * N_EXPERTS + e] + k) * MOE_CTILE, MOE_CTILE)
        buf = (blk % 2) * N_EXPERTS + e
        return pltpu.make_async_copy(ys_hbm.at[pl.ds(start, MOE_CTILE), :],
                                     ybuf.at[buf, pl.ds(k * MOE_CTILE, MOE_CTILE), :],
                                     sem.at[buf * MOE_CMAX + k])

    def start_block(blk):
        for e in range(N_EXPERTS):
            for k in range(MOE_CMAX):
                @pl.when(nt_ref[blk * N_EXPERTS + e] > k)
                def _(e=e, k=k):
                    copy(blk, e, k).start()

    @pl.when(s == 0)
    def _():
        ybuf[...] = jnp.zeros_like(ybuf)
        start_block(s)

    @pl.when(s + 1 < pl.num_programs(0))
    def _():
        start_block(s + 1)

    most = nt_ref[s * N_EXPERTS]
    for e in range(N_EXPERTS):
        most = jnp.maximum(most, nt_ref[s * N_EXPERTS + e])
        for k in range(MOE_CMAX):
            @pl.when(nt_ref[s * N_EXPERTS + e] > k)
            def _(e=e, k=k):
                copy(s, e, k).wait()

    def combine(n_t):
        lane = lax.broadcasted_iota(jnp.int32, (1, n_t * MOE_CTILE), 1)
        y = None
        for e in range(N_EXPERTS):
            rank = rank_ref[:, e:e + 1]
            pos = jnp.where(rank >= 0, rank + first_ref[s * N_EXPERTS + e], -1)
            base = t0_ref[s * N_EXPERTS + e] * MOE_CTILE
            pick = jnp.where(pos == base + lane, 1.0, 0.0).astype(BF16)
            part = gates_ref[:, e:e + 1] * _dot(
                pick, ybuf[(s % 2) * N_EXPERTS + e, 0:n_t * MOE_CTILE, :])
            y = part if y is None else y + part
        o_ref[...] = x1_ref[...] + mod_ref[0, 5:6, :] * _rms(y, g_ref[3:4, :])

    @pl.when(most < MOE_CMAX)
    def _():
        combine(MOE_CMAX - 1)

    @pl.when(most >= MOE_CMAX)
    def _():
        combine(MOE_CMAX)


def _moe_combine(ys, rank_col, gates, c_t0, c_nt, c_first, x1, mod, g, *, rows_per_batch):
    n_tok = x1.shape[0]
    tpb = rows_per_batch // MOE_SRC
    nb = mod.shape[0]
    row_spec = lambda w: pl.BlockSpec((MOE_SRC, w), lambda s, *_: (s, 0))
    grid_spec = pltpu.PrefetchScalarGridSpec(
        num_scalar_prefetch=3,
        grid=(n_tok // MOE_SRC,),
        in_specs=[
            row_spec(LANES), row_spec(LANES),
            pl.BlockSpec(memory_space=pl.ANY),
            row_spec(D_MODEL),
            pl.BlockSpec((1, 6, D_MODEL), lambda s, *_: ((s // tpb) % nb, 0, 0)),
            pl.BlockSpec((4, D_MODEL), lambda s, *_: (0, 0), pipeline_mode=pl.Buffered(1)),
        ],
        out_specs=row_spec(D_MODEL),
        scratch_shapes=[pltpu.VMEM((2 * N_EXPERTS, MOE_CMAX * MOE_CTILE, D_MODEL), BF16),
                        pltpu.SemaphoreType.DMA((2 * N_EXPERTS * MOE_CMAX,))],
    )
    return pl.pallas_call(
        _moe_combine_kernel,
        grid_spec=grid_spec,
        out_shape=jax.ShapeDtypeStruct((n_tok, D_MODEL), F32),
        compiler_params=_cparams(("arbitrary",)),
        name="moe_combine_post",
    )(c_t0, c_nt, c_first, rank_col, gates, ys, x1, mod, g)


def _with_halo(x_ref, prev_ref, next_ref, t_idx, n_tiles):
    x = x_ref[...]
    prev = jnp.where(t_idx > 0, prev_ref[:, 0], 0.0)
    nxt = jnp.where(t_idx < n_tiles - 1, next_ref[:, 0], 0.0)
    return jnp.concatenate([prev, x, nxt], axis=1)


def _scan_kernel(*refs, fwd, tt, n_tiles, n_seq):
    if fwd:
        (xr_ref, xrp_ref, xrn_ref, xq_ref, xqp_ref, xqn_ref, cw_ref, cb_ref, wgt_ref, ba_ref, bx_ref,
         lam_ref, h0_ref, pw_ref, ps_ref, hf_ref, yp_ref, hl_ref, a_scr, b_scr, h_scr, carry) = refs
    else:
        (xr_ref, xrp_ref, xrn_ref, cw_ref, cb_ref, wgt_ref, ba_ref, bx_ref, lam_ref, h0_ref,
         hf_ref, gate_ref, yr_ref, hl_ref, a_scr, b_scr, h_scr, carry) = refs
    i = pl.program_id(1)
    t_idx = i if fwd else n_tiles - 1 - i
    nb = SUBLANES

    ext = _with_halo(xr_ref, xrp_ref, xrn_ref, t_idx, n_tiles)
    xc = cb_ref[...][None]
    for j in range(CONV_W):
        lo = HALO + j - CONV_LEFT
        xc = xc + cw_ref[j:j + 1, :][None] * ext[:, lo:lo + tt, :]
    xc = xc.reshape(nb * tt, D_RNN)
    xb = xc.astype(BF16)
    r0 = _dot(xb[:, :RNN_HALF], wgt_ref[0])
    r1 = _dot(xb[:, RNN_HALF:], wgt_ref[1])
    ra = jnp.concatenate([r0[:, :RNN_HALF], r1[:, :RNN_HALF]], axis=1) + ba_ref[...]
    ri = jnp.concatenate([r0[:, RNN_HALF:], r1[:, RNN_HALF:]], axis=1) + bx_ref[...]
    nl = -lam_ref[...]
    softplus = jnp.maximum(nl, 0.0) + jnp.log1p(jnp.exp(-jnp.abs(nl)))
    a = jnp.exp((-0.5 * RG_C * softplus) * (1.0 + jnp.tanh(0.5 * ra)))
    b = jnp.sqrt(1.0 - a * a) * ((0.5 + 0.5 * jnp.tanh(0.5 * ri)) * xc)
    n_lc = D_RNN // LANES
    for bi in range(nb):
        for lc in range(n_lc):
            a_scr[lc, pl.ds(bi, tt, stride=nb), :] = a[bi * tt:(bi + 1) * tt, lc * LANES:(lc + 1) * LANES]
            b_scr[lc, pl.ds(bi, tt, stride=nb), :] = b[bi * tt:(bi + 1) * tt, lc * LANES:(lc + 1) * LANES]

    @pl.when(i == 0)
    def _():
        for lc in range(n_lc):
            carry[lc] = h0_ref[:, lc * LANES:(lc + 1) * LANES]

    def step(s, h):
        t = s if fwd else tt - 1 - s
        row = pl.multiple_of(t * nb, nb)
        h = a_scr[:, pl.ds(row, nb), :] * h + b_scr[:, pl.ds(row, nb), :]
        h_scr[:, pl.ds(row, nb), :] = h
        return h

    h_last = lax.fori_loop(0, tt, step, carry[...], unroll=8)
    carry[...] = h_last
    for lc in range(n_lc):
        hl_ref[:, lc * LANES:(lc + 1) * LANES] = h_last[lc]

    def unscan(bi):
        return jnp.concatenate([h_scr[lc, pl.ds(bi, tt, stride=nb), :] for lc in range(n_lc)], axis=1)

    if fwd:
        for bi in range(nb):
            hf_ref[bi] = unscan(bi)
        e = _with_halo(xq_ref, xqp_ref, xqn_ref, t_idx, n_tiles)
        length = tt + 2 * HALO
        a2 = e[:, 0:length - 1] + e[:, 1:length]
        a4 = a2[:, 0:length - 3] + a2[:, 2:length - 1]
        a8 = a4[:, 0:length - 7] + a4[:, 4:length - 3]
        a16 = a8[:, 0:length - 15] + a8[:, 8:length - 7]
        lane = lax.broadcasted_iota(jnp.int32, (1, 1, D_POOL), 2)
        wsum = jnp.where(lane < POOL_C, a2[:, HALO - 1:HALO - 1 + tt],
                         jnp.where(lane < 2 * POOL_C, a4[:, HALO - 2:HALO - 2 + tt],
                                   jnp.where(lane < 3 * POOL_C, a8[:, HALO - 4:HALO - 4 + tt],
                                             a16[:, 0:tt])))
        half = jnp.where(lane < POOL_C, POOL_WINDOWS[0] // 2,
                         jnp.where(lane < 2 * POOL_C, POOL_WINDOWS[1] // 2,
                                   jnp.where(lane < 3 * POOL_C, POOL_WINDOWS[2] // 2,
                                             POOL_WINDOWS[3] // 2)))
        tg = t_idx * tt + lax.broadcasted_iota(jnp.int32, (1, tt, D_POOL), 1)
        cnt = (jnp.minimum(tg + half, n_seq) - jnp.maximum(tg - half, 0)).astype(F32)
        dlt = (wsum / cnt - xq_ref[...]).astype(BF16).reshape(nb * tt, D_POOL)
        yp = _dot(dlt, pw_ref[...]) * ps_ref[...]
        yp_ref[...] = yp.reshape(nb, tt, D_POOL).astype(BF16)
    else:
        for bi in range(nb):
            hb = unscan(bi)
            yr_ref[bi] = ((hf_ref[bi] + hb) * _gelu(gate_ref[bi].astype(F32))).astype(BF16)


def _scan(fwd, xr, xq, conv_w, conv_b, w_gates, ba, bx, lam, h0, pool_w, pool_scale, h_f, gate, *, tt):
    n_batch, n_seq, _ = xr.shape
    n_bg = n_batch // SUBLANES
    n_tiles = n_seq // tt
    hb = tt // HALO
    n_hb = n_seq // HALO

    def t_of(i):
        return i if fwd else n_tiles - 1 - i

    def tile(c):
        return pl.BlockSpec((SUBLANES, tt, c), lambda b, i: (b, t_of(i), 0))

    def prev(c):
        return pl.BlockSpec((SUBLANES, 1, HALO, c),
                            lambda b, i: (b, jnp.maximum(t_of(i) * hb - 1, 0), 0, 0))

    def nxt(c):
        return pl.BlockSpec((SUBLANES, 1, HALO, c),
                            lambda b, i: (b, jnp.minimum((t_of(i) + 1) * hb, n_hb - 1), 0, 0))

    state = pl.BlockSpec((SUBLANES, D_RNN), lambda b, i: (b, 0))
    xr4 = xr.reshape(n_batch, n_hb, HALO, D_RNN)
    common = [_const_spec((CONV_W, D_RNN)), _const_spec((1, D_RNN)),
              _const_spec((2, RNN_HALF, D_RNN)), _const_spec((1, D_RNN)), _const_spec((1, D_RNN)),
              _const_spec((1, D_RNN)), state]
    common_args = [conv_w, conv_b, w_gates, ba, bx, lam, h0]
    n_lc = D_RNN // LANES
    scratch = ([pltpu.VMEM((n_lc, SUBLANES * tt, LANES), F32)] * 3
               + [pltpu.VMEM((n_lc, SUBLANES, LANES), F32)])
    if fwd:
        xq4 = xq.reshape(n_batch, n_hb, HALO, D_POOL)
        in_specs = ([tile(D_RNN), prev(D_RNN), nxt(D_RNN), tile(D_POOL), prev(D_POOL), nxt(D_POOL)]
                    + common + [_const_spec((D_POOL, D_POOL)), _const_spec((1, D_POOL))])
        args = [xr, xr4, xr4, xq, xq4, xq4] + common_args + [pool_w, pool_scale]
        out_specs = [tile(D_RNN), tile(D_POOL), state]
        out_shape = [jax.ShapeDtypeStruct((n_batch, n_seq, D_RNN), F32),
                     jax.ShapeDtypeStruct((n_batch, n_seq, D_POOL), BF16),
                     jax.ShapeDtypeStruct((n_batch, D_RNN), F32)]
    else:
        in_specs = [tile(D_RNN), prev(D_RNN), nxt(D_RNN)] + common + [tile(D_RNN), tile(D_RNN)]
        args = [xr, xr4, xr4] + common_args + [h_f, gate]
        out_specs = [tile(D_RNN), state]
        out_shape = [jax.ShapeDtypeStruct((n_batch, n_seq, D_RNN), BF16),
                     jax.ShapeDtypeStruct((n_batch, D_RNN), F32)]
    return pl.pallas_call(
        functools.partial(_scan_kernel, fwd=fwd, tt=tt, n_tiles=n_tiles, n_seq=n_seq),
        grid=(n_bg, n_tiles),
        in_specs=in_specs,
        out_specs=out_specs,
        out_shape=out_shape,
        scratch_shapes=scratch,
        compiler_params=_cparams(("arbitrary", "arbitrary")),
        name="rglru_forward_pool" if fwd else "rglru_backward_combine",
    )(*args)


def _gate_weights(wa, wx):
    per_half = RNN_BLOCKS // 2

    def half_dense(w, k):
        blocks = [w[k * per_half + b] for b in range(per_half)]
        rows = []
        for bi, blk in enumerate(blocks):
            row = [blk if bj == bi else jnp.zeros_like(blk) for bj in range(per_half)]
            rows.append(jnp.concatenate(row, axis=1))
        return jnp.concatenate(rows, axis=0)

    return jnp.stack([jnp.concatenate([half_dense(wa, k), half_dense(wx, k)], axis=1)
                      for k in range(2)]).astype(BF16)


def _pool_weights(pool_w):
    n = pool_w.shape[0]
    rows = []
    for i in range(n):
        rows.append(jnp.concatenate([pool_w[i] if j == i else jnp.zeros_like(pool_w[i])
                                     for j in range(n)], axis=1))
    return jnp.concatenate(rows, axis=0).astype(BF16)


def kernel(x_prompt, x_sample, cache_k, cache_v, state_h, c, c_ctx, w_ada, b_ada, norm_g, w_in_even,
           w_out_even, na_rpb, sgu_w, sgu_b, w_in_odd, w_out_odd, conv_w, conv_b, rg_wa, rg_ba, rg_wx,
           rg_bx, rg_lam, pool_w, pool_scale, ffn_wg, ffn_wu, ffn_wd, router_w, moe_wg, moe_wu, moe_wd):
    bp, n_p, _ = x_prompt.shape
    bs, n_s, _ = x_sample.shape
    n_ctx = cache_k.shape[3]

    cond = jnp.concatenate([c_ctx[None, :], c, jnp.zeros((2 * SUBLANES - 1 - bs, D_MODEL), F32)], axis=0)
    mods = _modulation(cond, w_ada, b_ada).reshape(w_ada.shape[0], cond.shape[0], 6, D_MODEL)

    streams = [
        dict(x=x_prompt.reshape(bp * n_p, D_MODEL), nb=bp, n=n_p, tm=256, latent=False),
        dict(x=x_sample.reshape(bs * n_s, D_MODEL), nb=bs, n=n_s, tm=512, latent=True),
    ]
    new_k = new_v = new_h = None

    g = norm_g[0]
    w_in = w_in_even[0].astype(BF16)
    w_out = w_out_even[0].astype(BF16)
    w_s = sgu_w[0].astype(BF16)
    b_s_t = sgu_b[0].T
    wg, wu, wd = (w[0].astype(BF16) for w in (ffn_wg, ffn_wu, ffn_wd))
    w_in1 = w_in_odd[0].astype(BF16)
    bias = _latent_bias_table(na_rpb[0], n_s // GRID_W)
    kc = cache_k[:, 0].transpose(0, 2, 1, 3).reshape(bs * n_ctx, W_A).astype(BF16)
    vc = cache_v[:, 0].transpose(0, 2, 1, 3).reshape(bs * n_ctx, W_A).astype(BF16)
    for st in streams:
        mod = mods[0, 1:1 + bs] if st["latent"] else mods[0, 0:1]
        outs = _l0_in(st["x"], mod, g, w_in, w_s, b_s_t, tm=st["tm"], rows_per_batch=st["n"],
                      emit_f32=not st["latent"])
        q, k, v, sg = outs[:4]
        if st["latent"]:
            oa = _attn_latent(q, k, v, kc, vc, bias, n_batch=st["nb"], n_seq=st["n"], n_ctx=n_ctx)
        else:
            oa = _attn_ctx(q, k, v, n_batch=st["nb"], n_seq=st["n"])
            heads = lambda a: a.reshape(bp, n_p, N_HEADS, HEAD_DIM).transpose(0, 2, 1, 3)[:, None]
            new_k, new_v = heads(outs[4]), heads(outs[5])
        mod1 = mods[1, 1:1 + bs] if st["latent"] else mods[1, 0:1]
        st["x"], st["gate"], st["xr"], st["xq"] = _l0_tail(
            oa, sg, st["x"], mod, g, w_out, wg, wu, wd, mod1, norm_g[1], w_in1,
            tm=st["tm"], rows_per_batch=st["n"])

    g = norm_g[1]
    w_out = w_out_odd[0].astype(BF16)
    w_gates = [_gate_weights(rg_wa[0, d], rg_wx[0, d]) for d in range(2)]
    pw = _pool_weights(pool_w[0])
    ps = pool_scale[0][None, :]
    rw = jnp.pad(router_w[0].T, ((0, LANES - N_EXPERTS), (0, 0)))
    wg, wu, wd = (w[0].astype(BF16) for w in (moe_wg, moe_wu, moe_wd))
    for st in streams:
        mod = mods[1, 1:1 + bs] if st["latent"] else mods[1, 0:1]
        nb, n = st["nb"], st["n"]
        gate, xr, xq = (st[key].reshape(nb, n, -1) for key in ("gate", "xr", "xq"))
        h0 = state_h[:, 0].astype(F32) if st["latent"] else jnp.zeros((nb, 2, D_RNN), F32)
        scan_args = lambda d: (conv_w[0], conv_b[0][None, :], w_gates[d], rg_ba[0, d][None, :],
                               rg_bx[0, d][None, :], rg_lam[0, d][None, :], h0[:, d])
        h_f, y_pool, h_f_last = _scan(True, xr, xq, *scan_args(0), pw, ps, None, None, tt=128)
        y_rec, h_b_last = _scan(False, xr, None, *scan_args(1), None, None, h_f, gate, tt=128)
        if not st["latent"]:
            new_h = jnp.stack([h_f_last, h_b_last], axis=1)[:, None]
        x1, h2, gates, rank_col, rank_row = _out_proj(
            [y_rec.reshape(nb * n, D_RNN), y_pool.reshape(nb * n, D_POOL)],
            st["x"], mod, g, w_out, rw, tm=st["tm"], rows_per_batch=n)
        pos_row, tile_e, tile_lo, tile_n, c_t0, c_nt, c_first, n_tiles = _moe_plan(rank_row)
        ys = _moe_ffn(h2, pos_row, tile_e, tile_lo, tile_n, wg, wu, wd, n_tiles=n_tiles)
        st["x"] = _moe_combine(ys, rank_col, gates, c_t0, c_nt, c_first, x1, mod, g, rows_per_batch=n)

    y_prompt = streams[0]["x"].reshape(bp, n_p, D_MODEL)
    y_sample = streams[1]["x"].reshape(bs, n_s, D_MODEL)
    return (y_prompt, y_sample, new_k, new_v, new_h)
```

```python
import functools

import jax
import jax.numpy as jnp
import numpy as np
from jax import lax
from jax.experimental import pallas as pl
from jax.experimental.pallas import tpu as pltpu

F32 = jnp.float32
BF16 = jnp.bfloat16

D_MODEL = 1024
GRID_W = 64
HEAD_DIM = 64
N_HEADS = 8
W_A = N_HEADS * HEAD_DIM
WIN_R = 8
WIN_C = 16
N_GROUPS_B = 4
W_B = 512
CHUNK = 128
D_RNN = 768
RNN_BLOCKS = 8
RNN_BW = D_RNN // RNN_BLOCKS
RNN_HALF = D_RNN // 2
CONV_W = 4
CONV_LEFT = 2
RG_C = 8.0
D_POOL = 256
POOL_WINDOWS = (2, 4, 8, 16)
POOL_C = D_POOL // len(POOL_WINDOWS)
D_FF = 2816
N_EXPERTS = 8
EPS = 1e-6
NEG = -1e30

LANES = 128
SUBLANES = 8
VMEM_LIMIT = 56 * 1024 * 1024
Q_ROWS = 4
KEY_ROWS = Q_ROWS + WIN_R - 1
HALO = 8
FF_CHUNKS = 2
FF_CHUNK = D_FF // FF_CHUNKS
MOE_SRC = 256
MOE_TILE = 256
MOE_RING = 8
MOE_RING_SIZES = (4, 5, 6, MOE_RING)
MOE_CTILE = 128
MOE_CMAX = (MOE_SRC - 1 + MOE_CTILE - 1) // MOE_CTILE + 1


def _cparams(sem):
    return pltpu.CompilerParams(dimension_semantics=sem, vmem_limit_bytes=VMEM_LIMIT)


def _const_spec(shape):
    nd = len(shape)
    return pl.BlockSpec(shape, lambda *_: (0,) * nd, pipeline_mode=pl.Buffered(1))


def _dot(a, b):
    return jnp.dot(a, b, preferred_element_type=F32)


def _dot_nt(a, b):
    return lax.dot_general(a, b, (((1,), (1,)), ((), ())), preferred_element_type=F32)


def _rms(x, g):
    ms = jnp.mean(x * x, axis=-1, keepdims=True)
    return x * lax.rsqrt(ms + EPS) * g


def _pre(x, g, shift, scale):
    return _rms(x, g) * (1.0 + scale) + shift


def _gelu(x):
    return jax.nn.gelu(x, approximate=True)


def _sigmoid(x):
    return 1.0 / (1.0 + jnp.exp(-x))


def _silu(x):
    return x * _sigmoid(x)


def _mod_kernel(c_ref, w_ref, b_ref, o_ref):
    s = _silu(c_ref[...])
    o_ref[0] = jnp.dot(s, w_ref[0], preferred_element_type=F32,
                       precision=lax.Precision.HIGHEST) + b_ref[0]


def _modulation(cond, w_ada, b_ada):
    depth, d, n = w_ada.shape
    rows = cond.shape[0]
    bn = 768
    return pl.pallas_call(
        _mod_kernel,
        grid=(depth, n // bn),
        in_specs=[
            pl.BlockSpec((rows, d), lambda i, j: (0, 0)),
            pl.BlockSpec((1, d, bn), lambda i, j: (i, 0, j)),
            pl.BlockSpec((1, 1, bn), lambda i, j: (i, 0, j)),
        ],
        out_specs=pl.BlockSpec((1, rows, bn), lambda i, j: (i, 0, j)),
        out_shape=jax.ShapeDtypeStruct((depth, rows, n), F32),
        compiler_params=_cparams(("arbitrary", "arbitrary")),
        name="adaln_modulation",
    )(cond, w_ada, b_ada.reshape(depth, 1, n))


def _l0_in_kernel(x_ref, mod_ref, g_ref, w_ref, ws_ref, bs_ref, *out_refs, tm, emit_f32):
    q_ref, k_ref, v_ref, sg_ref = out_refs[:4]
    x = x_ref[...]
    h = _pre(x, g_ref[0:1, :], mod_ref[0, 0:1, :], mod_ref[0, 1:2, :]).astype(BF16)
    z = _dot(h, w_ref[...])
    q_ref[...] = z[:, 0:W_A].astype(BF16)
    k_ref[...] = z[:, W_A:2 * W_A].astype(BF16)
    v_ref[...] = z[:, 2 * W_A:3 * W_A].astype(BF16)
    if emit_f32:
        out_refs[4][...] = z[:, W_A:2 * W_A]
        out_refs[5][...] = z[:, 2 * W_A:3 * W_A]
    u = _gelu(z[:, 3 * W_A:3 * W_A + W_B])
    gf = _gelu(z[:, 3 * W_A + W_B:])
    n_chunks = tm // CHUNK
    for gi in range(N_GROUPS_B):
        gg = gf[:, gi * LANES:(gi + 1) * LANES]
        mu = jnp.mean(gg, axis=-1, keepdims=True)
        dd = gg - mu
        var = jnp.mean(dd * dd, axis=-1, keepdims=True)
        gn = (dd * lax.rsqrt(var + EPS)).astype(BF16)
        rhs = jnp.concatenate([gn[c * CHUNK:(c + 1) * CHUNK, :] for c in range(n_chunks)], axis=1)
        s = _dot(ws_ref[gi], rhs) + bs_ref[:, gi:gi + 1]
        for c in range(n_chunks):
            uu = u[c * CHUNK:(c + 1) * CHUNK, gi * LANES:(gi + 1) * LANES]
            sg_ref[c * CHUNK:(c + 1) * CHUNK, gi * LANES:(gi + 1) * LANES] = (
                uu * s[:, c * LANES:(c + 1) * LANES]).astype(BF16)


def _l0_in(x, mod, g, w_in, w_s, b_s_t, *, tm, rows_per_batch, emit_f32):
    rows = x.shape[0]
    tpb = rows_per_batch // tm
    nb = mod.shape[0]
    d_in = w_in.shape[1]
    row_spec = lambda w: pl.BlockSpec((tm, w), lambda i: (i, 0))
    out_shape = [jax.ShapeDtypeStruct((rows, W_A), BF16)] * 3 + [jax.ShapeDtypeStruct((rows, W_B), BF16)]
    out_specs = [row_spec(W_A)] * 3 + [row_spec(W_B)]
    if emit_f32:
        out_shape += [jax.ShapeDtypeStruct((rows, W_A), F32)] * 2
        out_specs += [row_spec(W_A)] * 2
    return pl.pallas_call(
        functools.partial(_l0_in_kernel, tm=tm, emit_f32=emit_f32),
        grid=(rows // tm,),
        in_specs=[
            row_spec(D_MODEL),
            pl.BlockSpec((1, 6, D_MODEL), lambda i: ((i // tpb) % nb, 0, 0)),
            _const_spec((4, D_MODEL)),
            _const_spec((D_MODEL, d_in)),
            _const_spec((N_GROUPS_B, CHUNK, CHUNK)),
            _const_spec((CHUNK, N_GROUPS_B)),
        ],
        out_specs=out_specs,
        out_shape=out_shape,
        compiler_params=_cparams(("arbitrary",)),
        name="l0_in_proj_sgu",
    )(x, mod, g, w_in, w_s, b_s_t)


def _attn_kernel(*refs, n_local, has_ctx, rows):
    if has_ctx:
        q_ref, k_ref, v_ref, kc_ref, vc_ref, bias_ref, o_ref = refs
        r0 = pl.program_id(1) * Q_ROWS
        ks = jnp.clip(r0 - WIN_R // 2, 0, rows - KEY_ROWS)
        start = pl.multiple_of(ks * GRID_W, GRID_W)
        local = pl.ds(start, n_local)
    else:
        q_ref, k_ref, v_ref, o_ref = refs
        local = slice(None)
    first = lax.broadcasted_iota(jnp.int32, (1, LANES), 1) < HEAD_DIM
    for hp in range(W_A // LANES):
        lanes = slice(hp * LANES, (hp + 1) * LANES)
        kl = k_ref[local, lanes]
        vl = v_ref[local, lanes]
        q = q_ref[:, lanes] * jnp.asarray(HEAD_DIM ** -0.5, BF16)
        outs = []
        for half in range(2):
            qh = jnp.where(first if half == 0 else jnp.logical_not(first), q, jnp.zeros_like(q))
            s = _dot_nt(qh, kl)
            if has_ctx:
                s = s + bias_ref[0, 2 * hp + half]
                sc = _dot_nt(qh, kc_ref[:, lanes])
                m = jnp.maximum(jnp.max(s, axis=-1, keepdims=True), jnp.max(sc, axis=-1, keepdims=True))
            else:
                m = jnp.max(s, axis=-1, keepdims=True)
            p = jnp.exp(s - m)
            l = jnp.sum(p, axis=-1, keepdims=True)
            acc = _dot(p.astype(BF16), vl)
            if has_ctx:
                pc = jnp.exp(sc - m)
                l = l + jnp.sum(pc, axis=-1, keepdims=True)
                acc = acc + _dot(pc.astype(BF16), vc_ref[:, lanes])
            outs.append(acc / l)
        o_ref[:, lanes] = jnp.where(first, outs[0], outs[1]).astype(BF16)


def _attn_ctx(q, k, v, *, n_batch, n_seq):
    rows = q.shape[0]
    spec = pl.BlockSpec((n_seq, W_A), lambda b: (b, 0))
    return pl.pallas_call(
        functools.partial(_attn_kernel, n_local=n_seq, has_ctx=False, rows=0),
        grid=(n_batch,),
        in_specs=[spec, spec, spec],
        out_specs=spec,
        out_shape=jax.ShapeDtypeStruct((rows, W_A), BF16),
        compiler_params=_cparams(("arbitrary",)),
        name="attn_context",
    )(q, k, v)


def _attn_latent(q, k, v, kc, vc, bias, *, n_batch, n_seq, n_ctx):
    rows = n_seq // GRID_W
    n_rg = rows // Q_ROWS
    tq = Q_ROWS * GRID_W
    n_local = KEY_ROWS * GRID_W
    img_spec = pl.BlockSpec((n_seq, W_A), lambda b, rg: (b, 0))
    ctx_spec = pl.BlockSpec((n_ctx, W_A), lambda b, rg: (b, 0))
    q_spec = pl.BlockSpec((tq, W_A), lambda b, rg: (b * n_rg + rg, 0))

    def bias_map(b, rg):
        cfg = jnp.where(rg == 0, 0, jnp.where(rg == n_rg - 1, 2, 1))
        return (cfg, 0, 0, 0)

    return pl.pallas_call(
        functools.partial(_attn_kernel, n_local=n_local, has_ctx=True, rows=rows),
        grid=(n_batch, n_rg),
        in_specs=[q_spec, img_spec, img_spec, ctx_spec, ctx_spec,
                  pl.BlockSpec((1, N_HEADS, tq, n_local), bias_map)],
        out_specs=q_spec,
        out_shape=jax.ShapeDtypeStruct((n_batch * n_seq, W_A), BF16),
        compiler_params=_cparams(("arbitrary", "arbitrary")),
        name="attn_latent",
    )(q, k, v, kc, vc, bias)


def _latent_bias_table(rpb, rows):
    n_heads = rpb.shape[0]
    qcol = np.arange(GRID_W)[:, None]
    kcol = np.arange(GRID_W)[None, :]
    cs = np.clip(qcol - WIN_C // 2, 0, GRID_W - WIN_C)
    col_valid = (kcol >= cs) & (kcol < cs + WIN_C)
    dc = np.clip(kcol - qcol + (WIN_C - 1), 0, 2 * WIN_C - 2)
    pick = (dc[:, :, None] == np.arange(2 * WIN_C - 1)).astype(np.float32)
    col_tab = jnp.einsum("hrc,qkc->hrqk", rpb.astype(F32), pick, precision=lax.Precision.HIGHEST)
    col_tab = jnp.where(col_valid[None, None], col_tab, NEG)
    tables = []
    for r0 in (0, 2 * Q_ROWS, rows - Q_ROWS):
        ks = min(max(r0 - WIN_R // 2, 0), rows - KEY_ROWS)
        qrow = r0 + np.arange(Q_ROWS)[:, None]
        krow = ks + np.arange(KEY_ROWS)[None, :]
        rs = np.clip(qrow - WIN_R // 2, 0, rows - WIN_R)
        row_valid = (krow >= rs) & (krow < rs + WIN_R)
        dr = np.clip(krow - qrow + (WIN_R - 1), 0, 2 * WIN_R - 2)
        blocks = [jnp.stack([col_tab[:, dr[i, k]] if row_valid[i, k]
                             else jnp.full((n_heads, GRID_W, GRID_W), NEG, F32)
                             for k in range(KEY_ROWS)], axis=2)
                  for i in range(Q_ROWS)]
        b = jnp.stack(blocks, axis=1)
        tables.append(b.reshape(n_heads, Q_ROWS * GRID_W, KEY_ROWS * GRID_W))
    return jnp.stack(tables)


def _out_kernel(*refs, n_in):
    in_refs = refs[:n_in]
    x_ref, mod_ref, g_ref, w_ref, rw_ref, x1_ref, h2_ref, gates_ref, rankc_ref, rankr_ref = refs[n_in:]
    y = None
    off = 0
    for r in in_refs:
        w = r.shape[1]
        part = _dot(r[...], w_ref[off:off + w, :])
        y = part if y is None else y + part
        off += w
    x1 = x_ref[...] + mod_ref[0, 2:3, :] * _rms(y, g_ref[1:2, :])
    x1_ref[...] = x1
    h2 = _pre(x1, g_ref[2:3, :], mod_ref[0, 3:4, :], mod_ref[0, 4:5, :])
    h2_ref[...] = h2.astype(BF16)
    for sb in range(h2.shape[0] // MOE_SRC):
        blk = slice(sb * MOE_SRC, (sb + 1) * MOE_SRC)
        hs = h2[blk, :]
        h_hi = hs.astype(BF16)
        h_lo = (hs - h_hi.astype(F32)).astype(BF16)
        rw = rw_ref[...]
        r_hi = rw.astype(BF16)
        r_lo = (rw - r_hi.astype(F32)).astype(BF16)
        logits = _dot_nt(r_hi, h_hi) + (_dot_nt(r_hi, h_lo) + _dot_nt(r_lo, h_hi))
        tm = MOE_SRC
        row = lax.broadcasted_iota(jnp.int32, logits.shape, 0).astype(F32)
        logits = jnp.where(row < N_EXPERTS, logits, -jnp.inf)
        m1 = jnp.max(logits, axis=0, keepdims=True)
        i1 = jnp.min(jnp.where(logits == m1, row, float(LANES)), axis=0, keepdims=True)
        rest_l = jnp.where(row == i1, -jnp.inf, logits)
        m2 = jnp.max(rest_l, axis=0, keepdims=True)
        i2 = jnp.min(jnp.where(rest_l == m2, row, float(LANES)), axis=0, keepdims=True)
        e2 = jnp.exp(m2 - m1)
        w1 = 1.0 / (1.0 + e2)
        w2 = e2 / (1.0 + e2)
        sel1 = row == i1
        sel2 = row == i2
        member = jnp.where(sel1, 1.0, 0.0) + jnp.where(sel2, 1.0, 0.0)
        gates = jnp.where(sel1, w1, 0.0) + jnp.where(sel2, w2, 0.0)
        before = (lax.broadcasted_iota(jnp.int32, (tm, tm), 0)
                  < lax.broadcasted_iota(jnp.int32, (tm, tm), 1))
        rank = _dot(member.astype(BF16), jnp.where(before, 1.0, 0.0).astype(BF16))
        rank = jnp.where(member > 0.0, rank, -1.0)
        gates_ref[blk, :] = gates.T
        rankc_ref[blk, :] = rank.T.astype(jnp.int32)
        rankr_ref[:, blk] = rank[0:N_EXPERTS, :].astype(jnp.int32)


def _out_proj(ins, x, mod, g, w_out, router_w, *, tm, rows_per_batch):
    rows = x.shape[0]
    tpb = rows_per_batch // tm
    nb = mod.shape[0]
    row_spec = lambda w: pl.BlockSpec((tm, w), lambda i: (i, 0))
    in_specs = [row_spec(a.shape[1]) for a in ins] + [
        row_spec(D_MODEL),
        pl.BlockSpec((1, 6, D_MODEL), lambda i: ((i // tpb) % nb, 0, 0)),
        _const_spec((4, D_MODEL)),
        _const_spec((D_MODEL, D_MODEL)),
        _const_spec((LANES, D_MODEL)),
    ]
    out_shape = [jax.ShapeDtypeStruct((rows, D_MODEL), F32), jax.ShapeDtypeStruct((rows, D_MODEL), BF16),
                 jax.ShapeDtypeStruct((rows, LANES), F32),
                 jax.ShapeDtypeStruct((rows, LANES), jnp.int32),
                 jax.ShapeDtypeStruct((N_EXPERTS, rows), jnp.int32)]
    out_specs = [row_spec(D_MODEL), row_spec(D_MODEL), row_spec(LANES), row_spec(LANES),
                 pl.BlockSpec((N_EXPERTS, tm), lambda i: (0, i))]
    return pl.pallas_call(
        functools.partial(_out_kernel, n_in=len(ins)),
        grid=(rows // tm,),
        in_specs=in_specs,
        out_specs=out_specs,
        out_shape=out_shape,
        compiler_params=_cparams(("arbitrary",)),
        name="out_proj_norms_router",
    )(*ins, x, mod, g, w_out, router_w)


def _l0_tail_kernel(oa_ref, sg_ref, x_ref, mod_ref, g_ref, wo_ref, wg_ref, wu_ref, wd_ref,
                    mod1_ref, g1_ref, wi_ref, x2_ref, gate_ref, xr_ref, xq_ref, x1_scr, h2_scr, acc_ref):
    fc = pl.program_id(1)

    @pl.when(fc == 0)
    def _():
        y = _dot(oa_ref[...], wo_ref[0:W_A, :]) + _dot(sg_ref[...], wo_ref[W_A:, :])
        x1 = x_ref[...] + mod_ref[0, 2:3, :] * _rms(y, g_ref[1:2, :])
        x1_scr[...] = x1
        h2_scr[...] = _pre(x1, g_ref[2:3, :], mod_ref[0, 3:4, :], mod_ref[0, 4:5, :]).astype(BF16)
        acc_ref[...] = jnp.zeros_like(acc_ref)

    h = h2_scr[...]
    t = _silu(_dot(h, wg_ref[...])) * _dot(h, wu_ref[...])
    acc_ref[...] += _dot(t.astype(BF16), wd_ref[...])

    @pl.when(fc == FF_CHUNKS - 1)
    def _():
        x2 = x1_scr[...] + mod_ref[0, 5:6, :] * _rms(acc_ref[...], g_ref[3:4, :])
        x2_ref[...] = x2
        hn = _pre(x2, g1_ref[0:1, :], mod1_ref[0, 0:1, :], mod1_ref[0, 1:2, :]).astype(BF16)
        z = _dot(hn, wi_ref[...])
        gate_ref[...] = z[:, 0:D_RNN].astype(BF16)
        xr_ref[...] = z[:, D_RNN:2 * D_RNN]
        xq_ref[...] = z[:, 2 * D_RNN:]


def _l0_tail(oa, sg, x, mod, g, w_out, wg, wu, wd, mod1, g1, w_in1, *, tm, rows_per_batch):
    rows = x.shape[0]
    tpb = rows_per_batch // tm
    nb = mod.shape[0]
    row_spec = lambda w: pl.BlockSpec((tm, w), lambda i, f: (i, 0))
    mod_spec = pl.BlockSpec((1, 6, D_MODEL), lambda i, f: ((i // tpb) % nb, 0, 0))
    return pl.pallas_call(
        _l0_tail_kernel,
        grid=(rows // tm, FF_CHUNKS),
        in_specs=[row_spec(W_A), row_spec(W_B), row_spec(D_MODEL), mod_spec,
                  _const_spec((4, D_MODEL)), _const_spec((W_A + W_B, D_MODEL)),
                  pl.BlockSpec((D_MODEL, FF_CHUNK), lambda i, f: (0, f)),
                  pl.BlockSpec((D_MODEL, FF_CHUNK), lambda i, f: (0, f)),
                  pl.BlockSpec((FF_CHUNK, D_MODEL), lambda i, f: (f, 0)),
                  mod_spec, _const_spec((4, D_MODEL)), _const_spec((D_MODEL, w_in1.shape[1]))],
        out_specs=[row_spec(D_MODEL), row_spec(D_RNN), row_spec(D_RNN), row_spec(D_POOL)],
        out_shape=[jax.ShapeDtypeStruct((rows, D_MODEL), F32),
                   jax.ShapeDtypeStruct((rows, D_RNN), BF16),
                   jax.ShapeDtypeStruct((rows, D_RNN), F32),
                   jax.ShapeDtypeStruct((rows, D_POOL), F32)],
        scratch_shapes=[pltpu.VMEM((tm, D_MODEL), F32), pltpu.VMEM((tm, D_MODEL), BF16),
                        pltpu.VMEM((tm, D_MODEL), F32)],
        compiler_params=_cparams(("arbitrary", "arbitrary")),
        name="l0_tail_l1_in_proj",
    )(oa, sg, x, mod, g, w_out, wg, wu, wd, mod1, g1, w_in1)


def _moe_plan(rank_row):
    n_tok = rank_row.shape[1]
    n_src = n_tok // MOE_SRC
    n_tiles = 2 * n_tok // MOE_TILE + N_EXPERTS
    member = rank_row >= 0
    cnt_blk = member.reshape(N_EXPERTS, n_src, MOE_SRC).sum(-1).astype(jnp.int32)
    cum = jnp.concatenate([jnp.zeros((N_EXPERTS, 1), jnp.int32), jnp.cumsum(cnt_blk, axis=1)], axis=1)
    cnt = cum[:, -1]
    tiles_e = (cnt + MOE_TILE - 1) // MOE_TILE
    tile_end = jnp.cumsum(tiles_e)
    tile_base = tile_end - tiles_e
    slot0 = tile_base[:, None] * MOE_TILE + cum[:, :-1]
    pos_row = jnp.where(member, jnp.repeat(slot0, MOE_SRC, axis=1) + rank_row, -1)
    pos_row = jnp.pad(pos_row, ((0, 0), (0, MOE_RING * MOE_SRC)), constant_values=-1)
    d = jnp.arange(n_tiles, dtype=jnp.int32)
    tile_e = jnp.minimum(jnp.sum(d[:, None] >= tile_end[None, :], axis=1), N_EXPERTS - 1).astype(jnp.int32)
    valid = d < tile_end[-1]
    lo_slot = (d - tile_base[tile_e]) * MOE_TILE
    hi_slot = jnp.minimum(lo_slot + MOE_TILE, cnt[tile_e])
    cum_d = cum[tile_e]
    tile_lo = jnp.sum(cum_d[:, 1:] <= lo_slot[:, None], axis=1).astype(jnp.int32)
    tile_hi = jnp.sum(cum_d[:, :-1] < hi_slot[:, None], axis=1).astype(jnp.int32) - 1
    tile_n = jnp.where(valid, tile_hi - tile_lo + 1, 0).astype(jnp.int32)
    tile_lo = jnp.where(valid, tile_lo, 0)
    first = slot0.T
    c_t0 = first // MOE_CTILE
    c_nt = jnp.where(cnt_blk.T > 0, (first + cnt_blk.T - 1) // MOE_CTILE - c_t0 + 1, 0)
    flat = lambda a: a.reshape(-1).astype(jnp.int32)
    return pos_row, tile_e, tile_lo, tile_n, flat(c_t0), flat(c_nt), flat(first), n_tiles


def _moe_ffn_kernel(te_ref, tl_ref, tn_ref, pos_ref, h_hbm, wg_ref, wu_ref, wd_ref, o_ref,
                    hbuf, sem, acc_ref, *, n_tiles):
    d = pl.program_id(0)
    e = te_ref[d]
    lo = tl_ref[d]
    n = tn_ref[d]

    def copy(blk, slot):
        start = pl.multiple_of(blk * MOE_SRC, MOE_SRC)
        return pltpu.make_async_copy(h_hbm.at[pl.ds(start, MOE_SRC), :], hbuf.at[slot], sem.at[slot])

    def start_first(first_blk, count):
        for k in range(MOE_RING):
            @pl.when(k < count)
            def _(k=k):
                copy(first_blk + k, k).start()

    @pl.when(d == 0)
    def _():
        hbuf[...] = jnp.zeros_like(hbuf)
        start_first(lo, n)

    slot_ids = d * MOE_TILE + lax.broadcasted_iota(jnp.int32, (MOE_TILE, 1), 0)

    def picks(first_blk, n_blk):
        start = pl.multiple_of(first_blk * MOE_SRC, MOE_SRC)
        pos = pos_ref[pl.ds(e, 1), pl.ds(start, n_blk * MOE_SRC)]
        return jnp.where(pos == slot_ids, 1.0, 0.0).astype(BF16)

    for k in range(MOE_RING):
        @pl.when(k < n)
        def _(k=k):
            copy(lo + k, k).wait()

    prev = 0
    for n_blk in MOE_RING_SIZES:
        last = n_blk == MOE_RING_SIZES[-1]

        @pl.when((n > prev) if last else ((n > prev) & (n <= n_blk)))
        def _(n_blk=n_blk):
            acc_ref[...] = _dot(picks(lo, n_blk), hbuf[0:n_blk].reshape(n_blk * MOE_SRC, D_MODEL))

        prev = n_blk

    def extra(j, carry):
        cp = copy(lo + j, 0)
        cp.start()
        cp.wait()
        acc_ref[...] += _dot(picks(lo + j, 1), hbuf[0])
        return carry

    lax.fori_loop(MOE_RING, n, extra, 0)

    @pl.when(d + 1 < n_tiles)
    def _():
        start_first(tl_ref[d + 1], tn_ref[d + 1])

    @pl.when(n > 0)
    def _():
        x = acc_ref[...].astype(BF16)
        y = None
        for fc in range(FF_CHUNKS):
            cols = slice(fc * FF_CHUNK, (fc + 1) * FF_CHUNK)
            t = _silu(_dot(x, wg_ref[0, :, cols])) * _dot(x, wu_ref[0, :, cols])
            part = _dot(t.astype(BF16), wd_ref[0, cols, :])
            y = part if y is None else y + part
        o_ref[...] = y.astype(BF16)

    @pl.when(n == 0)
    def _():
        o_ref[...] = jnp.zeros_like(o_ref)


def _moe_ffn(h2, pos_row, tile_e, tile_lo, tile_n, wg, wu, wd, *, n_tiles):
    n_tok = h2.shape[0]
    w_spec = lambda shape: pl.BlockSpec((1,) + shape, lambda d, te, tl, tn: (te[d], 0, 0))
    grid_spec = pltpu.PrefetchScalarGridSpec(
        num_scalar_prefetch=3,
        grid=(n_tiles,),
        in_specs=[
            pl.BlockSpec(pos_row.shape, lambda d, te, tl, tn: (0, 0), pipeline_mode=pl.Buffered(1)),
            pl.BlockSpec(memory_space=pl.ANY),
            w_spec((D_MODEL, D_FF)), w_spec((D_MODEL, D_FF)), w_spec((D_FF, D_MODEL)),
        ],
        out_specs=pl.BlockSpec((MOE_TILE, D_MODEL), lambda d, te, tl, tn: (d, 0)),
        scratch_shapes=[pltpu.VMEM((MOE_RING, MOE_SRC, D_MODEL), BF16),
                        pltpu.SemaphoreType.DMA((MOE_RING,)),
                        pltpu.VMEM((MOE_TILE, D_MODEL), F32)],
    )
    return pl.pallas_call(
        functools.partial(_moe_ffn_kernel, n_tiles=n_tiles),
        grid_spec=grid_spec,
        out_shape=jax.ShapeDtypeStruct((n_tiles * MOE_TILE, D_MODEL), BF16),
        compiler_params=_cparams(("arbitrary",)),
        name="moe_dispatch_swiglu",
    )(tile_e, tile_lo, tile_n, pos_row, h2, wg, wu, wd)


def _moe_combine_kernel(t0_ref, nt_ref, first_ref, rank_ref, gates_ref, ys_hbm, x1_ref, mod_ref, g_ref,
                        o_ref, ybuf, sem):
    s = pl.program_id(0)

    def copy(blk, e, k):
        start = pl.multiple_of((t0_ref[blk * N_EXPERTS + e] + k) * MOE_CTILE, MOE_CTILE)
        buf = (blk % 2) * N_EXPERTS + e
        return pltpu.make_async_copy(ys_hbm.at[pl.ds(start, MOE_CTILE), :],
                                     ybuf.at[buf, pl.ds(k * MOE_CTILE, MOE_CTILE), :],
                                     sem.at[buf * MOE_CMAX + k])

    def start_block(blk):
        for e in range(N_EXPERTS):
            for k in range(MOE_CMAX):
                @pl.when(nt_ref[blk * N_EXPERTS + e] > k)
                def _(e=e, k=k):
                    copy(blk, e, k).start()

    @pl.when(s == 0)
    def _():
        ybuf[...] = jnp.zeros_like(ybuf)
        start_block(s)

    @pl.when(s + 1 < pl.num_programs(0))
    def _():
        start_block(s + 1)

    most = nt_ref[s * N_EXPERTS]
    for e in range(N_EXPERTS):
        most = jnp.maximum(most, nt_ref[s * N_EXPERTS + e])
        for k in range(MOE_CMAX):
            @pl.when(nt_ref[s * N_EXPERTS + e] > k)
            def _(e=e, k=k):
                copy(s, e, k).wait()

    def combine(n_t):
        lane = lax.broadcasted_iota(jnp.int32, (1, n_t * MOE_CTILE), 1)
        y = None
        for e in range(N_EXPERTS):
            rank = rank_ref[:, e:e + 1]
            pos = jnp.where(rank >= 0, rank + first_ref[s * N_EXPERTS + e], -1)
            base = t0_ref[s * N_EXPERTS + e] * MOE_CTILE
            pick = jnp.where(pos == base + lane, 1.0, 0.0).astype(BF16)
            part = gates_ref[:, e:e + 1] * _dot(
                pick, ybuf[(s % 2) * N_EXPERTS + e, 0:n_t * MOE_CTILE, :])
            y = part if y is None else y + part
        o_ref[...] = x1_ref[...] + mod_ref[0, 5:6, :] * _rms(y, g_ref[3:4, :])

    @pl.when(most < MOE_CMAX)
    def _():
        combine(MOE_CMAX - 1)

    @pl.when(most >= MOE_CMAX)
    def _():
        combine(MOE_CMAX)


def _moe_combine(ys, rank_col, gates, c_t0, c_nt, c_first, x1, mod, g, *, rows_per_batch):
    n_tok = x1.shape[0]
    tpb = rows_per_batch // MOE_SRC
    nb = mod.shape[0]
    row_spec = lambda w: pl.BlockSpec((MOE_SRC, w), lambda s, *_: (s, 0))
    grid_spec = pltpu.PrefetchScalarGridSpec(
        num_scalar_prefetch=3,
        grid=(n_tok // MOE_SRC,),
        in_specs=[
            row_spec(LANES), row_spec(LANES),
            pl.BlockSpec(memory_space=pl.ANY),
            row_spec(D_MODEL),
            pl.BlockSpec((1, 6, D_MODEL), lambda s, *_: ((s // tpb) % nb, 0, 0)),
            pl.BlockSpec((4, D_MODEL), lambda s, *_: (0, 0), pipeline_mode=pl.Buffered(1)),
        ],
        out_specs=row_spec(D_MODEL),
        scratch_shapes=[pltpu.VMEM((2 * N_EXPERTS, MOE_CMAX * MOE_CTILE, D_MODEL), BF16),
                        pltpu.SemaphoreType.DMA((2 * N_EXPERTS * MOE_CMAX,))],
    )
    return pl.pallas_call(
        _moe_combine_kernel,
        grid_spec=grid_spec,
        out_shape=jax.ShapeDtypeStruct((n_tok, D_MODEL), F32),
        compiler_params=_cparams(("arbitrary",)),
        name="moe_combine_post",
    )(c_t0, c_nt, c_first, rank_col, gates, ys, x1, mod, g)


def _to_time_major(x_ref, prev_ref, next_ref, dst, t_idx, n_tiles, tt):
    nb = SUBLANES
    for bi in range(nb):
        prev = jnp.where(t_idx > 0, prev_ref[bi, 0], 0.0)
        nxt = jnp.where(t_idx < n_tiles - 1, next_ref[bi, 0], 0.0)
        x = x_ref[bi]
        for lc in range(dst.shape[0]):
            lanes = slice(lc * LANES, (lc + 1) * LANES)
            dst[lc, pl.ds(bi, HALO, stride=nb), :] = prev[:, lanes]
            dst[lc, pl.ds(HALO * nb + bi, tt, stride=nb), :] = x[:, lanes]
            dst[lc, pl.ds((HALO + tt) * nb + bi, HALO, stride=nb), :] = nxt[:, lanes]


def _scan_kernel(*refs, fwd, tt, n_tiles, n_seq):
    if fwd:
        (xr_ref, xrp_ref, xrn_ref, xq_ref, xqp_ref, xqn_ref, cw_ref, cb_ref, wgt_ref, ba_ref, bx_ref,
         lam_ref, h0_ref, pw_ref, ps_ref, hf_ref, yp_ref, hl_ref,
         xt_scr, a_scr, b_scr, h_scr, carry, qt_scr, yp_scr) = refs
    else:
        (xr_ref, xrp_ref, xrn_ref, cw_ref, cb_ref, wgt_ref, ba_ref, bx_ref, lam_ref, h0_ref,
         hf_ref, gate_ref, yr_ref, hl_ref, xt_scr, a_scr, b_scr, h_scr, carry) = refs
    i = pl.program_id(1)
    t_idx = i if fwd else n_tiles - 1 - i
    nb = SUBLANES
    n_lc = D_RNN // LANES
    rows = tt * nb

    _to_time_major(xr_ref, xrp_ref, xrn_ref, xt_scr, t_idx, n_tiles, tt)
    chunks = []
    for lc in range(n_lc):
        lanes = slice(lc * LANES, (lc + 1) * LANES)
        acc = cb_ref[:, lanes]
        for j in range(CONV_W):
            acc = acc + cw_ref[j:j + 1, lanes] * xt_scr[lc, pl.ds((HALO + j - CONV_LEFT) * nb, rows), :]
        chunks.append(acc)
    xc = jnp.concatenate(chunks, axis=1)
    xb = xc.astype(BF16)
    r0 = _dot(xb[:, :RNN_HALF], wgt_ref[0])
    r1 = _dot(xb[:, RNN_HALF:], wgt_ref[1])
    ra = jnp.concatenate([r0[:, :RNN_HALF], r1[:, :RNN_HALF]], axis=1) + ba_ref[...]
    ri = jnp.concatenate([r0[:, RNN_HALF:], r1[:, RNN_HALF:]], axis=1) + bx_ref[...]
    nl = -lam_ref[...]
    softplus = jnp.maximum(nl, 0.0) + jnp.log1p(jnp.exp(-jnp.abs(nl)))
    a = jnp.exp((-0.5 * RG_C * softplus) * (1.0 + jnp.tanh(0.5 * ra)))
    b = jnp.sqrt(1.0 - a * a) * ((0.5 + 0.5 * jnp.tanh(0.5 * ri)) * xc)
    for lc in range(n_lc):
        a_scr[lc] = a[:, lc * LANES:(lc + 1) * LANES]
        b_scr[lc] = b[:, lc * LANES:(lc + 1) * LANES]

    @pl.when(i == 0)
    def _():
        for lc in range(n_lc):
            carry[lc] = h0_ref[:, lc * LANES:(lc + 1) * LANES]

    def step(s, h):
        t = s if fwd else tt - 1 - s
        row = pl.multiple_of(t * nb, nb)
        h = a_scr[:, pl.ds(row, nb), :] * h + b_scr[:, pl.ds(row, nb), :]
        h_scr[:, pl.ds(row, nb), :] = h
        return h

    h_last = lax.fori_loop(0, tt, step, carry[...], unroll=8)
    carry[...] = h_last
    for lc in range(n_lc):
        hl_ref[:, lc * LANES:(lc + 1) * LANES] = h_last[lc]

    def unscan(bi):
        return jnp.concatenate([h_scr[lc, pl.ds(bi, tt, stride=nb), :] for lc in range(n_lc)], axis=1)

    if fwd:
        for bi in range(nb):
            hf_ref[bi] = unscan(bi)
        _to_time_major(xq_ref, xqp_ref, xqn_ref, qt_scr, t_idx, n_tiles, tt)
        length = tt + 2 * HALO
        lane = lax.broadcasted_iota(jnp.int32, (1, LANES), 1)
        tg = t_idx * tt + lax.broadcasted_iota(jnp.int32, (rows, 1), 0) // nb
        deltas = []
        for lc in range(D_POOL // LANES):
            e = qt_scr[lc]
            sums = {1: e}
            for w in POOL_WINDOWS:
                p = sums[w // 2]
                n_out = length - w + 1
                sums[w] = p[0:n_out * nb] + p[(w // 2) * nb:(n_out + w // 2) * nb]
            w_a, w_b = POOL_WINDOWS[2 * lc], POOL_WINDOWS[2 * lc + 1]
            first = lane < POOL_C
            pick = lambda w: sums[w][(HALO - w // 2) * nb:(HALO - w // 2) * nb + rows]
            wsum = jnp.where(first, pick(w_a), pick(w_b))
            half = jnp.where(first, w_a // 2, w_b // 2)
            cnt = (jnp.minimum(tg + half, n_seq) - jnp.maximum(tg - half, 0)).astype(F32)
            deltas.append((wsum / cnt - e[HALO * nb:HALO * nb + rows]).astype(BF16))
        yp = _dot(jnp.concatenate(deltas, axis=1), pw_ref[...]) * ps_ref[...]
        for lc in range(D_POOL // LANES):
            yp_scr[lc] = yp[:, lc * LANES:(lc + 1) * LANES]
        for bi in range(nb):
            yp_ref[bi] = jnp.concatenate(
                [yp_scr[lc, pl.ds(bi, tt, stride=nb), :] for lc in range(D_POOL // LANES)],
                axis=1).astype(BF16)
    else:
        for bi in range(nb):
            hb = unscan(bi)
            yr_ref[bi] = ((hf_ref[bi] + hb) * _gelu(gate_ref[bi].astype(F32))).astype(BF16)


def _scan(fwd, xr, xq, conv_w, conv_b, w_gates, ba, bx, lam, h0, pool_w, pool_scale, h_f, gate, *, tt):
    n_batch, n_seq, _ = xr.shape
    n_bg = n_batch // SUBLANES
    n_tiles = n_seq // tt
    hb = tt // HALO
    n_hb = n_seq // HALO

    def t_of(i):
        return i if fwd else n_tiles - 1 - i

    def tile(c):
        return pl.BlockSpec((SUBLANES, tt, c), lambda b, i: (b, t_of(i), 0))

    def prev(c):
        return pl.BlockSpec((SUBLANES, 1, HALO, c),
                            lambda b, i: (b, jnp.maximum(t_of(i) * hb - 1, 0), 0, 0))

    def nxt(c):
        return pl.BlockSpec((SUBLANES, 1, HALO, c),
                            lambda b, i: (b, jnp.minimum((t_of(i) + 1) * hb, n_hb - 1), 0, 0))

    state = pl.BlockSpec((SUBLANES, D_RNN), lambda b, i: (b, 0))
    xr4 = xr.reshape(n_batch, n_hb, HALO, D_RNN)
    common = [_const_spec((CONV_W, D_RNN)), _const_spec((1, D_RNN)),
              _const_spec((2, RNN_HALF, D_RNN)), _const_spec((1, D_RNN)), _const_spec((1, D_RNN)),
              _const_spec((1, D_RNN)), state]
    common_args = [conv_w, conv_b, w_gates, ba, bx, lam, h0]
    n_lc = D_RNN // LANES
    ext_rows = (tt + 2 * HALO) * SUBLANES
    scratch = ([pltpu.VMEM((n_lc, ext_rows, LANES), F32)]
               + [pltpu.VMEM((n_lc, SUBLANES * tt, LANES), F32)] * 3
               + [pltpu.VMEM((n_lc, SUBLANES, LANES), F32)])
    if fwd:
        scratch += [pltpu.VMEM((D_POOL // LANES, ext_rows, LANES), F32),
                    pltpu.VMEM((D_POOL // LANES, SUBLANES * tt, LANES), F32)]
        xq4 = xq.reshape(n_batch, n_hb, HALO, D_POOL)
        in_specs = ([tile(D_RNN), prev(D_RNN), nxt(D_RNN), tile(D_POOL), prev(D_POOL), nxt(D_POOL)]
                    + common + [_const_spec((D_POOL, D_POOL)), _const_spec((1, D_POOL))])
        args = [xr, xr4, xr4, xq, xq4, xq4] + common_args + [pool_w, pool_scale]
        out_specs = [tile(D_RNN), tile(D_POOL), state]
        out_shape = [jax.ShapeDtypeStruct((n_batch, n_seq, D_RNN), F32),
                     jax.ShapeDtypeStruct((n_batch, n_seq, D_POOL), BF16),
                     jax.ShapeDtypeStruct((n_batch, D_RNN), F32)]
    else:
        in_specs = [tile(D_RNN), prev(D_RNN), nxt(D_RNN)] + common + [tile(D_RNN), tile(D_RNN)]
        args = [xr, xr4, xr4] + common_args + [h_f, gate]
        out_specs = [tile(D_RNN), state]
        out_shape = [jax.ShapeDtypeStruct((n_batch, n_seq, D_RNN), BF16),
                     jax.ShapeDtypeStruct((n_batch, D_RNN), F32)]
    return pl.pallas_call(
        functools.partial(_scan_kernel, fwd=fwd, tt=tt, n_tiles=n_tiles, n_seq=n_seq),
        grid=(n_bg, n_tiles),
        in_specs=in_specs,
        out_specs=out_specs,
        out_shape=out_shape,
        scratch_shapes=scratch,
        compiler_params=_cparams(("arbitrary", "arbitrary")),
        name="rglru_forward_pool" if fwd else "rglru_backward_combine",
    )(*args)


def _gate_weights(wa, wx):
    per_half = RNN_BLOCKS // 2

    def half_dense(w, k):
        blocks = [w[k * per_half + b] for b in range(per_half)]
        rows = []
        for bi, blk in enumerate(blocks):
            row = [blk if bj == bi else jnp.zeros_like(blk) for bj in range(per_half)]
            rows.append(jnp.concatenate(row, axis=1))
        return jnp.concatenate(rows, axis=0)

    return jnp.stack([jnp.concatenate([half_dense(wa, k), half_dense(wx, k)], axis=1)
                      for k in range(2)]).astype(BF16)


def _pool_weights(pool_w):
    n = pool_w.shape[0]
    rows = []
    for i in range(n):
        rows.append(jnp.concatenate([pool_w[i] if j == i else jnp.zeros_like(pool_w[i])
                                     for j in range(n)], axis=1))
    return jnp.concatenate(rows, axis=0).astype(BF16)


def kernel(x_prompt, x_sample, cache_k, cache_v, state_h, c, c_ctx, w_ada, b_ada, norm_g, w_in_even,
           w_out_even, na_rpb, sgu_w, sgu_b, w_in_odd, w_out_odd, conv_w, conv_b, rg_wa, rg_ba, rg_wx,
           rg_bx, rg_lam, pool_w, pool_scale, ffn_wg, ffn_wu, ffn_wd, router_w, moe_wg, moe_wu, moe_wd):
    bp, n_p, _ = x_prompt.shape
    bs, n_s, _ = x_sample.shape
    n_ctx = cache_k.shape[3]

    cond = jnp.concatenate([c_ctx[None, :], c, jnp.zeros((2 * SUBLANES - 1 - bs, D_MODEL), F32)], axis=0)
    mods = _modulation(cond, w_ada, b_ada).reshape(w_ada.shape[0], cond.shape[0], 6, D_MODEL)

    streams = [
        dict(x=x_prompt.reshape(bp * n_p, D_MODEL), nb=bp, n=n_p, tm=256, latent=False),
        dict(x=x_sample.reshape(bs * n_s, D_MODEL), nb=bs, n=n_s, tm=512, latent=True),
    ]
    new_k = new_v = new_h = None

    g = norm_g[0]
    w_in = w_in_even[0].astype(BF16)
    w_out = w_out_even[0].astype(BF16)
    w_s = sgu_w[0].astype(BF16)
    b_s_t = sgu_b[0].T
    wg, wu, wd = (w[0].astype(BF16) for w in (ffn_wg, ffn_wu, ffn_wd))
    w_in1 = w_in_odd[0].astype(BF16)
    bias = _latent_bias_table(na_rpb[0], n_s // GRID_W)
    kc = cache_k[:, 0].transpose(0, 2, 1, 3).reshape(bs * n_ctx, W_A).astype(BF16)
    vc = cache_v[:, 0].transpose(0, 2, 1, 3).reshape(bs * n_ctx, W_A).astype(BF16)
    for st in streams:
        mod = mods[0, 1:1 + bs] if st["latent"] else mods[0, 0:1]
        outs = _l0_in(st["x"], mod, g, w_in, w_s, b_s_t, tm=st["tm"], rows_per_batch=st["n"],
                      emit_f32=not st["latent"])
        q, k, v, sg = outs[:4]
        if st["latent"]:
            oa = _attn_latent(q, k, v, kc, vc, bias, n_batch=st["nb"], n_seq=st["n"], n_ctx=n_ctx)
        else:
            oa = _attn_ctx(q, k, v, n_batch=st["nb"], n_seq=st["n"])
            heads = lambda a: a.reshape(bp, n_p, N_HEADS, HEAD_DIM).transpose(0, 2, 1, 3)[:, None]
            new_k, new_v = heads(outs[4]), heads(outs[5])
        mod1 = mods[1, 1:1 + bs] if st["latent"] else mods[1, 0:1]
        st["x"], st["gate"], st["xr"], st["xq"] = _l0_tail(
            oa, sg, st["x"], mod, g, w_out, wg, wu, wd, mod1, norm_g[1], w_in1,
            tm=st["tm"], rows_per_batch=st["n"])

    g = norm_g[1]
    w_out = w_out_odd[0].astype(BF16)
    w_gates = [_gate_weights(rg_wa[0, d], rg_wx[0, d]) for d in range(2)]
    pw = _pool_weights(pool_w[0])
    ps = pool_scale[0][None, :]
    rw = jnp.pad(router_w[0].T, ((0, LANES - N_EXPERTS), (0, 0)))
    wg, wu, wd = (w[0].astype(BF16) for w in (moe_wg, moe_wu, moe_wd))
    for st in streams:
        mod = mods[1, 1:1 + bs] if st["latent"] else mods[1, 0:1]
        nb, n = st["nb"], st["n"]
        gate, xr, xq = (st[key].reshape(nb, n, -1) for key in ("gate", "xr", "xq"))
        h0 = state_h[:, 0].astype(F32) if st["latent"] else jnp.zeros((nb, 2, D_RNN), F32)
        scan_args = lambda d: (conv_w[0], conv_b[0][None, :], w_gates[d], rg_ba[0, d][None, :],
                               rg_bx[0, d][None, :], rg_lam[0, d][None, :], h0[:, d])
        h_f, y_pool, h_f_last = _scan(True, xr, xq, *scan_args(0), pw, ps, None, None, tt=128)
        y_rec, h_b_last = _scan(False, xr, None, *scan_args(1), None, None, h_f, gate, tt=128)
        if not st["latent"]:
            new_h = jnp.stack([h_f_last, h_b_last], axis=1)[:, None]
        x1, h2, gates, rank_col, rank_row = _out_proj(
            [y_rec.reshape(nb * n, D_RNN), y_pool.reshape(nb * n, D_POOL)],
            st["x"], mod, g, w_out, rw, tm=st["tm"], rows_per_batch=n)
        pos_row, tile_e, tile_lo, tile_n, c_t0, c_nt, c_first, n_tiles = _moe_plan(rank_row)
        ys = _moe_ffn(h2, pos_row, tile_e, tile_lo, tile_n, wg, wu, wd, n_tiles=n_tiles)
        st["x"] = _moe_combine(ys, rank_col, gates, c_t0, c_nt, c_first, x1, mod, g, rows_per_batch=n)

    y_prompt = streams[0]["x"].reshape(bp, n_p, D_MODEL)
    y_sample = streams[1]["x"].reshape(bs, n_s, D_MODEL)
    return (y_prompt, y_sample, new_k, new_v, new_h)
```

```python
import functools

import jax
import jax.numpy as jnp
import numpy as np
from jax import lax
from jax.experimental import pallas as pl
from jax.experimental.pallas import tpu as pltpu

F32 = jnp.float32
BF16 = jnp.bfloat16

D_MODEL = 1024
GRID_W = 64
HEAD_DIM = 64
N_HEADS = 8
W_A = N_HEADS * HEAD_DIM
WIN_R = 8
WIN_C = 16
N_GROUPS_B = 4
W_B = 512
CHUNK = 128
D_RNN = 768
RNN_BLOCKS = 8
RNN_BW = D_RNN // RNN_BLOCKS
RNN_HALF = D_RNN // 2
CONV_W = 4
CONV_LEFT = 2
RG_C = 8.0
D_POOL = 256
POOL_WINDOWS = (2, 4, 8, 16)
POOL_C = D_POOL // len(POOL_WINDOWS)
D_FF = 2816
N_EXPERTS = 8
EPS = 1e-6
NEG = -1e30

LANES = 128
SUBLANES = 8
VMEM_LIMIT = 56 * 1024 * 1024
Q_ROWS = 4
KEY_ROWS = Q_ROWS + WIN_R - 1
HALO = 8
FF_CHUNKS = 1
FF_CHUNK = D_FF // FF_CHUNKS
MOE_SRC = 256
MOE_TILE = 256
MOE_RING = 8
MOE_RING_SIZES = (4, 5, 6, MOE_RING)
MOE_CTILE = 128
MOE_CMAX = (MOE_SRC - 1 + MOE_CTILE - 1) // MOE_CTILE + 1


def _cparams(sem):
    return pltpu.CompilerParams(dimension_semantics=sem, vmem_limit_bytes=VMEM_LIMIT)


def _const_spec(shape):
    nd = len(shape)
    return pl.BlockSpec(shape, lambda *_: (0,) * nd, pipeline_mode=pl.Buffered(1))


def _dot(a, b):
    return jnp.dot(a, b, preferred_element_type=F32)


def _dot_nt(a, b):
    return lax.dot_general(a, b, (((1,), (1,)), ((), ())), preferred_element_type=F32)


def _rms(x, g):
    ms = jnp.mean(x * x, axis=-1, keepdims=True)
    return x * lax.rsqrt(ms + EPS) * g


def _pre(x, g, shift, scale):
    return _rms(x, g) * (1.0 + scale) + shift


def _gelu(x):
    return jax.nn.gelu(x, approximate=True)


def _sigmoid(x):
    return 1.0 / (1.0 + jnp.exp(-x))


def _silu(x):
    return x * _sigmoid(x)


def _mod_kernel(c_ref, w_ref, b_ref, o_ref):
    s = _silu(c_ref[...])
    o_ref[0] = jnp.dot(s, w_ref[0], preferred_element_type=F32,
                       precision=lax.Precision.HIGHEST) + b_ref[0]


def _modulation(cond, w_ada, b_ada):
    depth, d, n = w_ada.shape
    rows = cond.shape[0]
    bn = 768
    return pl.pallas_call(
        _mod_kernel,
        grid=(depth, n // bn),
        in_specs=[
            pl.BlockSpec((rows, d), lambda i, j: (0, 0)),
            pl.BlockSpec((1, d, bn), lambda i, j: (i, 0, j)),
            pl.BlockSpec((1, 1, bn), lambda i, j: (i, 0, j)),
        ],
        out_specs=pl.BlockSpec((1, rows, bn), lambda i, j: (i, 0, j)),
        out_shape=jax.ShapeDtypeStruct((depth, rows, n), F32),
        compiler_params=_cparams(("arbitrary", "arbitrary")),
        name="adaln_modulation",
    )(cond, w_ada, b_ada.reshape(depth, 1, n))


def _l0_in_kernel(x_ref, mod_ref, g_ref, w_ref, ws_ref, bs_ref, *out_refs, tm, emit_f32):
    q_ref, k_ref, v_ref, sg_ref = out_refs[:4]
    x = x_ref[...]
    h = _pre(x, g_ref[0:1, :], mod_ref[0, 0:1, :], mod_ref[0, 1:2, :]).astype(BF16)
    z = _dot(h, w_ref[...])
    q_ref[...] = z[:, 0:W_A].astype(BF16)
    k_ref[...] = z[:, W_A:2 * W_A].astype(BF16)
    v_ref[...] = z[:, 2 * W_A:3 * W_A].astype(BF16)
    if emit_f32:
        out_refs[4][...] = z[:, W_A:2 * W_A]
        out_refs[5][...] = z[:, 2 * W_A:3 * W_A]
    u = _gelu(z[:, 3 * W_A:3 * W_A + W_B])
    gf = _gelu(z[:, 3 * W_A + W_B:])
    n_chunks = tm // CHUNK
    for gi in range(N_GROUPS_B):
        gg = gf[:, gi * LANES:(gi + 1) * LANES]
        mu = jnp.mean(gg, axis=-1, keepdims=True)
        dd = gg - mu
        var = jnp.mean(dd * dd, axis=-1, keepdims=True)
        gn = (dd * lax.rsqrt(var + EPS)).astype(BF16)
        rhs = jnp.concatenate([gn[c * CHUNK:(c + 1) * CHUNK, :] for c in range(n_chunks)], axis=1)
        s = _dot(ws_ref[gi], rhs) + bs_ref[:, gi:gi + 1]
        for c in range(n_chunks):
            uu = u[c * CHUNK:(c + 1) * CHUNK, gi * LANES:(gi + 1) * LANES]
            sg_ref[c * CHUNK:(c + 1) * CHUNK, gi * LANES:(gi + 1) * LANES] = (
                uu * s[:, c * LANES:(c + 1) * LANES]).astype(BF16)


def _l0_in(x, mod, g, w_in, w_s, b_s_t, *, tm, rows_per_batch, emit_f32):
    rows = x.shape[0]
    tpb = rows_per_batch // tm
    nb = mod.shape[0]
    d_in = w_in.shape[1]
    row_spec = lambda w: pl.BlockSpec((tm, w), lambda i: (i, 0))
    out_shape = [jax.ShapeDtypeStruct((rows, W_A), BF16)] * 3 + [jax.ShapeDtypeStruct((rows, W_B), BF16)]
    out_specs = [row_spec(W_A)] * 3 + [row_spec(W_B)]
    if emit_f32:
        out_shape += [jax.ShapeDtypeStruct((rows, W_A), F32)] * 2
        out_specs += [row_spec(W_A)] * 2
    return pl.pallas_call(
        functools.partial(_l0_in_kernel, tm=tm, emit_f32=emit_f32),
        grid=(rows // tm,),
        in_specs=[
            row_spec(D_MODEL),
            pl.BlockSpec((1, 6, D_MODEL), lambda i: ((i // tpb) % nb, 0, 0)),
            _const_spec((4, D_MODEL)),
            _const_spec((D_MODEL, d_in)),
            _const_spec((N_GROUPS_B, CHUNK, CHUNK)),
            _const_spec((CHUNK, N_GROUPS_B)),
        ],
        out_specs=out_specs,
        out_shape=out_shape,
        compiler_params=_cparams(("arbitrary",)),
        name="l0_in_proj_sgu",
    )(x, mod, g, w_in, w_s, b_s_t)


def _attn_kernel(*refs, n_local, has_ctx, rows):
    if has_ctx:
        q_ref, k_ref, v_ref, kc_ref, vc_ref, bias_ref, o_ref = refs
        r0 = pl.program_id(1) * Q_ROWS
        ks = jnp.clip(r0 - WIN_R // 2, 0, rows - KEY_ROWS)
        start = pl.multiple_of(ks * GRID_W, GRID_W)
        local = pl.ds(start, n_local)
    else:
        q_ref, k_ref, v_ref, o_ref = refs
        local = slice(None)
    first = lax.broadcasted_iota(jnp.int32, (1, LANES), 1) < HEAD_DIM
    for hp in range(W_A // LANES):
        lanes = slice(hp * LANES, (hp + 1) * LANES)
        kl = k_ref[local, lanes]
        vl = v_ref[local, lanes]
        q = q_ref[:, lanes] * jnp.asarray(HEAD_DIM ** -0.5, BF16)
        outs = []
        for half in range(2):
            qh = jnp.where(first if half == 0 else jnp.logical_not(first), q, jnp.zeros_like(q))
            s = _dot_nt(qh, kl)
            if has_ctx:
                s = s + bias_ref[0, 2 * hp + half]
                sc = _dot_nt(qh, kc_ref[:, lanes])
                m = jnp.maximum(jnp.max(s, axis=-1, keepdims=True), jnp.max(sc, axis=-1, keepdims=True))
            else:
                m = jnp.max(s, axis=-1, keepdims=True)
            p = jnp.exp(s - m)
            l = jnp.sum(p, axis=-1, keepdims=True)
            acc = _dot(p.astype(BF16), vl)
            if has_ctx:
                pc = jnp.exp(sc - m)
                l = l + jnp.sum(pc, axis=-1, keepdims=True)
                acc = acc + _dot(pc.astype(BF16), vc_ref[:, lanes])
            outs.append(acc / l)
        o_ref[:, lanes] = jnp.where(first, outs[0], outs[1]).astype(BF16)


def _attn_ctx(q, k, v, *, n_batch, n_seq):
    rows = q.shape[0]
    spec = pl.BlockSpec((n_seq, W_A), lambda b: (b, 0))
    return pl.pallas_call(
        functools.partial(_attn_kernel, n_local=n_seq, has_ctx=False, rows=0),
        grid=(n_batch,),
        in_specs=[spec, spec, spec],
        out_specs=spec,
        out_shape=jax.ShapeDtypeStruct((rows, W_A), BF16),
        compiler_params=_cparams(("arbitrary",)),
        name="attn_context",
    )(q, k, v)


def _attn_latent(q, k, v, kc, vc, bias, *, n_batch, n_seq, n_ctx):
    rows = n_seq // GRID_W
    n_rg = rows // Q_ROWS
    tq = Q_ROWS * GRID_W
    n_local = KEY_ROWS * GRID_W
    img_spec = pl.BlockSpec((n_seq, W_A), lambda b, rg: (b, 0))
    ctx_spec = pl.BlockSpec((n_ctx, W_A), lambda b, rg: (b, 0))
    q_spec = pl.BlockSpec((tq, W_A), lambda b, rg: (b * n_rg + rg, 0))

    def bias_map(b, rg):
        cfg = jnp.where(rg == 0, 0, jnp.where(rg == n_rg - 1, 2, 1))
        return (cfg, 0, 0, 0)

    return pl.pallas_call(
        functools.partial(_attn_kernel, n_local=n_local, has_ctx=True, rows=rows),
        grid=(n_batch, n_rg),
        in_specs=[q_spec, img_spec, img_spec, ctx_spec, ctx_spec,
                  pl.BlockSpec((1, N_HEADS, tq, n_local), bias_map)],
        out_specs=q_spec,
        out_shape=jax.ShapeDtypeStruct((n_batch * n_seq, W_A), BF16),
        compiler_params=_cparams(("arbitrary", "arbitrary")),
        name="attn_latent",
    )(q, k, v, kc, vc, bias)


def _latent_bias_table(rpb, rows):
    n_heads = rpb.shape[0]
    qcol = np.arange(GRID_W)[:, None]
    kcol = np.arange(GRID_W)[None, :]
    cs = np.clip(qcol - WIN_C // 2, 0, GRID_W - WIN_C)
    col_valid = (kcol >= cs) & (kcol < cs + WIN_C)
    dc = np.clip(kcol - qcol + (WIN_C - 1), 0, 2 * WIN_C - 2)
    pick = (dc[:, :, None] == np.arange(2 * WIN_C - 1)).astype(np.float32)
    col_tab = jnp.einsum("hrc,qkc->hrqk", rpb.astype(F32), pick, precision=lax.Precision.HIGHEST)
    col_tab = jnp.where(col_valid[None, None], col_tab, NEG)
    tables = []
    for r0 in (0, 2 * Q_ROWS, rows - Q_ROWS):
        ks = min(max(r0 - WIN_R // 2, 0), rows - KEY_ROWS)
        qrow = r0 + np.arange(Q_ROWS)[:, None]
        krow = ks + np.arange(KEY_ROWS)[None, :]
        rs = np.clip(qrow - WIN_R // 2, 0, rows - WIN_R)
        row_valid = (krow >= rs) & (krow < rs + WIN_R)
        dr = np.clip(krow - qrow + (WIN_R - 1), 0, 2 * WIN_R - 2)
        blocks = [jnp.stack([col_tab[:, dr[i, k]] if row_valid[i, k]
                             else jnp.full((n_heads, GRID_W, GRID_W), NEG, F32)
                             for k in range(KEY_ROWS)], axis=2)
                  for i in range(Q_ROWS)]
        b = jnp.stack(blocks, axis=1)
        tables.append(b.reshape(n_heads, Q_ROWS * GRID_W, KEY_ROWS * GRID_W))
    return jnp.stack(tables)


def _out_kernel(*refs, n_in):
    in_refs = refs[:n_in]
    x_ref, mod_ref, g_ref, w_ref, rw_ref, x1_ref, h2_ref, gates_ref, rankc_ref, rankr_ref = refs[n_in:]
    y = None
    off = 0
    for r in in_refs:
        w = r.shape[1]
        part = _dot(r[...], w_ref[off:off + w, :])
        y = part if y is None else y + part
        off += w
    x1 = x_ref[...] + mod_ref[0, 2:3, :] * _rms(y, g_ref[1:2, :])
    x1_ref[...] = x1
    h2 = _pre(x1, g_ref[2:3, :], mod_ref[0, 3:4, :], mod_ref[0, 4:5, :])
    h2_ref[...] = h2.astype(BF16)
    for sb in range(h2.shape[0] // MOE_SRC):
        blk = slice(sb * MOE_SRC, (sb + 1) * MOE_SRC)
        hs = h2[blk, :]
        h_hi = hs.astype(BF16)
        h_lo = (hs - h_hi.astype(F32)).astype(BF16)
        rw = rw_ref[...]
        r_hi = rw.astype(BF16)
        r_lo = (rw - r_hi.astype(F32)).astype(BF16)
        logits = _dot_nt(r_hi, h_hi) + (_dot_nt(r_hi, h_lo) + _dot_nt(r_lo, h_hi))
        tm = MOE_SRC
        row = lax.broadcasted_iota(jnp.int32, logits.shape, 0).astype(F32)
        logits = jnp.where(row < N_EXPERTS, logits, -jnp.inf)
        m1 = jnp.max(logits, axis=0, keepdims=True)
        i1 = jnp.min(jnp.where(logits == m1, row, float(LANES)), axis=0, keepdims=True)
        rest_l = jnp.where(row == i1, -jnp.inf, logits)
        m2 = jnp.max(rest_l, axis=0, keepdims=True)
        i2 = jnp.min(jnp.where(rest_l == m2, row, float(LANES)), axis=0, keepdims=True)
        e2 = jnp.exp(m2 - m1)
        w1 = 1.0 / (1.0 + e2)
        w2 = e2 / (1.0 + e2)
        sel1 = row == i1
        sel2 = row == i2
        member = jnp.where(sel1, 1.0, 0.0) + jnp.where(sel2, 1.0, 0.0)
        gates = jnp.where(sel1, w1, 0.0) + jnp.where(sel2, w2, 0.0)
        before = (lax.broadcasted_iota(jnp.int32, (tm, tm), 0)
                  < lax.broadcasted_iota(jnp.int32, (tm, tm), 1))
        rank = _dot(member.astype(BF16), jnp.where(before, 1.0, 0.0).astype(BF16))
        rank = jnp.where(member > 0.0, rank, -1.0)
        gates_ref[blk, :] = gates.T
        rankc_ref[blk, :] = rank.T.astype(jnp.int32)
        rankr_ref[:, blk] = rank[0:N_EXPERTS, :].astype(jnp.int32)


def _out_proj(ins, x, mod, g, w_out, router_w, *, tm, rows_per_batch):
    rows = x.shape[0]
    tpb = rows_per_batch // tm
    nb = mod.shape[0]
    row_spec = lambda w: pl.BlockSpec((tm, w), lambda i: (i, 0))
    in_specs = [row_spec(a.shape[1]) for a in ins] + [
        row_spec(D_MODEL),
        pl.BlockSpec((1, 6, D_MODEL), lambda i: ((i // tpb) % nb, 0, 0)),
        _const_spec((4, D_MODEL)),
        _const_spec((D_MODEL, D_MODEL)),
        _const_spec((LANES, D_MODEL)),
    ]
    out_shape = [jax.ShapeDtypeStruct((rows, D_MODEL), F32), jax.ShapeDtypeStruct((rows, D_MODEL), BF16),
                 jax.ShapeDtypeStruct((rows, LANES), F32),
                 jax.ShapeDtypeStruct((rows, LANES), jnp.int32),
                 jax.ShapeDtypeStruct((N_EXPERTS, rows), jnp.int32)]
    out_specs = [row_spec(D_MODEL), row_spec(D_MODEL), row_spec(LANES), row_spec(LANES),
                 pl.BlockSpec((N_EXPERTS, tm), lambda i: (0, i))]
    return pl.pallas_call(
        functools.partial(_out_kernel, n_in=len(ins)),
        grid=(rows // tm,),
        in_specs=in_specs,
        out_specs=out_specs,
        out_shape=out_shape,
        compiler_params=_cparams(("arbitrary",)),
        name="out_proj_norms_router",
    )(*ins, x, mod, g, w_out, router_w)


def _l0_tail_kernel(oa_ref, sg_ref, x_ref, mod_ref, g_ref, wo_ref, wg_ref, wu_ref, wd_ref,
                    mod1_ref, g1_ref, wi_ref, x2_ref, gate_ref, xr_ref, xq_ref, x1_scr, h2_scr, acc_ref):
    fc = pl.program_id(1)

    @pl.when(fc == 0)
    def _():
        y = _dot(oa_ref[...], wo_ref[0:W_A, :]) + _dot(sg_ref[...], wo_ref[W_A:, :])
        x1 = x_ref[...] + mod_ref[0, 2:3, :] * _rms(y, g_ref[1:2, :])
        x1_scr[...] = x1
        h2_scr[...] = _pre(x1, g_ref[2:3, :], mod_ref[0, 3:4, :], mod_ref[0, 4:5, :]).astype(BF16)
        acc_ref[...] = jnp.zeros_like(acc_ref)

    h = h2_scr[...]
    t = _silu(_dot(h, wg_ref[...])) * _dot(h, wu_ref[...])
    acc_ref[...] += _dot(t.astype(BF16), wd_ref[...])

    @pl.when(fc == FF_CHUNKS - 1)
    def _():
        x2 = x1_scr[...] + mod_ref[0, 5:6, :] * _rms(acc_ref[...], g_ref[3:4, :])
        x2_ref[...] = x2
        hn = _pre(x2, g1_ref[0:1, :], mod1_ref[0, 0:1, :], mod1_ref[0, 1:2, :]).astype(BF16)
        z = _dot(hn, wi_ref[...])
        gate_ref[...] = z[:, 0:D_RNN].astype(BF16)
        xr_ref[...] = z[:, D_RNN:2 * D_RNN]
        xq_ref[...] = z[:, 2 * D_RNN:]


def _l0_tail(oa, sg, x, mod, g, w_out, wg, wu, wd, mod1, g1, w_in1, *, tm, rows_per_batch):
    rows = x.shape[0]
    tpb = rows_per_batch // tm
    nb = mod.shape[0]
    row_spec = lambda w: pl.BlockSpec((tm, w), lambda i, f: (i, 0))
    mod_spec = pl.BlockSpec((1, 6, D_MODEL), lambda i, f: ((i // tpb) % nb, 0, 0))
    return pl.pallas_call(
        _l0_tail_kernel,
        grid=(rows // tm, FF_CHUNKS),
        in_specs=[row_spec(W_A), row_spec(W_B), row_spec(D_MODEL), mod_spec,
                  _const_spec((4, D_MODEL)), _const_spec((W_A + W_B, D_MODEL)),
                  pl.BlockSpec((D_MODEL, FF_CHUNK), lambda i, f: (0, f)),
                  pl.BlockSpec((D_MODEL, FF_CHUNK), lambda i, f: (0, f)),
                  pl.BlockSpec((FF_CHUNK, D_MODEL), lambda i, f: (f, 0)),
                  mod_spec, _const_spec((4, D_MODEL)), _const_spec((D_MODEL, w_in1.shape[1]))],
        out_specs=[row_spec(D_MODEL), row_spec(D_RNN), row_spec(D_RNN), row_spec(D_POOL)],
        out_shape=[jax.ShapeDtypeStruct((rows, D_MODEL), F32),
                   jax.ShapeDtypeStruct((rows, D_RNN), BF16),
                   jax.ShapeDtypeStruct((rows, D_RNN), F32),
                   jax.ShapeDtypeStruct((rows, D_POOL), F32)],
        scratch_shapes=[pltpu.VMEM((tm, D_MODEL), F32), pltpu.VMEM((tm, D_MODEL), BF16),
                        pltpu.VMEM((tm, D_MODEL), F32)],
        compiler_params=_cparams(("arbitrary", "arbitrary")),
        name="l0_tail_l1_in_proj",
    )(oa, sg, x, mod, g, w_out, wg, wu, wd, mod1, g1, w_in1)


def _moe_plan(rank_row):
    n_tok = rank_row.shape[1]
    n_src = n_tok // MOE_SRC
    n_tiles = 2 * n_tok // MOE_TILE + N_EXPERTS
    member = rank_row >= 0
    cnt_blk = member.reshape(N_EXPERTS, n_src, MOE_SRC).sum(-1).astype(jnp.int32)
    cum = jnp.concatenate([jnp.zeros((N_EXPERTS, 1), jnp.int32), jnp.cumsum(cnt_blk, axis=1)], axis=1)
    cnt = cum[:, -1]
    tiles_e = (cnt + MOE_TILE - 1) // MOE_TILE
    tile_end = jnp.cumsum(tiles_e)
    tile_base = tile_end - tiles_e
    slot0 = tile_base[:, None] * MOE_TILE + cum[:, :-1]
    pos_row = jnp.where(member, jnp.repeat(slot0, MOE_SRC, axis=1) + rank_row, -1)
    pos_row = jnp.pad(pos_row, ((0, 0), (0, MOE_RING * MOE_SRC)), constant_values=-1)
    d = jnp.arange(n_tiles, dtype=jnp.int32)
    tile_e = jnp.minimum(jnp.sum(d[:, None] >= tile_end[None, :], axis=1), N_EXPERTS - 1).astype(jnp.int32)
    valid = d < tile_end[-1]
    lo_slot = (d - tile_base[tile_e]) * MOE_TILE
    hi_slot = jnp.minimum(lo_slot + MOE_TILE, cnt[tile_e])
    cum_d = cum[tile_e]
    tile_lo = jnp.sum(cum_d[:, 1:] <= lo_slot[:, None], axis=1).astype(jnp.int32)
    tile_hi = jnp.sum(cum_d[:, :-1] < hi_slot[:, None], axis=1).astype(jnp.int32) - 1
    tile_n = jnp.where(valid, tile_hi - tile_lo + 1, 0).astype(jnp.int32)
    tile_lo = jnp.where(valid, tile_lo, 0)
    first = slot0.T
    c_t0 = first // MOE_CTILE
    c_nt = jnp.where(cnt_blk.T > 0, (first + cnt_blk.T - 1) // MOE_CTILE - c_t0 + 1, 0)
    flat = lambda a: a.reshape(-1).astype(jnp.int32)
    return pos_row, tile_e, tile_lo, tile_n, flat(c_t0), flat(c_nt), flat(first), n_tiles


def _moe_ffn_kernel(te_ref, tl_ref, tn_ref, pos_ref, h_hbm, wg_ref, wu_ref, wd_ref, o_ref,
                    hbuf, sem, acc_ref, *, n_tiles):
    d = pl.program_id(0)
    e = te_ref[d]
    lo = tl_ref[d]
    n = tn_ref[d]

    def copy(blk, slot):
        start = pl.multiple_of(blk * MOE_SRC, MOE_SRC)
        return pltpu.make_async_copy(h_hbm.at[pl.ds(start, MOE_SRC), :], hbuf.at[slot], sem.at[slot])

    def start_first(first_blk, count):
        for k in range(MOE_RING):
            @pl.when(k < count)
            def _(k=k):
                copy(first_blk + k, k).start()

    @pl.when(d == 0)
    def _():
        hbuf[...] = jnp.zeros_like(hbuf)
        start_first(lo, n)

    slot_ids = d * MOE_TILE + lax.broadcasted_iota(jnp.int32, (MOE_TILE, 1), 0)

    def picks(first_blk, n_blk):
        start = pl.multiple_of(first_blk * MOE_SRC, MOE_SRC)
        pos = pos_ref[pl.ds(e, 1), pl.ds(start, n_blk * MOE_SRC)]
        return jnp.where(pos == slot_ids, 1.0, 0.0).astype(BF16)

    for k in range(MOE_RING):
        @pl.when(k < n)
        def _(k=k):
            copy(lo + k, k).wait()

    prev = 0
    for n_blk in MOE_RING_SIZES:
        last = n_blk == MOE_RING_SIZES[-1]

        @pl.when((n > prev) if last else ((n > prev) & (n <= n_blk)))
        def _(n_blk=n_blk):
            acc_ref[...] = _dot(picks(lo, n_blk), hbuf[0:n_blk].reshape(n_blk * MOE_SRC, D_MODEL))

        prev = n_blk

    def extra(j, carry):
        cp = copy(lo + j, 0)
        cp.start()
        cp.wait()
        acc_ref[...] += _dot(picks(lo + j, 1), hbuf[0])
        return carry

    lax.fori_loop(MOE_RING, n, extra, 0)

    @pl.when(d + 1 < n_tiles)
    def _():
        start_first(tl_ref[d + 1], tn_ref[d + 1])

    @pl.when(n > 0)
    def _():
        x = acc_ref[...].astype(BF16)
        y = None
        for fc in range(FF_CHUNKS):
            cols = slice(fc * FF_CHUNK, (fc + 1) * FF_CHUNK)
            t = _silu(_dot(x, wg_ref[0, :, cols])) * _dot(x, wu_ref[0, :, cols])
            part = _dot(t.astype(BF16), wd_ref[0, cols, :])
            y = part if y is None else y + part
        o_ref[...] = y.astype(BF16)

    @pl.when(n == 0)
    def _():
        o_ref[...] = jnp.zeros_like(o_ref)


def _moe_ffn(h2, pos_row, tile_e, tile_lo, tile_n, wg, wu, wd, *, n_tiles):
    n_tok = h2.shape[0]
    w_spec = lambda shape: pl.BlockSpec((1,) + shape, lambda d, te, tl, tn: (te[d], 0, 0))
    grid_spec = pltpu.PrefetchScalarGridSpec(
        num_scalar_prefetch=3,
        grid=(n_tiles,),
        in_specs=[
            pl.BlockSpec(pos_row.shape, lambda d, te, tl, tn: (0, 0), pipeline_mode=pl.Buffered(1)),
            pl.BlockSpec(memory_space=pl.ANY),
            w_spec((D_MODEL, D_FF)), w_spec((D_MODEL, D_FF)), w_spec((D_FF, D_MODEL)),
        ],
        out_specs=pl.BlockSpec((MOE_TILE, D_MODEL), lambda d, te, tl, tn: (d, 0)),
        scratch_shapes=[pltpu.VMEM((MOE_RING, MOE_SRC, D_MODEL), BF16),
                        pltpu.SemaphoreType.DMA((MOE_RING,)),
                        pltpu.VMEM((MOE_TILE, D_MODEL), F32)],
    )
    return pl.pallas_call(
        functools.partial(_moe_ffn_kernel, n_tiles=n_tiles),
        grid_spec=grid_spec,
        out_shape=jax.ShapeDtypeStruct((n_tiles * MOE_TILE, D_MODEL), BF16),
        compiler_params=_cparams(("arbitrary",)),
        name="moe_dispatch_swiglu",
    )(tile_e, tile_lo, tile_n, pos_row, h2, wg, wu, wd)


def _moe_combine_kernel(t0_ref, nt_ref, first_ref, rank_ref, gates_ref, ys_hbm, x1_ref, mod_ref, g_ref,
                        o_ref, ybuf, sem):
    s = pl.program_id(0)

    def copy(blk, e, k):
        start = pl.multiple_of((t0_ref[blk * N_EXPERTS + e] + k) * MOE_CTILE, MOE_CTILE)
        buf = (blk % 2) * N_EXPERTS + e
        return pltpu.make_async_copy(ys_hbm.at[pl.ds(start, MOE_CTILE), :],
                                     ybuf.at[buf, pl.ds(k * MOE_CTILE, MOE_CTILE), :],
                                     sem.at[buf * MOE_CMAX + k])

    def start_block(blk):
        for e in range(N_EXPERTS):
            for k in range(MOE_CMAX):
                @pl.when(nt_ref[blk * N_EXPERTS + e] > k)
                def _(e=e, k=k):
                    copy(blk, e, k).start()

    @pl.when(s == 0)
    def _():
        ybuf[...] = jnp.zeros_like(ybuf)
        start_block(s)

    @pl.when(s + 1 < pl.num_programs(0))
    def _():
        start_block(s + 1)

    most = nt_ref[s * N_EXPERTS]
    for e in range(N_EXPERTS):
        most = jnp.maximum(most, nt_ref[s * N_EXPERTS + e])
        for k in range(MOE_CMAX):
            @pl.when(nt_ref[s * N_EXPERTS + e] > k)
            def _(e=e, k=k):
                copy(s, e, k).wait()

    def combine(n_t):
        lane = lax.broadcasted_iota(jnp.int32, (1, n_t * MOE_CTILE), 1)
        y = None
        for e in range(N_EXPERTS):
            rank = rank_ref[:, e:e + 1]
            pos = jnp.where(rank >= 0, rank + first_ref[s * N_EXPERTS + e], -1)
            base = t0_ref[s * N_EXPERTS + e] * MOE_CTILE
            pick = jnp.where(pos == base + lane, 1.0, 0.0).astype(BF16)
            part = gates_ref[:, e:e + 1] * _dot(
                pick, ybuf[(s % 2) * N_EXPERTS + e, 0:n_t * MOE_CTILE, :])
            y = part if y is None else y + part
        o_ref[...] = x1_ref[...] + mod_ref[0, 5:6, :] * _rms(y, g_ref[3:4, :])

    @pl.when(most < MOE_CMAX)
    def _():
        combine(MOE_CMAX - 1)

    @pl.when(most >= MOE_CMAX)
    def _():
        combine(MOE_CMAX)


def _moe_combine(ys, rank_col, gates, c_t0, c_nt, c_first, x1, mod, g, *, rows_per_batch):
    n_tok = x1.shape[0]
    tpb = rows_per_batch // MOE_SRC
    nb = mod.shape[0]
    row_spec = lambda w: pl.BlockSpec((MOE_SRC, w), lambda s, *_: (s, 0))
    grid_spec = pltpu.PrefetchScalarGridSpec(
        num_scalar_prefetch=3,
        grid=(n_tok // MOE_SRC,),
        in_specs=[
            row_spec(LANES), row_spec(LANES),
            pl.BlockSpec(memory_space=pl.ANY),
            row_spec(D_MODEL),
            pl.BlockSpec((1, 6, D_MODEL), lambda s, *_: ((s // tpb) % nb, 0, 0)),
            pl.BlockSpec((4, D_MODEL), lambda s, *_: (0, 0), pipeline_mode=pl.Buffered(1)),
        ],
        out_specs=row_spec(D_MODEL),
        scratch_shapes=[pltpu.VMEM((2 * N_EXPERTS, MOE_CMAX * MOE_CTILE, D_MODEL), BF16),
                        pltpu.SemaphoreType.DMA((2 * N_EXPERTS * MOE_CMAX,))],
    )
    return pl.pallas_call(
        _moe_combine_kernel,
        grid_spec=grid_spec,
        out_shape=jax.ShapeDtypeStruct((n_tok, D_MODEL), F32),
        compiler_params=_cparams(("arbitrary",)),
        name="moe_combine_post",
    )(c_t0, c_nt, c_first, rank_col, gates, ys, x1, mod, g)


def _to_time_major(x_ref, prev_ref, next_ref, dst, t_idx, n_tiles, tt):
    nb = SUBLANES
    for bi in range(nb):
        prev = jnp.where(t_idx > 0, prev_ref[bi, 0], 0.0)
        nxt = jnp.where(t_idx < n_tiles - 1, next_ref[bi, 0], 0.0)
        x = x_ref[bi]
        for lc in range(dst.shape[0]):
            lanes = slice(lc * LANES, (lc + 1) * LANES)
            dst[lc, pl.ds(bi, HALO, stride=nb), :] = prev[:, lanes]
            dst[lc, pl.ds(HALO * nb + bi, tt, stride=nb), :] = x[:, lanes]
            dst[lc, pl.ds((HALO + tt) * nb + bi, HALO, stride=nb), :] = nxt[:, lanes]


def _scan_kernel(*refs, fwd, tt, n_tiles, n_seq):
    if fwd:
        (xr_ref, xrp_ref, xrn_ref, xq_ref, xqp_ref, xqn_ref, cw_ref, cb_ref, wgt_ref, ba_ref, bx_ref,
         lam_ref, h0_ref, pw_ref, ps_ref, hf_ref, yp_ref, hl_ref,
         xt_scr, a_scr, b_scr, h_scr, carry, qt_scr, yp_scr) = refs
    else:
        (xr_ref, xrp_ref, xrn_ref, cw_ref, cb_ref, wgt_ref, ba_ref, bx_ref, lam_ref, h0_ref,
         hf_ref, gate_ref, yr_ref, hl_ref, xt_scr, a_scr, b_scr, h_scr, carry) = refs
    i = pl.program_id(1)
    t_idx = i if fwd else n_tiles - 1 - i
    nb = SUBLANES
    n_lc = D_RNN // LANES
    rows = tt * nb

    _to_time_major(xr_ref, xrp_ref, xrn_ref, xt_scr, t_idx, n_tiles, tt)
    chunks = []
    for lc in range(n_lc):
        lanes = slice(lc * LANES, (lc + 1) * LANES)
        acc = cb_ref[:, lanes]
        for j in range(CONV_W):
            acc = acc + cw_ref[j:j + 1, lanes] * xt_scr[lc, pl.ds((HALO + j - CONV_LEFT) * nb, rows), :]
        chunks.append(acc)
    xc = jnp.concatenate(chunks, axis=1)
    xb = xc.astype(BF16)
    r0 = _dot(xb[:, :RNN_HALF], wgt_ref[0])
    r1 = _dot(xb[:, RNN_HALF:], wgt_ref[1])
    ra = jnp.concatenate([r0[:, :RNN_HALF], r1[:, :RNN_HALF]], axis=1) + ba_ref[...]
    ri = jnp.concatenate([r0[:, RNN_HALF:], r1[:, RNN_HALF:]], axis=1) + bx_ref[...]
    nl = -lam_ref[...]
    softplus = jnp.maximum(nl, 0.0) + jnp.log1p(jnp.exp(-jnp.abs(nl)))
    a = jnp.exp((-0.5 * RG_C * softplus) * (1.0 + jnp.tanh(0.5 * ra)))
    b = jnp.sqrt(1.0 - a * a) * ((0.5 + 0.5 * jnp.tanh(0.5 * ri)) * xc)
    for lc in range(n_lc):
        a_scr[lc] = a[:, lc * LANES:(lc + 1) * LANES]
        b_scr[lc] = b[:, lc * LANES:(lc + 1) * LANES]

    @pl.when(i == 0)
    def _():
        for lc in range(n_lc):
            carry[lc] = h0_ref[:, lc * LANES:(lc + 1) * LANES]

    def step(s, h):
        t = s if fwd else tt - 1 - s
        row = pl.multiple_of(t * nb, nb)
        h = a_scr[:, pl.ds(row, nb), :] * h + b_scr[:, pl.ds(row, nb), :]
        h_scr[:, pl.ds(row, nb), :] = h
        return h

    h_last = lax.fori_loop(0, tt, step, carry[...], unroll=8)
    carry[...] = h_last
    for lc in range(n_lc):
        hl_ref[:, lc * LANES:(lc + 1) * LANES] = h_last[lc]

    def unscan(bi):
        return jnp.concatenate([h_scr[lc, pl.ds(bi, tt, stride=nb), :] for lc in range(n_lc)], axis=1)

    if fwd:
        for bi in range(nb):
            hf_ref[bi] = unscan(bi)
        _to_time_major(xq_ref, xqp_ref, xqn_ref, qt_scr, t_idx, n_tiles, tt)
        length = tt + 2 * HALO
        lane = lax.broadcasted_iota(jnp.int32, (1, LANES), 1)
        tg = t_idx * tt + lax.broadcasted_iota(jnp.int32, (rows, 1), 0) // nb
        deltas = []
        for lc in range(D_POOL // LANES):
            e = qt_scr[lc]
            sums = {1: e}
            for w in POOL_WINDOWS:
                p = sums[w // 2]
                n_out = length - w + 1
                sums[w] = p[0:n_out * nb] + p[(w // 2) * nb:(n_out + w // 2) * nb]
            w_a, w_b = POOL_WINDOWS[2 * lc], POOL_WINDOWS[2 * lc + 1]
            first = lane < POOL_C
            pick = lambda w: sums[w][(HALO - w // 2) * nb:(HALO - w // 2) * nb + rows]
            wsum = jnp.where(first, pick(w_a), pick(w_b))
            half = jnp.where(first, w_a // 2, w_b // 2)
            cnt = (jnp.minimum(tg + half, n_seq) - jnp.maximum(tg - half, 0)).astype(F32)
            deltas.append((wsum / cnt - e[HALO * nb:HALO * nb + rows]).astype(BF16))
        yp = _dot(jnp.concatenate(deltas, axis=1), pw_ref[...]) * ps_ref[...]
        for lc in range(D_POOL // LANES):
            yp_scr[lc] = yp[:, lc * LANES:(lc + 1) * LANES]
        for bi in range(nb):
            yp_ref[bi] = jnp.concatenate(
                [yp_scr[lc, pl.ds(bi, tt, stride=nb), :] for lc in range(D_POOL // LANES)],
                axis=1).astype(BF16)
    else:
        for bi in range(nb):
            hb = unscan(bi)
            yr_ref[bi] = ((hf_ref[bi] + hb) * _gelu(gate_ref[bi].astype(F32))).astype(BF16)


def _scan(fwd, xr, xq, conv_w, conv_b, w_gates, ba, bx, lam, h0, pool_w, pool_scale, h_f, gate, *, tt):
    n_batch, n_seq, _ = xr.shape
    n_bg = n_batch // SUBLANES
    n_tiles = n_seq // tt
    hb = tt // HALO
    n_hb = n_seq // HALO

    def t_of(i):
        return i if fwd else n_tiles - 1 - i

    def tile(c):
        return pl.BlockSpec((SUBLANES, tt, c), lambda b, i: (b, t_of(i), 0))

    def prev(c):
        return pl.BlockSpec((SUBLANES, 1, HALO, c),
                            lambda b, i: (b, jnp.maximum(t_of(i) * hb - 1, 0), 0, 0))

    def nxt(c):
        return pl.BlockSpec((SUBLANES, 1, HALO, c),
                            lambda b, i: (b, jnp.minimum((t_of(i) + 1) * hb, n_hb - 1), 0, 0))

    state = pl.BlockSpec((SUBLANES, D_RNN), lambda b, i: (b, 0))
    xr4 = xr.reshape(n_batch, n_hb, HALO, D_RNN)
    common = [_const_spec((CONV_W, D_RNN)), _const_spec((1, D_RNN)),
              _const_spec((2, RNN_HALF, D_RNN)), _const_spec((1, D_RNN)), _const_spec((1, D_RNN)),
              _const_spec((1, D_RNN)), state]
    common_args = [conv_w, conv_b, w_gates, ba, bx, lam, h0]
    n_lc = D_RNN // LANES
    ext_rows = (tt + 2 * HALO) * SUBLANES
    scratch = ([pltpu.VMEM((n_lc, ext_rows, LANES), F32)]
               + [pltpu.VMEM((n_lc, SUBLANES * tt, LANES), F32)] * 3
               + [pltpu.VMEM((n_lc, SUBLANES, LANES), F32)])
    if fwd:
        scratch += [pltpu.VMEM((D_POOL // LANES, ext_rows, LANES), F32),
                    pltpu.VMEM((D_POOL // LANES, SUBLANES * tt, LANES), F32)]
        xq4 = xq.reshape(n_batch, n_hb, HALO, D_POOL)
        in_specs = ([tile(D_RNN), prev(D_RNN), nxt(D_RNN), tile(D_POOL), prev(D_POOL), nxt(D_POOL)]
                    + common + [_const_spec((D_POOL, D_POOL)), _const_spec((1, D_POOL))])
        args = [xr, xr4, xr4, xq, xq4, xq4] + common_args + [pool_w, pool_scale]
        out_specs = [tile(D_RNN), tile(D_POOL), state]
        out_shape = [jax.ShapeDtypeStruct((n_batch, n_seq, D_RNN), F32),
                     jax.ShapeDtypeStruct((n_batch, n_seq, D_POOL), BF16),
                     jax.ShapeDtypeStruct((n_batch, D_RNN), F32)]
    else:
        in_specs = [tile(D_RNN), prev(D_RNN), nxt(D_RNN)] + common + [tile(D_RNN), tile(D_RNN)]
        args = [xr, xr4, xr4] + common_args + [h_f, gate]
        out_specs = [tile(D_RNN), state]
        out_shape = [jax.ShapeDtypeStruct((n_batch, n_seq, D_RNN), BF16),
                     jax.ShapeDtypeStruct((n_batch, D_RNN), F32)]
    return pl.pallas_call(
        functools.partial(_scan_kernel, fwd=fwd, tt=tt, n_tiles=n_tiles, n_seq=n_seq),
        grid=(n_bg, n_tiles),
        in_specs=in_specs,
        out_specs=out_specs,
        out_shape=out_shape,
        scratch_shapes=scratch,
        compiler_params=_cparams(("arbitrary", "arbitrary")),
        name="rglru_forward_pool" if fwd else "rglru_backward_combine",
    )(*args)


def _gate_weights(wa, wx):
    per_half = RNN_BLOCKS // 2

    def half_dense(w, k):
        blocks = [w[k * per_half + b] for b in range(per_half)]
        rows = []
        for bi, blk in enumerate(blocks):
            row = [blk if bj == bi else jnp.zeros_like(blk) for bj in range(per_half)]
            rows.append(jnp.concatenate(row, axis=1))
        return jnp.concatenate(rows, axis=0)

    return jnp.stack([jnp.concatenate([half_dense(wa, k), half_dense(wx, k)], axis=1)
                      for k in range(2)]).astype(BF16)


def _pool_weights(pool_w):
    n = pool_w.shape[0]
    rows = []
    for i in range(n):
        rows.append(jnp.concatenate([pool_w[i] if j == i else jnp.zeros_like(pool_w[i])
                                     for j in range(n)], axis=1))
    return jnp.concatenate(rows, axis=0).astype(BF16)


def kernel(x_prompt, x_sample, cache_k, cache_v, state_h, c, c_ctx, w_ada, b_ada, norm_g, w_in_even,
           w_out_even, na_rpb, sgu_w, sgu_b, w_in_odd, w_out_odd, conv_w, conv_b, rg_wa, rg_ba, rg_wx,
           rg_bx, rg_lam, pool_w, pool_scale, ffn_wg, ffn_wu, ffn_wd, router_w, moe_wg, moe_wu, moe_wd):
    bp, n_p, _ = x_prompt.shape
    bs, n_s, _ = x_sample.shape
    n_ctx = cache_k.shape[3]

    cond = jnp.concatenate([c_ctx[None, :], c, jnp.zeros((2 * SUBLANES - 1 - bs, D_MODEL), F32)], axis=0)
    mods = _modulation(cond, w_ada, b_ada).reshape(w_ada.shape[0], cond.shape[0], 6, D_MODEL)

    streams = [
        dict(x=x_prompt.reshape(bp * n_p, D_MODEL), nb=bp, n=n_p, tm=256, latent=False),
        dict(x=x_sample.reshape(bs * n_s, D_MODEL), nb=bs, n=n_s, tm=512, latent=True),
    ]
    new_k = new_v = new_h = None

    g = norm_g[0]
    w_in = w_in_even[0].astype(BF16)
    w_out = w_out_even[0].astype(BF16)
    w_s = sgu_w[0].astype(BF16)
    b_s_t = sgu_b[0].T
    wg, wu, wd = (w[0].astype(BF16) for w in (ffn_wg, ffn_wu, ffn_wd))
    w_in1 = w_in_odd[0].astype(BF16)
    bias = _latent_bias_table(na_rpb[0], n_s // GRID_W)
    kc = cache_k[:, 0].transpose(0, 2, 1, 3).reshape(bs * n_ctx, W_A).astype(BF16)
    vc = cache_v[:, 0].transpose(0, 2, 1, 3).reshape(bs * n_ctx, W_A).astype(BF16)
    for st in streams:
        mod = mods[0, 1:1 + bs] if st["latent"] else mods[0, 0:1]
        outs = _l0_in(st["x"], mod, g, w_in, w_s, b_s_t, tm=st["tm"], rows_per_batch=st["n"],
                      emit_f32=not st["latent"])
        q, k, v, sg = outs[:4]
        if st["latent"]:
            oa = _attn_latent(q, k, v, kc, vc, bias, n_batch=st["nb"], n_seq=st["n"], n_ctx=n_ctx)
        else:
            oa = _attn_ctx(q, k, v, n_batch=st["nb"], n_seq=st["n"])
            heads = lambda a: a.reshape(bp, n_p, N_HEADS, HEAD_DIM).transpose(0, 2, 1, 3)[:, None]
            new_k, new_v = heads(outs[4]), heads(outs[5])
        mod1 = mods[1, 1:1 + bs] if st["latent"] else mods[1, 0:1]
        st["x"], st["gate"], st["xr"], st["xq"] = _l0_tail(
            oa, sg, st["x"], mod, g, w_out, wg, wu, wd, mod1, norm_g[1], w_in1,
            tm=st["tm"], rows_per_batch=st["n"])

    g = norm_g[1]
    w_out = w_out_odd[0].astype(BF16)
    w_gates = [_gate_weights(rg_wa[0, d], rg_wx[0, d]) for d in range(2)]
    pw = _pool_weights(pool_w[0])
    ps = pool_scale[0][None, :]
    rw = jnp.pad(router_w[0].T, ((0, LANES - N_EXPERTS), (0, 0)))
    wg, wu, wd = (w[0].astype(BF16) for w in (moe_wg, moe_wu, moe_wd))
    for st in streams:
        mod = mods[1, 1:1 + bs] if st["latent"] else mods[1, 0:1]
        nb, n = st["nb"], st["n"]
        gate, xr, xq = (st[key].reshape(nb, n, -1) for key in ("gate", "xr", "xq"))
        h0 = state_h[:, 0].astype(F32) if st["latent"] else jnp.zeros((nb, 2, D_RNN), F32)
        scan_args = lambda d: (conv_w[0], conv_b[0][None, :], w_gates[d], rg_ba[0, d][None, :],
                               rg_bx[0, d][None, :], rg_lam[0, d][None, :], h0[:, d])
        h_f, y_pool, h_f_last = _scan(True, xr, xq, *scan_args(0), pw, ps, None, None, tt=128)
        y_rec, h_b_last = _scan(False, xr, None, *scan_args(1), None, None, h_f, gate, tt=128)
        if not st["latent"]:
            new_h = jnp.stack([h_f_last, h_b_last], axis=1)[:, None]
        x1, h2, gates, rank_col, rank_row = _out_proj(
            [y_rec.reshape(nb * n, D_RNN), y_pool.reshape(nb * n, D_POOL)],
            st["x"], mod, g, w_out, rw, tm=st["tm"], rows_per_batch=n)
        pos_row, tile_e, tile_lo, tile_n, c_t0, c_nt, c_first, n_tiles = _moe_plan(rank_row)
        ys = _moe_ffn(h2, pos_row, tile_e, tile_lo, tile_n, wg, wu, wd, n_tiles=n_tiles)
        st["x"] = _moe_combine(ys, rank_col, gates, c_t0, c_nt, c_first, x1, mod, g, rows_per_batch=n)

    y_prompt = streams[0]["x"].reshape(bp, n_p, D_MODEL)
    y_sample = streams[1]["x"].reshape(bs, n_s, D_MODEL)
    return (y_prompt, y_sample, new_k, new_v, new_h)
```
